```python
import jax, jax.numpy as jnp
from jax import lax
import numpy as np

D_MODEL = 2048
BATCH = 8
SEQ = 2048
DEPTH = 2

GLA_W = 3 * D_MODEL // 8
MLSTM_W = 3 * D_MODEL // 8
SGU_W = D_MODEL - GLA_W - MLSTM_W

GLA_HEADS = 4
GLA_DV = GLA_W // GLA_HEADS
GLA_DK = GLA_DV // 2
GLA_GATE_RANK = 16
GLA_GATE_TAU = 16.0

MLSTM_HEADS = 4
MLSTM_DV = MLSTM_W // MLSTM_HEADS
MLSTM_DK = MLSTM_DV // 2
CONV_WIDTH = 4

SGU_GROUPS = 4
SGU_CH = SGU_W // SGU_GROUPS
SGU_BLOCK = 128

REC_CHUNK = 64
D_FF = -(-8 * D_MODEL // 768) * 256
EPS = 1e-6

kernel_name = "hymba_style_gla_mlstm_gmlp_hybrid"


def _proj_sizes():
    gk = GLA_HEADS * GLA_DK
    mk = MLSTM_HEADS * MLSTM_DK
    return (gk, gk, GLA_W, GLA_W, GLA_GATE_RANK,
            mk, mk, MLSTM_W, MLSTM_W, MLSTM_HEADS, MLSTM_HEADS,
            SGU_W, SGU_W)


def rms_norm(x, g):
    xf = x.astype(jnp.float32)
    y = xf * lax.rsqrt(jnp.mean(xf * xf, axis=-1, keepdims=True) + EPS)
    return (y * g.astype(jnp.float32)).astype(x.dtype)


def to_chunks(t, c):
    b, s, h, d = t.shape
    return t.reshape(b, s // c, c, h, d).transpose(1, 0, 3, 2, 4)


def from_chunks(t):
    n, b, h, c, d = t.shape
    return t.transpose(1, 0, 3, 2, 4).reshape(b, n * c, h, d)


def gla_chunked(q, k, v, log_a):
    b, s, h, dk = q.shape
    dv = v.shape[-1]
    mask = jnp.tril(jnp.ones((REC_CHUNK, REC_CHUNK), dtype=bool))

    def step(state, inp):
        qb, kb, vb, ab = inp
        cum = jnp.cumsum(ab, axis=2)
        diff = cum[:, :, :, None, :] - cum[:, :, None, :, :]
        decay = jnp.exp(jnp.where(mask[:, :, None], diff, -jnp.inf))
        attn = jnp.einsum('bhtsd,bhsd->bhts', qb[:, :, :, None, :] * decay, kb)
        o = (jnp.einsum('bhts,bhsv->bhtv', attn, vb)
             + jnp.einsum('bhtd,bhdv->bhtv', qb * jnp.exp(cum), state))
        last = cum[:, :, -1:, :]
        k_dec = kb * jnp.exp(last - cum)
        state = state * jnp.exp(last[:, :, 0, :, None]) + jnp.einsum('bhsd,bhsv->bhdv', k_dec, vb)
        return state, o

    s0 = jnp.zeros((b, h, dk, dv), jnp.float32)
    _, o = lax.scan(step, s0, tuple(to_chunks(t, REC_CHUNK) for t in (q, k, v, log_a)))
    return from_chunks(o)


def mlstm_chunked(q, k, v, i_pre, log_f):
    b, s, h, dk = q.shape
    dv = v.shape[-1]
    n = s // REC_CHUNK
    mask = jnp.tril(jnp.ones((REC_CHUNK, REC_CHUNK), dtype=bool))
    gate_chunks = lambda g: g.reshape(b, n, REC_CHUNK, h).transpose(1, 0, 3, 2)

    def step(carry, inp):
        c_st, n_st, m_st = carry
        qb, kb, vb, ib, fb = inp
        cum = jnp.cumsum(fb, axis=-1)
        dmat = jnp.where(mask, cum[..., :, None] - cum[..., None, :] + ib[..., None, :], -jnp.inf)
        inter = cum + m_st[..., None]
        m_t = jnp.maximum(inter, jnp.max(dmat, axis=-1))
        w = jnp.exp(dmat - m_t[..., None])
        sc_inter = jnp.exp(inter - m_t)
        sc = jnp.einsum('bhtd,bhsd->bhts', qb, kb) * w
        num = (jnp.einsum('bhts,bhsv->bhtv', sc, vb)
               + sc_inter[..., None] * jnp.einsum('bhtd,bhdv->bhtv', qb, c_st))
        den = jnp.sum(sc, axis=-1) + sc_inter * jnp.einsum('bhtd,bhd->bht', qb, n_st)
        h_out = num / jnp.maximum(jnp.abs(den), jnp.exp(-m_t))[..., None]
        total = cum[..., -1]
        g = total[..., None] - cum + ib
        m_new = jnp.maximum(total + m_st, jnp.max(g, axis=-1))
        wj = jnp.exp(g - m_new[..., None])
        dprev = jnp.exp(total + m_st - m_new)
        c_new = dprev[..., None, None] * c_st + jnp.einsum('bhs,bhsd,bhsv->bhdv', wj, kb, vb)
        n_new = dprev[..., None] * n_st + jnp.einsum('bhs,bhsd->bhd', wj, kb)
        return (c_new, n_new, m_new), h_out

    init = (jnp.zeros((b, h, dk, dv), jnp.float32),
            jnp.zeros((b, h, dk), jnp.float32),
            jnp.zeros((b, h), jnp.float32))
    xs = (to_chunks(q, REC_CHUNK), to_chunks(k, REC_CHUNK), to_chunks(v, REC_CHUNK),
          gate_chunks(i_pre), gate_chunks(log_f))
    _, hs = lax.scan(step, init, xs)
    return from_chunks(hs)


def causal_dwconv(x, w):
    kw = w.shape[0]
    s = x.shape[1]
    xp = jnp.pad(x, ((0, 0), (kw - 1, 0), (0, 0)))
    return sum(xp[:, j:j + s] * w[j] for j in range(kw))


def spatial_gating(u, v, w_s, b_s, ln_g, ln_b):
    b, s, _ = v.shape
    vf = v.astype(jnp.float32)
    mu = jnp.mean(vf, axis=-1, keepdims=True)
    var = jnp.mean(jnp.square(vf - mu), axis=-1, keepdims=True)
    vn = ((vf - mu) * lax.rsqrt(var + EPS) * ln_g + ln_b).astype(v.dtype)
    vb = vn.reshape(b, s // SGU_BLOCK, SGU_BLOCK, SGU_GROUPS, SGU_CH)
    w_causal = jnp.tril(w_s)
    mixed = jnp.einsum('gts,bnsgc->bntgc', w_causal, vb) + b_s.T[None, None, :, :, None]
    return u * mixed.reshape(b, s, SGU_W)


def hybrid_layer(x, norm_mix, w_in, gla_a2, gla_ab, gla_norm, ml_conv, ml_ib, ml_fb, ml_norm,
                 sgu_ln_g, sgu_ln_b, sgu_w, sgu_b, w_out, norm_ffn, w_gu, w_down):
    f32 = jnp.float32
    b, s, _ = x.shape
    h = rms_norm(x, norm_mix)
    proj = h @ w_in
    idx = np.cumsum(_proj_sizes())[:-1].tolist()
    (gq, gk, gv, gg, ga1, mq, mk, mv, mo, mi, mf, su, sv) = jnp.split(proj, idx, axis=-1)

    q = gq.reshape(b, s, GLA_HEADS, GLA_DK).astype(f32) * (GLA_DK ** -0.5)
    k = gk.reshape(b, s, GLA_HEADS, GLA_DK).astype(f32)
    v = gv.reshape(b, s, GLA_HEADS, GLA_DV).astype(f32)
    log_a = jax.nn.log_sigmoid((ga1 @ gla_a2 + gla_ab).astype(f32)) / GLA_GATE_TAU
    log_a = log_a.reshape(b, s, GLA_HEADS, GLA_DK)
    o_gla = gla_chunked(q, k, v, log_a)
    o_gla = rms_norm(o_gla, gla_norm.reshape(GLA_HEADS, GLA_DV)).reshape(b, s, GLA_W)
    out_a = o_gla * jax.nn.silu(gg.astype(f32))

    mqk = jax.nn.silu(causal_dwconv(jnp.concatenate([mq, mk], axis=-1), ml_conv))
    mq2, mk2 = jnp.split(mqk, 2, axis=-1)
    qm = mq2.reshape(b, s, MLSTM_HEADS, MLSTM_DK).astype(f32)
    km = mk2.reshape(b, s, MLSTM_HEADS, MLSTM_DK).astype(f32) * (MLSTM_DK ** -0.5)
    vm = mv.reshape(b, s, MLSTM_HEADS, MLSTM_DV).astype(f32)
    i_pre = (mi + ml_ib).astype(f32)
    log_f = jax.nn.log_sigmoid((mf + ml_fb).astype(f32))
    o_ml = mlstm_chunked(qm, km, vm, i_pre, log_f)
    o_ml = rms_norm(o_ml, ml_norm.reshape(MLSTM_HEADS, MLSTM_DV)).reshape(b, s, MLSTM_W)
    out_b = o_ml * jax.nn.sigmoid(mo.astype(f32))

    out_c = spatial_gating(jax.nn.gelu(su), jax.nn.gelu(sv), sgu_w, sgu_b, sgu_ln_g, sgu_ln_b)

    mix = jnp.concatenate([out_a.astype(x.dtype), out_b.astype(x.dtype), out_c.astype(x.dtype)], axis=-1)
    x = x + mix @ w_out

    hf = rms_norm(x, norm_ffn)
    gate, up = jnp.split(hf @ w_gu, 2, axis=-1)
    return x + (jax.nn.silu(gate) * up) @ w_down


def setup_inputs(seed: int = 0) -> dict:
    key = jax.random.key(seed)
    ks = jax.random.split(key, 20)
    L, D = DEPTH, D_MODEL
    nrm = lambda k, shape, scale: jax.random.normal(k, shape, jnp.float32) * scale
    p = sum(_proj_sizes())
    gk = GLA_HEADS * GLA_DK
    mk = MLSTM_HEADS * MLSTM_DK
    return {
        "x": nrm(ks[0], (BATCH, SEQ, D), 1.0),
        "norm_mix": 1.0 + nrm(ks[1], (L, D), 0.1),
        "w_in": nrm(ks[2], (L, D, p), D ** -0.5),
        "gla_a2": nrm(ks[3], (L, GLA_GATE_RANK, gk), GLA_GATE_RANK ** -0.5),
        "gla_ab": nrm(ks[4], (L, gk), 0.1),
        "gla_norm": 1.0 + nrm(ks[5], (L, GLA_W), 0.1),
        "ml_conv": nrm(ks[6], (L, CONV_WIDTH, 2 * mk), CONV_WIDTH ** -0.5),
        "ml_ib": nrm(ks[7], (L, MLSTM_HEADS), 0.1),
        "ml_fb": 3.0 + 3.0 * jax.random.uniform(ks[8], (L, MLSTM_HEADS), jnp.float32),
        "ml_norm": 1.0 + nrm(ks[9], (L, MLSTM_W), 0.1),
        "sgu_ln_g": 1.0 + nrm(ks[10], (L, SGU_W), 0.1),
        "sgu_ln_b": nrm(ks[11], (L, SGU_W), 0.1),
        "sgu_w": nrm(ks[12], (L, SGU_GROUPS, SGU_BLOCK, SGU_BLOCK), SGU_BLOCK ** -0.5),
        "sgu_b": 1.0 + nrm(ks[13], (L, SGU_GROUPS, SGU_BLOCK), 0.1),
        "w_out": nrm(ks[14], (L, D, D), D ** -0.5),
        "norm_ffn": 1.0 + nrm(ks[15], (L, D), 0.1),
        "w_gu": nrm(ks[16], (L, D, 2 * D_FF), D ** -0.5),
        "w_down": nrm(ks[17], (L, D_FF, D), D_FF ** -0.5),
        "norm_final": 1.0 + nrm(ks[18], (D,), 0.1),
    }


def reference(x, norm_mix, w_in, gla_a2, gla_ab, gla_norm, ml_conv, ml_ib, ml_fb, ml_norm,
              sgu_ln_g, sgu_ln_b, sgu_w, sgu_b, w_out, norm_ffn, w_gu, w_down, norm_final):
    for l in range(DEPTH):
        x = hybrid_layer(x, norm_mix[l], w_in[l], gla_a2[l], gla_ab[l], gla_norm[l],
                         ml_conv[l], ml_ib[l], ml_fb[l], ml_norm[l],
                         sgu_ln_g[l], sgu_ln_b[l], sgu_w[l], sgu_b[l],
                         w_out[l], norm_ffn[l], w_gu[l], w_down[l])
    return rms_norm(x, norm_final)
```

```python
import functools

import numpy as np
import jax
import jax.numpy as jnp
from jax import lax
from jax.experimental import pallas as pl
from jax.experimental.pallas import tpu as pltpu

F32 = jnp.float32
BF16 = jnp.bfloat16

D_MODEL = 2048
HEADS = 4
DK = 96
DKP = 128
DV = 192
QK_W = HEADS * DKP
V_W = HEADS * DV
PAIR_K = 2 * DKP
PAIR_V = 2 * DV
SGU_W = 512
SGU_GROUPS = 4
SGU_CH = 128
GATE_RANK = 16
GATE_TAU = 16.0
CONV_WIDTH = 4
D_FF = 5632
EPS = 1e-6
CHUNK = 128
N_LEVELS = 7
LANE = 128

OFF_GLA_Q, OFF_GLA_K, OFF_ML_Q, OFF_ML_K = 0, 512, 1024, 1536
OFF_SGU_U, OFF_SGU_V = 2048, 2560
OFF_GLA_V, OFF_GLA_G, OFF_ML_V, OFF_ML_O = 3072, 3840, 4608, 5376
OFF_GLA_A1, OFF_ML_IF = 6144, 6272
PROJ_W = 6400

VMEM_LIMIT = 56 * 1024 * 1024


def _proj_column_map():
    gk = HEADS * DK
    sizes = (gk, gk, V_W, V_W, GATE_RANK, gk, gk, V_W, V_W, HEADS, HEADS, SGU_W, SGU_W)
    starts = np.concatenate([[0], np.cumsum(sizes)[:-1]])
    (gq, gkk, gv, gg, ga1, mq, mk, mv, mo, mi, mf, su, sv) = starts.tolist()
    src = np.full((PROJ_W,), -1, np.int64)

    def heads(dst, s):
        for h in range(HEADS):
            src[dst + h * DKP: dst + h * DKP + DK] = np.arange(s + h * DK, s + (h + 1) * DK)

    def plain(dst, s, n):
        src[dst: dst + n] = np.arange(s, s + n)

    heads(OFF_GLA_Q, gq)
    heads(OFF_GLA_K, gkk)
    heads(OFF_ML_Q, mq)
    heads(OFF_ML_K, mk)
    plain(OFF_SGU_U, su, SGU_W)
    plain(OFF_SGU_V, sv, SGU_W)
    plain(OFF_GLA_V, gv, V_W)
    plain(OFF_GLA_G, gg, V_W)
    plain(OFF_ML_V, mv, V_W)
    plain(OFF_ML_O, mo, V_W)
    plain(OFF_GLA_A1, ga1, GATE_RANK)
    plain(OFF_ML_IF, mi, HEADS)
    plain(OFF_ML_IF + HEADS, mf, HEADS)
    return src


def _pad_heads(a):
    lead = a.shape[:-1]
    a = a.reshape(lead + (HEADS, DK))
    a = jnp.pad(a, [(0, 0)] * len(lead) + [(0, 0), (0, DKP - DK)])
    return a.reshape(lead + (QK_W,))


def _gla_level_matrix():
    c = CHUNK
    t = np.arange(c)[:, None]
    u = np.arange(c)[None, :]
    blocks = []
    for l in range(N_LEVELS):
        m = 1 << l
        seg_t, seg_u = t // m, u // m
        q_role = (seg_t % 2 == 1) & (seg_u == seg_t) & (u <= t)
        k_role = (seg_t % 2 == 0) & (seg_u == seg_t) & (u > t)
        blocks.append(q_role | k_role)
    blocks.append(u <= t)
    blocks.append(u > t)
    return np.concatenate(blocks, axis=0).astype(np.float32)


def _pair_level_matrix():
    c = CHUNK
    t = np.arange(c)[:, None]
    s = np.arange(c)[None, :]
    x = np.bitwise_xor(t, s)
    lvl = np.floor(np.log2(np.maximum(x, 1))).astype(np.int32)
    lvl = np.where(s == t, -1, lvl)
    lvl = np.where(s > t, -2, lvl)
    return lvl.astype(np.int32)


def _dot(a, b):
    return jnp.dot(a, b, preferred_element_type=F32)


def _dot_nt(a, b):
    return lax.dot_general(a, b, (((1,), (1,)), ((), ())), preferred_element_type=F32)


def _split3(x):
    hi = x.astype(BF16)
    r = x - hi.astype(F32)
    mid = r.astype(BF16)
    lo = (r - mid.astype(F32)).astype(BF16)
    return hi, mid, lo


def _log_sigmoid(x):
    return jnp.minimum(x, 0.0) - jnp.log1p(jnp.exp(-jnp.abs(x)))


def _head_rms_scale(o, lo_mask):
    o2 = o * o
    ss0 = jnp.sum(jnp.where(lo_mask, o2, 0.0), axis=-1, keepdims=True)
    ss1 = jnp.sum(jnp.where(lo_mask, 0.0, o2), axis=-1, keepdims=True)
    return jnp.where(lo_mask, lax.rsqrt(ss0 / DV + EPS), lax.rsqrt(ss1 / DV + EPS))


def _in_proj_kernel(x_ref, g_ref, w_ref, o_ref, h_ref):
    @pl.when(pl.program_id(1) == 0)
    def _():
        x = x_ref[...]
        ms = jnp.mean(x * x, axis=-1, keepdims=True)
        h_ref[...] = (x * lax.rsqrt(ms + EPS) * g_ref[...]).astype(BF16)

    o_ref[...] = _dot(h_ref[...], w_ref[...])


def _in_proj(x2, g, w, tm, tn):
    t, d = x2.shape
    p = w.shape[1]
    return pl.pallas_call(
        _in_proj_kernel,
        grid=(t // tm, p // tn),
        in_specs=[
            pl.BlockSpec((tm, d), lambda i, j: (i, 0)),
            pl.BlockSpec((1, d), lambda i, j: (0, 0)),
            pl.BlockSpec((d, tn), lambda i, j: (0, j)),
        ],
        out_specs=pl.BlockSpec((tm, tn), lambda i, j: (i, j)),
        out_shape=jax.ShapeDtypeStruct((t, p), F32),
        scratch_shapes=[pltpu.VMEM((tm, d), BF16)],
        compiler_params=pltpu.CompilerParams(
            dimension_semantics=("parallel", "arbitrary"), vmem_limit_bytes=VMEM_LIMIT),
        name="in_proj",
    )(x2, g, w)


def _gla_kernel(q_ref, k_ref, v_ref, g_ref, a1_ref, a2_ref, ab_ref, nrm_ref, amat_ref, lvl_ref,
                o_ref, s_ref, e_ref):
    @pl.when(pl.program_id(1) == 0)
    def _():
        s_ref[...] = jnp.zeros_like(s_ref)

    q = q_ref[...] * (DK ** -0.5)
    k = k_ref[...]
    z = _dot(a1_ref[...].astype(BF16), a2_ref[...]) + ab_ref[...]
    log_a = _log_sigmoid(z) * (1.0 / GATE_TAU)
    amat = amat_ref[...]
    hi, mid, lo = _split3(log_a)
    e_ref[...] = _dot(amat, hi) + _dot(amat, mid) + _dot(amat, lo)

    lvl = lvl_ref[...]
    qb = q.astype(BF16)
    kb = k.astype(BF16)
    attn = []
    for h in range(HEADS):
        hs = slice(h * DKP, (h + 1) * DKP)
        attn.append(jnp.where(lvl == -1, _dot_nt(qb[:, hs], kb[:, hs]), 0.0))
    for l in range(N_LEVELS):
        e = jnp.exp(e_ref[l * CHUNK:(l + 1) * CHUNK, :])
        xq = (q * e).astype(BF16)
        yk = (k * e).astype(BF16)
        for h in range(HEADS):
            hs = slice(h * DKP, (h + 1) * DKP)
            attn[h] = jnp.where(lvl == l, _dot_nt(xq[:, hs], yk[:, hs]), attn[h])

    cum = e_ref[N_LEVELS * CHUNK:(N_LEVELS + 1) * CHUNK, :]
    q_dec = (q * jnp.exp(cum)).astype(BF16)
    k_dec = k * jnp.exp(e_ref[(N_LEVELS + 1) * CHUNK:(N_LEVELS + 2) * CHUNK, :])
    decay_all = jnp.exp(cum[CHUNK - 1:CHUNK, :])

    lane = lax.broadcasted_iota(jnp.int32, (CHUNK, PAIR_V), 1)
    lo_mask = lane < DV
    row_s = lax.broadcasted_iota(jnp.int32, (PAIR_K, PAIR_V), 0)
    lane_s = lax.broadcasted_iota(jnp.int32, (PAIR_K, PAIR_V), 1)
    block_mask = jnp.where(row_s < DKP, 0, 1) == jnp.where(lane_s < DV, 0, 1)

    for p in range(HEADS // 2):
        ks = slice(p * PAIR_K, (p + 1) * PAIR_K)
        vs = slice(p * PAIR_V, (p + 1) * PAIR_V)
        vb = v_ref[:, vs].astype(BF16)
        zero = jnp.zeros_like(vb)
        v_blk = jnp.concatenate([jnp.where(lo_mask, vb, zero), jnp.where(lo_mask, zero, vb)], axis=0)
        state = s_ref[p]
        lhs = jnp.concatenate([attn[2 * p].astype(BF16), attn[2 * p + 1].astype(BF16), q_dec[:, ks]],
                              axis=1)
        rhs = jnp.concatenate([v_blk, state.astype(BF16)], axis=0)
        o = _dot(lhs, rhs)

        upd = _dot(k_dec[:, ks].T.astype(BF16), vb)
        dcols = []
        for h in (2 * p, 2 * p + 1):
            d_row = jnp.broadcast_to(decay_all[:, h * DKP:(h + 1) * DKP], (DKP, DKP))
            d_col = d_row.T
            dcols.append(jnp.concatenate([d_col] * (PAIR_V // LANE), axis=1))
        s_ref[p] = state * jnp.concatenate(dcols, axis=0) + jnp.where(block_mask, upd, 0.0)

        gate = g_ref[:, vs]
        out = o * _head_rms_scale(o, lo_mask) * nrm_ref[:, vs] * (gate * jax.nn.sigmoid(gate))
        o_ref[:, vs] = out.astype(o_ref.dtype)


def _gla(proj, a2p, abp, nrm, amat, lvl, batch, seq):
    nc = seq // CHUNK
    tok = lambda w, cb: pl.BlockSpec((CHUNK, w), lambda b, c: (b * nc + c, cb))
    const = lambda shape: pl.BlockSpec(shape, lambda b, c: (0,) * len(shape))
    return pl.pallas_call(
        _gla_kernel,
        grid=(batch, nc),
        in_specs=[
            tok(QK_W, OFF_GLA_Q // QK_W),
            tok(QK_W, OFF_GLA_K // QK_W),
            tok(V_W, OFF_GLA_V // V_W),
            tok(V_W, OFF_GLA_G // V_W),
            tok(LANE, OFF_GLA_A1 // LANE),
            const((LANE, QK_W)),
            const((1, QK_W)),
            const((1, V_W)),
            const(((N_LEVELS + 2) * CHUNK, CHUNK)),
            const((CHUNK, CHUNK)),
        ],
        out_specs=pl.BlockSpec((CHUNK, V_W), lambda b, c: (b * nc + c, 0)),
        out_shape=jax.ShapeDtypeStruct((batch * seq, V_W), BF16),
        scratch_shapes=[
            pltpu.VMEM((HEADS // 2, PAIR_K, PAIR_V), F32),
            pltpu.VMEM(((N_LEVELS + 2) * CHUNK, QK_W), F32),
        ],
        compiler_params=pltpu.CompilerParams(
            dimension_semantics=("parallel", "arbitrary"), vmem_limit_bytes=VMEM_LIMIT),
        name="gla",
    )(proj, proj, proj, proj, proj, a2p, abp, nrm, amat, lvl)


ST_W = PAIR_V + LANE
TAIL = 8


def _mlstm_kernel(q_ref, k_ref, v_ref, og_ref, if_ref, cw_ref, ifb_ref, nrm_ref, ltri_ref, lvl_ref,
                  o_ref, c_ref, m_ref, xe_ref):
    @pl.when(pl.program_id(1) == 0)
    def _():
        c_ref[...] = jnp.zeros_like(c_ref)
        m_ref[...] = jnp.zeros_like(m_ref)
        xe_ref[0:TAIL, :] = jnp.zeros((TAIL, 2 * QK_W), F32)

    xe_ref[TAIL:TAIL + CHUNK, 0:QK_W] = q_ref[...]
    xe_ref[TAIL:TAIL + CHUNK, QK_W:2 * QK_W] = k_ref[...]
    y = jnp.zeros((CHUNK, 2 * QK_W), F32)
    for j in range(CONV_WIDTH):
        y = y + cw_ref[j:j + 1, :] * xe_ref[pl.ds(TAIL - (CONV_WIDTH - 1) + j, CHUNK), :]
    xe_ref[0:TAIL, :] = xe_ref[CHUNK:CHUNK + TAIL, :]
    y = y * jax.nn.sigmoid(y)
    qm = y[:, 0:QK_W]
    km = y[:, QK_W:2 * QK_W] * (DK ** -0.5)
    qb = qm.astype(BF16)
    kb = km.astype(BF16)
    km_t = km.T

    slab = if_ref[...] + ifb_ref[...]
    lane_g = lax.broadcasted_iota(jnp.int32, (CHUNK, LANE), 1)
    gates = jnp.where(lane_g < HEADS, slab, _log_sigmoid(slab))
    ltri = ltri_ref[...]
    g_hi, g_mid, g_lo = _split3(gates)
    cum_col = _dot(ltri, g_hi) + _dot(ltri, g_mid) + _dot(ltri, g_lo)
    gates_t = gates.T[0:2 * HEADS, :]
    t_hi, t_mid, t_lo = _split3(gates_t)
    cum_row = _dot_nt(t_hi, ltri) + _dot_nt(t_mid, ltri) + _dot_nt(t_lo, ltri)

    lvl = lvl_ref[...]
    causal = lvl >= -1
    lane = lax.broadcasted_iota(jnp.int32, (CHUNK, PAIR_V), 1)
    lo_mask = lane < DV
    row_s = lax.broadcasted_iota(jnp.int32, (PAIR_K, ST_W), 0)
    lane_s = lax.broadcasted_iota(jnp.int32, (PAIR_K, ST_W), 1)
    lane_head = jnp.where(lane_s < DV, 0, jnp.where(lane_s < PAIR_V, 1, lane_s - PAIR_V))
    state_mask = jnp.where(row_s < DKP, 0, 1) == lane_head
    lane_e = lax.broadcasted_iota(jnp.int32, (CHUNK, LANE), 1)
    ones_cols = jnp.where(lane_e < 2, 1.0, 0.0).astype(BF16)
    lane_r = lax.broadcasted_iota(jnp.int32, (PAIR_K, LANE), 1)
    row_r = lax.broadcasted_iota(jnp.int32, (PAIR_K, LANE), 0)
    ones_blk = jnp.where(lane_r == jnp.where(row_r < CHUNK, 0, 1), 1.0, 0.0).astype(BF16)

    for p in range(HEADS // 2):
        ks = slice(p * PAIR_K, (p + 1) * PAIR_K)
        vs = slice(p * PAIR_V, (p + 1) * PAIR_V)
        vb = v_ref[:, vs].astype(BF16)
        zero = jnp.zeros_like(vb)
        sc_parts, qs_parts, floor_parts, kw_parts, dprev_parts = [], [], [], [], []
        for h in (2 * p, 2 * p + 1):
            hs = slice(h * DKP, (h + 1) * DKP)
            cc = jnp.broadcast_to(cum_col[:, HEADS + h:HEADS + h + 1], (CHUNK, CHUNK))
            cum_r = cum_row[HEADS + h:HEADS + h + 1, :]
            ib_r = gates_t[h:h + 1, :]
            m_prev = m_ref[h:h + 1, :]
            dmat = jnp.where(causal, cc - cum_r + ib_r, -jnp.inf)
            inter = cc + m_prev
            m_t = jnp.maximum(inter, jnp.max(dmat, axis=-1, keepdims=True))
            w = jnp.exp(dmat - m_t)
            sc_inter = jnp.exp(inter - m_t)
            sc_parts.append((_dot_nt(qb[:, hs], kb[:, hs]) * w).astype(BF16))
            qs_parts.append((qm[:, hs] * sc_inter).astype(BF16))
            floor_parts.append(jnp.exp(-m_t))
            total = cum_r[:, CHUNK - 1:CHUNK]
            g_row = total - cum_r + ib_r
            m_new = jnp.maximum(total + m_prev, jnp.max(g_row, axis=-1, keepdims=True))
            wj = jnp.exp(g_row - m_new)
            dprev_parts.append(jnp.exp(total + m_prev - m_new))
            kw_parts.append((km_t[hs, :] * wj).astype(BF16))
            m_ref[h:h + 1, :] = m_new

        state = c_ref[p]
        st_m = jnp.where(state_mask, state, 0.0).astype(BF16)
        v_blk = jnp.concatenate([jnp.where(lo_mask, vb, zero), jnp.where(lo_mask, zero, vb)], axis=0)
        rhs = jnp.concatenate([jnp.concatenate([v_blk, ones_blk], axis=1), st_m], axis=0)
        lhs = jnp.concatenate(sc_parts + qs_parts, axis=1)
        res = _dot(lhs, rhs)
        num = res[:, 0:PAIR_V]
        den = jnp.where(lo_mask,
                        jnp.broadcast_to(res[:, PAIR_V:PAIR_V + 1], (CHUNK, PAIR_V)),
                        jnp.broadcast_to(res[:, PAIR_V + 1:PAIR_V + 2], (CHUNK, PAIR_V)))
        floor = jnp.where(lo_mask,
                          jnp.concatenate([floor_parts[0]] * (PAIR_V // LANE), axis=1),
                          jnp.concatenate([floor_parts[1]] * (PAIR_V // LANE), axis=1))
        hid = num / jnp.maximum(jnp.abs(den), floor)

        kw = jnp.concatenate(kw_parts, axis=0)
        v_ext = jnp.concatenate([vb, ones_cols], axis=1)
        upd = _dot(kw, v_ext)
        d_rows = jnp.concatenate(
            [jnp.broadcast_to(jnp.concatenate([d] * (ST_W // LANE), axis=1), (DKP, ST_W))
             for d in dprev_parts], axis=0)
        c_ref[p] = d_rows * state + jnp.where(state_mask, upd, 0.0)

        og = og_ref[:, vs]
        out = hid * _head_rms_scale(hid, lo_mask) * nrm_ref[:, vs] * jax.nn.sigmoid(og)
        o_ref[:, vs] = out.astype(o_ref.dtype)


def _mlstm(proj, cw, ifb, nrm, ltri, lvl, batch, seq):
    nc = seq // CHUNK
    tok = lambda w, cb: pl.BlockSpec((CHUNK, w), lambda b, c: (b * nc + c, cb))
    const = lambda shape: pl.BlockSpec(shape, lambda b, c: (0,) * len(shape))
    return pl.pallas_call(
        _mlstm_kernel,
        grid=(batch, nc),
        in_specs=[
            tok(QK_W, OFF_ML_Q // QK_W),
            tok(QK_W, OFF_ML_K // QK_W),
            tok(V_W, OFF_ML_V // V_W),
            tok(V_W, OFF_ML_O // V_W),
            tok(LANE, OFF_ML_IF // LANE),
            const((CONV_WIDTH, 2 * QK_W)),
            const((1, LANE)),
            const((1, V_W)),
            const((CHUNK, CHUNK)),
            const((CHUNK, CHUNK)),
        ],
        out_specs=pl.BlockSpec((CHUNK, V_W), lambda b, c: (b * nc + c, 0)),
        out_shape=jax.ShapeDtypeStruct((batch * seq, V_W), BF16),
        scratch_shapes=[
            pltpu.VMEM((HEADS // 2, PAIR_K, ST_W), F32),
            pltpu.VMEM((2 * HEADS, LANE), F32),
            pltpu.VMEM((TAIL + CHUNK, 2 * QK_W), F32),
        ],
        compiler_params=pltpu.CompilerParams(
            dimension_semantics=("parallel", "arbitrary"), vmem_limit_bytes=VMEM_LIMIT),
        name="mlstm",
    )(proj, proj, proj, proj, proj, cw, ifb, nrm, ltri, lvl)


def _sgu_kernel(u_ref, v_ref, lng_ref, lnb_ref, w_ref, b_ref, lvl_ref, o_ref):
    u = jax.nn.gelu(u_ref[...])
    v = jax.nn.gelu(v_ref[...])
    mu = jnp.mean(v, axis=-1, keepdims=True)
    var = jnp.mean(jnp.square(v - mu), axis=-1, keepdims=True)
    vn = ((v - mu) * lax.rsqrt(var + EPS) * lng_ref[...] + lnb_ref[...]).astype(BF16)
    causal = lvl_ref[...] >= -1
    for g in range(SGU_GROUPS):
        gs = slice(g * SGU_CH, (g + 1) * SGU_CH)
        w = jnp.where(causal, w_ref[g], 0.0).astype(BF16)
        mixed = _dot(w, vn[:, gs]) + b_ref[:, gs]
        o_ref[:, gs] = (u[:, gs] * mixed).astype(o_ref.dtype)


def _sgu(proj, lng, lnb, w, b_full, lvl, batch, seq):
    nb = batch * seq // CHUNK
    tok = lambda w_, cb: pl.BlockSpec((CHUNK, w_), lambda i: (i, cb))
    const = lambda shape: pl.BlockSpec(shape, lambda i: (0,) * len(shape))
    return pl.pallas_call(
        _sgu_kernel,
        grid=(nb,),
        in_specs=[
            tok(SGU_W, OFF_SGU_U // SGU_W),
            tok(SGU_W, OFF_SGU_V // SGU_W),
            const((1, SGU_W)),
            const((1, SGU_W)),
            const((SGU_GROUPS, CHUNK, CHUNK)),
            const((CHUNK, SGU_W)),
            const((CHUNK, CHUNK)),
        ],
        out_specs=pl.BlockSpec((CHUNK, SGU_W), lambda i: (i, 0)),
        out_shape=jax.ShapeDtypeStruct((batch * seq, SGU_W), BF16),
        compiler_params=pltpu.CompilerParams(
            dimension_semantics=("parallel",), vmem_limit_bytes=VMEM_LIMIT),
        name="sgu",
    )(proj, proj, lng, lnb, w, b_full, lvl)


def _out_proj_kernel(x_ref, a_ref, b_ref, c_ref, wa_ref, wb_ref, wc_ref, o_ref):
    o_ref[...] = (x_ref[...] + _dot(a_ref[...], wa_ref[...]) + _dot(b_ref[...], wb_ref[...])
                  + _dot(c_ref[...], wc_ref[...]))


def _out_proj(x2, mix_a, mix_b, mix_c, w_out, tm, tn):
    t, d = x2.shape
    return pl.pallas_call(
        _out_proj_kernel,
        grid=(t // tm, d // tn),
        in_specs=[
            pl.BlockSpec((tm, tn), lambda i, j: (i, j)),
            pl.BlockSpec((tm, V_W), lambda i, j: (i, 0)),
            pl.BlockSpec((tm, V_W), lambda i, j: (i, 0)),
            pl.BlockSpec((tm, SGU_W), lambda i, j: (i, 0)),
            pl.BlockSpec((V_W, tn), lambda i, j: (0, j)),
            pl.BlockSpec((V_W, tn), lambda i, j: (1, j)),
            pl.BlockSpec((SGU_W, tn), lambda i, j: (2 * V_W // SGU_W, j)),
        ],
        out_specs=pl.BlockSpec((tm, tn), lambda i, j: (i, j)),
        out_shape=jax.ShapeDtypeStruct((t, d), F32),
        compiler_params=pltpu.CompilerParams(
            dimension_semantics=("parallel", "arbitrary"), vmem_limit_bytes=VMEM_LIMIT),
        name="out_proj",
    )(x2, mix_a, mix_b, mix_c, w_out, w_out, w_out)


def _ffn_kernel(x_ref, g_ref, wg_ref, wu_ref, wd_ref, gf_ref, o_ref, h_ref, *, final_norm):
    j = pl.program_id(1)

    @pl.when(j == 0)
    def _():
        x = x_ref[...]
        ms = jnp.mean(x * x, axis=-1, keepdims=True)
        h_ref[...] = (x * lax.rsqrt(ms + EPS) * g_ref[...]).astype(BF16)
        o_ref[...] = x

    h = h_ref[...]
    gate = _dot(h, wg_ref[...])
    up = _dot(h, wu_ref[...])
    act = (gate * jax.nn.sigmoid(gate) * up).astype(BF16)
    o_ref[...] += _dot(act, wd_ref[...])

    if final_norm:
        @pl.when(j == pl.num_programs(1) - 1)
        def _():
            y = o_ref[...]
            ms = jnp.mean(y * y, axis=-1, keepdims=True)
            o_ref[...] = y * lax.rsqrt(ms + EPS) * gf_ref[...]


def _ffn(x2, g, w_gu, w_down, g_final, tm, tf, final_norm):
    t, d = x2.shape
    nf = D_FF // tf
    return pl.pallas_call(
        functools.partial(_ffn_kernel, final_norm=final_norm),
        grid=(t // tm, nf),
        in_specs=[
            pl.BlockSpec((tm, d), lambda i, j: (i, 0)),
            pl.BlockSpec((1, d), lambda i, j: (0, 0)),
            pl.BlockSpec((d, tf), lambda i, j: (0, j)),
            pl.BlockSpec((d, tf), lambda i, j: (0, nf + j)),
            pl.BlockSpec((tf, d), lambda i, j: (j, 0)),
            pl.BlockSpec((1, d), lambda i, j: (0, 0)),
        ],
        out_specs=pl.BlockSpec((tm, d), lambda i, j: (i, 0)),
        out_shape=jax.ShapeDtypeStruct((t, d), F32),
        scratch_shapes=[pltpu.VMEM((tm, d), BF16)],
        compiler_params=pltpu.CompilerParams(
            dimension_semantics=("parallel", "arbitrary"), vmem_limit_bytes=VMEM_LIMIT),
        name="ffn",
    )(x2, g, w_gu, w_gu, w_down, g_final)


def _tiles(tokens):
    tm = 512
    while tokens % tm:
        tm //= 2
    return tm


def kernel(x, norm_mix, w_in, gla_a2, gla_ab, gla_norm, ml_conv, ml_ib, ml_fb, ml_norm,
           sgu_ln_g, sgu_ln_b, sgu_w, sgu_b, w_out, norm_ffn, w_gu, w_down, norm_final):
    batch, seq, d = x.shape
    depth = w_in.shape[0]
    tokens = batch * seq
    tm = _tiles(tokens)

    src = _proj_column_map()
    col_idx = jnp.asarray(np.maximum(src, 0), jnp.int32)
    col_valid = jnp.asarray(src >= 0)
    amat = jnp.asarray(_gla_level_matrix(), BF16)
    lvl = jnp.asarray(_pair_level_matrix())
    ltri = jnp.asarray(np.tril(np.ones((CHUNK, CHUNK), np.float32)), BF16)

    xc = x.reshape(tokens, d)
    for l in range(depth):
        w_in_p = jnp.where(col_valid[None, :], jnp.take(w_in[l], col_idx, axis=1), 0.0).astype(BF16)
        proj = _in_proj(xc, norm_mix[l][None, :], w_in_p, tm, 1280)

        a2p = jnp.pad(_pad_heads(gla_a2[l]), ((0, LANE - GATE_RANK), (0, 0))).astype(BF16)
        abp = _pad_heads(gla_ab[l])[None, :]
        mix_a = _gla(proj, a2p, abp, gla_norm[l][None, :], amat, lvl, batch, seq)

        cw = jnp.concatenate([_pad_heads(ml_conv[l][:, :HEADS * DK]),
                              _pad_heads(ml_conv[l][:, HEADS * DK:])], axis=1)
        ifb = jnp.pad(jnp.concatenate([ml_ib[l], ml_fb[l]]), (0, LANE - 2 * HEADS))[None, :]
        mix_b = _mlstm(proj, cw, ifb, ml_norm[l][None, :], ltri, lvl, batch, seq)

        b_full = jnp.repeat(sgu_b[l].T, SGU_CH, axis=1)
        mix_c = _sgu(proj, sgu_ln_g[l][None, :], sgu_ln_b[l][None, :], sgu_w[l], b_full, lvl, batch, seq)

        x1 = _out_proj(xc, mix_a, mix_b, mix_c, w_out[l].astype(BF16), tm, 1024)
        xc = _ffn(x1, norm_ffn[l][None, :], w_gu[l].astype(BF16), w_down[l].astype(BF16),
                  norm_final[None, :], tm, 512, final_norm=(l == depth - 1))
    return xc.reshape(batch, seq, d)
```

```python
import functools

import numpy as np
import jax
import jax.numpy as jnp
from jax import lax
from jax.experimental import pallas as pl
from jax.experimental.pallas import tpu as pltpu

F32 = jnp.float32
BF16 = jnp.bfloat16

D_MODEL = 2048
HEADS = 4
DK = 96
DKP = 128
DV = 192
QK_W = HEADS * DKP
V_W = HEADS * DV
PAIR_K = 2 * DKP
PAIR_V = 2 * DV
SGU_W = 512
SGU_GROUPS = 4
SGU_CH = 128
GATE_RANK = 16
GATE_TAU = 16.0
CONV_WIDTH = 4
D_FF = 5632
EPS = 1e-6
CHUNK = 128
N_LEVELS = 7
LANE = 128

OFF_GLA_Q, OFF_GLA_K, OFF_ML_Q, OFF_ML_K = 0, 512, 1024, 1536
OFF_SGU_U, OFF_SGU_V = 2048, 2560
OFF_GLA_V, OFF_GLA_G, OFF_ML_V, OFF_ML_O = 3072, 3840, 4608, 5376
MAIN_W = 6144
OFF_GLA_A1, OFF_ML_IF = 6144, 6272
GATE_W = 2 * LANE
PROJ_W = MAIN_W + GATE_W
SGU_STEP = 4

VMEM_LIMIT = 56 * 1024 * 1024


def _proj_column_map():
    gk = HEADS * DK
    sizes = (gk, gk, V_W, V_W, GATE_RANK, gk, gk, V_W, V_W, HEADS, HEADS, SGU_W, SGU_W)
    starts = np.concatenate([[0], np.cumsum(sizes)[:-1]])
    (gq, gkk, gv, gg, ga1, mq, mk, mv, mo, mi, mf, su, sv) = starts.tolist()
    src = np.full((PROJ_W,), -1, np.int64)

    def heads(dst, s):
        for h in range(HEADS):
            src[dst + h * DKP: dst + h * DKP + DK] = np.arange(s + h * DK, s + (h + 1) * DK)

    def plain(dst, s, n):
        src[dst: dst + n] = np.arange(s, s + n)

    heads(OFF_GLA_Q, gq)
    heads(OFF_GLA_K, gkk)
    heads(OFF_ML_Q, mq)
    heads(OFF_ML_K, mk)
    plain(OFF_SGU_U, su, SGU_W)
    plain(OFF_SGU_V, sv, SGU_W)
    plain(OFF_GLA_V, gv, V_W)
    plain(OFF_GLA_G, gg, V_W)
    plain(OFF_ML_V, mv, V_W)
    plain(OFF_ML_O, mo, V_W)
    plain(OFF_GLA_A1, ga1, GATE_RANK)
    plain(OFF_ML_IF, mi, HEADS)
    plain(OFF_ML_IF + HEADS, mf, HEADS)
    return src


def _pad_heads(a):
    lead = a.shape[:-1]
    a = a.reshape(lead + (HEADS, DK))
    a = jnp.pad(a, [(0, 0)] * len(lead) + [(0, 0), (0, DKP - DK)])
    return a.reshape(lead + (QK_W,))


def _pair_level_matrix():
    c = CHUNK
    t = np.arange(c)[:, None]
    s = np.arange(c)[None, :]
    x = np.bitwise_xor(t, s)
    lvl = np.floor(np.log2(np.maximum(x, 1))).astype(np.int32)
    lvl = np.where(s == t, -1, lvl)
    lvl = np.where(s > t, -2, lvl)
    return lvl.astype(np.int32)


def _dot(a, b):
    return jnp.dot(a, b, preferred_element_type=F32)


def _dot_nt(a, b):
    return lax.dot_general(a, b, (((1,), (1,)), ((), ())), preferred_element_type=F32)


def _split3(x):
    hi = x.astype(BF16)
    r = x - hi.astype(F32)
    mid = r.astype(BF16)
    lo = (r - mid.astype(F32)).astype(BF16)
    return hi, mid, lo


def _log_sigmoid(x):
    return jnp.minimum(x, 0.0) - jnp.log1p(jnp.exp(-jnp.abs(x)))


def _head_rms_scale(o, lo_mask):
    o2 = o * o
    ss0 = jnp.sum(jnp.where(lo_mask, o2, 0.0), axis=-1, keepdims=True)
    ss1 = jnp.sum(jnp.where(lo_mask, 0.0, o2), axis=-1, keepdims=True)
    return jnp.where(lo_mask, lax.rsqrt(ss0 / DV + EPS), lax.rsqrt(ss1 / DV + EPS))


def _in_proj_kernel(x_ref, g_ref, w_ref, wg_ref, o_ref, og_ref, h_ref):
    @pl.when(pl.program_id(1) == 0)
    def _():
        x = x_ref[...]
        ms = jnp.mean(x * x, axis=-1, keepdims=True)
        h_ref[...] = (x * lax.rsqrt(ms + EPS) * g_ref[...]).astype(BF16)
        og_ref[...] = _dot(h_ref[...], wg_ref[...])

    o_ref[...] = _dot(h_ref[...], w_ref[...]).astype(o_ref.dtype)


def _in_proj(x2, g, w, layer, tm, tn):
    t, d = x2.shape
    return pl.pallas_call(
        _in_proj_kernel,
        grid=(t // tm, MAIN_W // tn),
        in_specs=[
            pl.BlockSpec((tm, d), lambda i, j: (i, 0)),
            pl.BlockSpec((1, d), lambda i, j: (0, 0)),
            pl.BlockSpec((None, d, tn), lambda i, j: (layer, 0, j)),
            pl.BlockSpec((None, d, GATE_W), lambda i, j: (layer, 0, MAIN_W // GATE_W)),
        ],
        out_specs=[pl.BlockSpec((tm, tn), lambda i, j: (i, j)),
                   pl.BlockSpec((tm, GATE_W), lambda i, j: (i, 0))],
        out_shape=[jax.ShapeDtypeStruct((t, MAIN_W), BF16),
                   jax.ShapeDtypeStruct((t, GATE_W), F32)],
        scratch_shapes=[pltpu.VMEM((tm, d), BF16)],
        compiler_params=pltpu.CompilerParams(
            dimension_semantics=("parallel", "arbitrary"), vmem_limit_bytes=VMEM_LIMIT),
        name="in_proj",
    )(x2, g, w, w)


def _gla_kernel(q_ref, k_ref, v_ref, g_ref, a1_ref, a2_ref, ab_ref, nrm_ref, ltri_ref, lvl_ref,
                o_ref, s_ref, cum_ref, la_ref):
    @pl.when(pl.program_id(1) == 0)
    def _():
        s_ref[...] = jnp.zeros_like(s_ref)
        la_ref[0:8, :] = jnp.zeros((8, QK_W), F32)
        la_ref[8 + CHUNK:16 + CHUNK, :] = jnp.zeros((8, QK_W), F32)

    q = q_ref[...].astype(F32) * (DK ** -0.5)
    k = k_ref[...].astype(F32)
    z = _dot(a1_ref[...].astype(BF16), a2_ref[...]) + ab_ref[...]
    log_a = _log_sigmoid(z) * (1.0 / GATE_TAU)
    la_ref[8:8 + CHUNK, :] = log_a
    ltri = ltri_ref[...]
    hi, mid, lo = _split3(log_a)
    cum = _dot(ltri, hi) + _dot(ltri, mid) + _dot(ltri, lo)
    cum_ref[...] = cum

    lvl = lvl_ref[...]
    groups = CHUNK // 8
    rows8 = lambda g: slice(8 * g, 8 * g + 8)
    heads = [slice(h * DKP, (h + 1) * DKP) for h in range(HEADS)]
    row = lax.broadcasted_iota(jnp.int32, (CHUNK, QK_W), 0)

    def scores(xq, yk):
        return [_dot_nt(xq[:, hs], yk[:, hs]) for hs in heads]

    qb = q.astype(BF16)
    kb = k.astype(BF16)
    attn = [[jnp.where(lvl[rows8(g), :] == -1, p[rows8(g), :], 0.0) for g in range(groups)]
            for p in scores(qb, kb)]

    def merge(level, parts, q_groups):
        for h in range(HEADS):
            for i, g in enumerate(q_groups):
                attn[h][g] = jnp.where(lvl[rows8(g), :] == level, parts[h][rows8(i), :], attn[h][g])

    e = jnp.exp(jnp.where((row & 1) == 1, log_a, 0.0))
    merge(0, scores((q * e).astype(BF16), (k * e).astype(BF16)), range(groups))
    nxt = la_ref[pl.ds(9, CHUNK), :]
    prv = la_ref[pl.ds(7, CHUNK), :]
    r4 = row & 3
    e = jnp.exp(jnp.where(r4 == 0, nxt, jnp.where(r4 == 1, 0.0, jnp.where(r4 == 2, log_a, log_a + prv))))
    merge(1, scores((q * e).astype(BF16), (k * e).astype(BF16)), range(groups))
    sub8 = lax.broadcasted_iota(jnp.int32, (8, QK_W), 0)
    pieces = []
    for g in range(groups):
        d = cum[rows8(g), :] - cum_ref[8 * g + 3:8 * g + 4, :]
        pieces.append(jnp.where(sub8 < 4, -d, d))
    e = jnp.exp(jnp.concatenate(pieces, axis=0))
    merge(2, scores((q * e).astype(BF16), (k * e).astype(BF16)), range(groups))
    for level in range(3, N_LEVELS):
        m = 1 << level
        xq, yk, q_groups = [], [], []
        for base in range(0, CHUNK, 2 * m):
            k_rows = slice(base, base + m)
            q_rows = slice(base + m, base + 2 * m)
            edge = cum_ref[base + m - 1:base + m, :]
            yk += [k[k_rows, :] * jnp.exp(edge - cum[k_rows, :]), k[q_rows, :]]
            xq.append(q[q_rows, :] * jnp.exp(cum[q_rows, :] - edge))
            q_groups += range((base + m) // 8, (base + 2 * m) // 8)
        merge(level, scores(jnp.concatenate(xq, axis=0).astype(BF16),
                            jnp.concatenate(yk, axis=0).astype(BF16)), q_groups)

    last = cum_ref[CHUNK - 1:CHUNK, :]
    q_dec = (q * jnp.exp(cum)).astype(BF16)
    k_dec = k * jnp.exp(last - cum)
    decay_all = jnp.exp(last)

    lane = lax.broadcasted_iota(jnp.int32, (CHUNK, PAIR_V), 1)
    lo_mask = lane < DV
    row_s = lax.broadcasted_iota(jnp.int32, (PAIR_K, PAIR_V), 0)
    lane_s = lax.broadcasted_iota(jnp.int32, (PAIR_K, PAIR_V), 1)
    block_mask = jnp.where(row_s < DKP, 0, 1) == jnp.where(lane_s < DV, 0, 1)

    for p in range(HEADS // 2):
        ks = slice(p * PAIR_K, (p + 1) * PAIR_K)
        vs = slice(p * PAIR_V, (p + 1) * PAIR_V)
        vb = v_ref[:, vs]
        zero = jnp.zeros_like(vb)
        v_blk = jnp.concatenate([jnp.where(lo_mask, vb, zero), jnp.where(lo_mask, zero, vb)], axis=0)
        state = s_ref[p]
        lhs = jnp.concatenate([jnp.concatenate(attn[2 * p], axis=0).astype(BF16),
                               jnp.concatenate(attn[2 * p + 1], axis=0).astype(BF16),
                               q_dec[:, ks]], axis=1)
        rhs = jnp.concatenate([v_blk, state.astype(BF16)], axis=0)
        o = _dot(lhs, rhs)

        upd = _dot(k_dec[:, ks].T.astype(BF16), vb)
        dcols = []
        for h in (2 * p, 2 * p + 1):
            d_row = jnp.broadcast_to(decay_all[:, h * DKP:(h + 1) * DKP], (DKP, DKP))
            d_col = d_row.T
            dcols.append(jnp.concatenate([d_col] * (PAIR_V // LANE), axis=1))
        s_ref[p] = state * jnp.concatenate(dcols, axis=0) + jnp.where(block_mask, upd, 0.0)

        gate = g_ref[:, vs].astype(F32)
        out = o * _head_rms_scale(o, lo_mask) * nrm_ref[:, vs] * (gate * jax.nn.sigmoid(gate))
        o_ref[:, vs] = out.astype(o_ref.dtype)


def _gla(proj, gates, a2p, abp, nrm, ltri, lvl, batch, seq):
    nc = seq // CHUNK
    tok = lambda w, cb: pl.BlockSpec((CHUNK, w), lambda b, c: (b * nc + c, cb))
    const = lambda shape: pl.BlockSpec(shape, lambda b, c: (0,) * len(shape))
    return pl.pallas_call(
        _gla_kernel,
        grid=(batch, nc),
        in_specs=[
            tok(QK_W, OFF_GLA_Q // QK_W),
            tok(QK_W, OFF_GLA_K // QK_W),
            tok(V_W, OFF_GLA_V // V_W),
            tok(V_W, OFF_GLA_G // V_W),
            tok(LANE, (OFF_GLA_A1 - MAIN_W) // LANE),
            const((LANE, QK_W)),
            const((1, QK_W)),
            const((1, V_W)),
            const((CHUNK, CHUNK)),
            const((CHUNK, CHUNK)),
        ],
        out_specs=pl.BlockSpec((CHUNK, V_W), lambda b, c: (b * nc + c, 0)),
        out_shape=jax.ShapeDtypeStruct((batch * seq, V_W), BF16),
        scratch_shapes=[
            pltpu.VMEM((HEADS // 2, PAIR_K, PAIR_V), F32),
            pltpu.VMEM((CHUNK, QK_W), F32),
            pltpu.VMEM((CHUNK + 16, QK_W), F32),
        ],
        compiler_params=pltpu.CompilerParams(
            dimension_semantics=("parallel", "arbitrary"), vmem_limit_bytes=VMEM_LIMIT),
        name="gla",
    )(proj, proj, proj, proj, gates, a2p, abp, nrm, ltri, lvl)


ST_W = PAIR_V + LANE
TAIL = 8


def _mlstm_kernel(q_ref, k_ref, v_ref, og_ref, if_ref, cw_ref, ifb_ref, nrm_ref, ltri_ref, lvl_ref,
                  o_ref, c_ref, m_ref, xe_ref):
    @pl.when(pl.program_id(1) == 0)
    def _():
        c_ref[...] = jnp.zeros_like(c_ref)
        m_ref[...] = jnp.zeros_like(m_ref)
        xe_ref[0:TAIL, :] = jnp.zeros((TAIL, 2 * QK_W), F32)

    xe_ref[TAIL:TAIL + CHUNK, 0:QK_W] = q_ref[...].astype(F32)
    xe_ref[TAIL:TAIL + CHUNK, QK_W:2 * QK_W] = k_ref[...].astype(F32)
    y = jnp.zeros((CHUNK, 2 * QK_W), F32)
    for j in range(CONV_WIDTH):
        y = y + cw_ref[j:j + 1, :] * xe_ref[pl.ds(TAIL - (CONV_WIDTH - 1) + j, CHUNK), :]
    xe_ref[0:TAIL, :] = xe_ref[CHUNK:CHUNK + TAIL, :]
    y = y * jax.nn.sigmoid(y)
    qm = y[:, 0:QK_W]
    km = y[:, QK_W:2 * QK_W] * (DK ** -0.5)
    qb = qm.astype(BF16)
    kb = km.astype(BF16)
    km_t = km.T

    slab = if_ref[...] + ifb_ref[...]
    lane_g = lax.broadcasted_iota(jnp.int32, (CHUNK, LANE), 1)
    gates = jnp.where(lane_g < HEADS, slab, _log_sigmoid(slab))
    ltri = ltri_ref[...]
    g_hi, g_mid, g_lo = _split3(gates)
    cum_col = _dot(ltri, g_hi) + _dot(ltri, g_mid) + _dot(ltri, g_lo)
    gates_t = gates.T[0:2 * HEADS, :]
    t_hi, t_mid, t_lo = _split3(gates_t)
    cum_row = _dot_nt(t_hi, ltri) + _dot_nt(t_mid, ltri) + _dot_nt(t_lo, ltri)

    lvl = lvl_ref[...]
    causal = lvl >= -1
    lane = lax.broadcasted_iota(jnp.int32, (CHUNK, PAIR_V), 1)
    lo_mask = lane < DV
    row_s = lax.broadcasted_iota(jnp.int32, (PAIR_K, ST_W), 0)
    lane_s = lax.broadcasted_iota(jnp.int32, (PAIR_K, ST_W), 1)
    lane_head = jnp.where(lane_s < DV, 0, jnp.where(lane_s < PAIR_V, 1, lane_s - PAIR_V))
    state_mask = jnp.where(row_s < DKP, 0, 1) == lane_head
    lane_e = lax.broadcasted_iota(jnp.int32, (CHUNK, LANE), 1)
    ones_cols = jnp.where(lane_e < 2, 1.0, 0.0).astype(BF16)
    lane_r = lax.broadcasted_iota(jnp.int32, (PAIR_K, LANE), 1)
    row_r = lax.broadcasted_iota(jnp.int32, (PAIR_K, LANE), 0)
    ones_blk = jnp.where(lane_r == jnp.where(row_r < CHUNK, 0, 1), 1.0, 0.0).astype(BF16)

    for p in range(HEADS // 2):
        ks = slice(p * PAIR_K, (p + 1) * PAIR_K)
        vs = slice(p * PAIR_V, (p + 1) * PAIR_V)
        vb = v_ref[:, vs]
        zero = jnp.zeros_like(vb)
        sc_parts, qs_parts, floor_parts, kw_parts, dprev_parts = [], [], [], [], []
        for h in (2 * p, 2 * p + 1):
            hs = slice(h * DKP, (h + 1) * DKP)
            cc = jnp.broadcast_to(cum_col[:, HEADS + h:HEADS + h + 1], (CHUNK, CHUNK))
            cum_r = cum_row[HEADS + h:HEADS + h + 1, :]
            ib_r = gates_t[h:h + 1, :]
            m_prev = m_ref[h:h + 1, :]
            dmat = jnp.where(causal, cc - cum_r + ib_r, -jnp.inf)
            inter = cc + m_prev
            m_t = jnp.maximum(inter, jnp.max(dmat, axis=-1, keepdims=True))
            w = jnp.exp(dmat - m_t)
            sc_inter = jnp.exp(inter - m_t)
            sc_parts.append((_dot_nt(qb[:, hs], kb[:, hs]) * w).astype(BF16))
            qs_parts.append((qm[:, hs] * sc_inter).astype(BF16))
            floor_parts.append(jnp.exp(-m_t))
            total = cum_r[:, CHUNK - 1:CHUNK]
            g_row = total - cum_r + ib_r
            m_new = jnp.maximum(total + m_prev, jnp.max(g_row, axis=-1, keepdims=True))
            wj = jnp.exp(g_row - m_new)
            dprev_parts.append(jnp.exp(total + m_prev - m_new))
            kw_parts.append((km_t[hs, :] * wj).astype(BF16))
            m_ref[h:h + 1, :] = m_new

        state = c_ref[p]
        st_m = jnp.where(state_mask, state, 0.0).astype(BF16)
        v_blk = jnp.concatenate([jnp.where(lo_mask, vb, zero), jnp.where(lo_mask, zero, vb)], axis=0)
        rhs = jnp.concatenate([jnp.concatenate([v_blk, ones_blk], axis=1), st_m], axis=0)
        lhs = jnp.concatenate(sc_parts + qs_parts, axis=1)
        res = _dot(lhs, rhs)
        num = res[:, 0:PAIR_V]
        den = jnp.where(lo_mask,
                        jnp.broadcast_to(res[:, PAIR_V:PAIR_V + 1], (CHUNK, PAIR_V)),
                        jnp.broadcast_to(res[:, PAIR_V + 1:PAIR_V + 2], (CHUNK, PAIR_V)))
        floor = jnp.where(lo_mask,
                          jnp.concatenate([floor_parts[0]] * (PAIR_V // LANE), axis=1),
                          jnp.concatenate([floor_parts[1]] * (PAIR_V // LANE), axis=1))
        hid = num / jnp.maximum(jnp.abs(den), floor)

        kw = jnp.concatenate(kw_parts, axis=0)
        v_ext = jnp.concatenate([vb, ones_cols], axis=1)
        upd = _dot(kw, v_ext)
        d_rows = jnp.concatenate(
            [jnp.broadcast_to(jnp.concatenate([d] * (ST_W // LANE), axis=1), (DKP, ST_W))
             for d in dprev_parts], axis=0)
        c_ref[p] = d_rows * state + jnp.where(state_mask, upd, 0.0)

        og = og_ref[:, vs].astype(F32)
        out = hid * _head_rms_scale(hid, lo_mask) * nrm_ref[:, vs] * jax.nn.sigmoid(og)
        o_ref[:, vs] = out.astype(o_ref.dtype)


def _mlstm(proj, gates, cw, ifb, nrm, ltri, lvl, batch, seq):
    nc = seq // CHUNK
    tok = lambda w, cb: pl.BlockSpec((CHUNK, w), lambda b, c: (b * nc + c, cb))
    const = lambda shape: pl.BlockSpec(shape, lambda b, c: (0,) * len(shape))
    return pl.pallas_call(
        _mlstm_kernel,
        grid=(batch, nc),
        in_specs=[
            tok(QK_W, OFF_ML_Q // QK_W),
            tok(QK_W, OFF_ML_K // QK_W),
            tok(V_W, OFF_ML_V // V_W),
            tok(V_W, OFF_ML_O // V_W),
            tok(LANE, (OFF_ML_IF - MAIN_W) // LANE),
            const((CONV_WIDTH, 2 * QK_W)),
            const((1, LANE)),
            const((1, V_W)),
            const((CHUNK, CHUNK)),
            const((CHUNK, CHUNK)),
        ],
        out_specs=pl.BlockSpec((CHUNK, V_W), lambda b, c: (b * nc + c, 0)),
        out_shape=jax.ShapeDtypeStruct((batch * seq, V_W), BF16),
        scratch_shapes=[
            pltpu.VMEM((HEADS // 2, PAIR_K, ST_W), F32),
            pltpu.VMEM((2 * HEADS, LANE), F32),
            pltpu.VMEM((TAIL + CHUNK, 2 * QK_W), F32),
        ],
        compiler_params=pltpu.CompilerParams(
            dimension_semantics=("parallel", "arbitrary"), vmem_limit_bytes=VMEM_LIMIT),
        name="mlstm",
    )(proj, proj, proj, proj, gates, cw, ifb, nrm, ltri, lvl)


def _sgu_kernel(u_ref, v_ref, lng_ref, lnb_ref, w_ref, b_ref, lvl_ref, o_ref):
    causal = lvl_ref[...] >= -1
    w_causal = [jnp.where(causal, w_ref[g], 0.0).astype(BF16) for g in range(SGU_GROUPS)]
    for n in range(u_ref.shape[0] // CHUNK):
        rows = slice(n * CHUNK, (n + 1) * CHUNK)
        u = jax.nn.gelu(u_ref[rows, :].astype(F32))
        v = jax.nn.gelu(v_ref[rows, :].astype(F32))
        mu = jnp.mean(v, axis=-1, keepdims=True)
        var = jnp.mean(jnp.square(v - mu), axis=-1, keepdims=True)
        vn = ((v - mu) * lax.rsqrt(var + EPS) * lng_ref[...] + lnb_ref[...]).astype(BF16)
        for g in range(SGU_GROUPS):
            gs = slice(g * SGU_CH, (g + 1) * SGU_CH)
            mixed = _dot(w_causal[g], vn[:, gs]) + b_ref[:, gs]
            o_ref[rows, gs] = (u[:, gs] * mixed).astype(o_ref.dtype)


def _sgu(proj, lng, lnb, w, b_full, lvl, batch, seq):
    step = CHUNK * SGU_STEP
    while (batch * seq) % step:
        step //= 2
    nb = batch * seq // step
    tok = lambda w_, cb: pl.BlockSpec((step, w_), lambda i: (i, cb))
    const = lambda shape: pl.BlockSpec(shape, lambda i: (0,) * len(shape))
    return pl.pallas_call(
        _sgu_kernel,
        grid=(nb,),
        in_specs=[
            tok(SGU_W, OFF_SGU_U // SGU_W),
            tok(SGU_W, OFF_SGU_V // SGU_W),
            const((1, SGU_W)),
            const((1, SGU_W)),
            const((SGU_GROUPS, CHUNK, CHUNK)),
            const((CHUNK, SGU_W)),
            const((CHUNK, CHUNK)),
        ],
        out_specs=pl.BlockSpec((step, SGU_W), lambda i: (i, 0)),
        out_shape=jax.ShapeDtypeStruct((batch * seq, SGU_W), BF16),
        compiler_params=pltpu.CompilerParams(
            dimension_semantics=("parallel",), vmem_limit_bytes=VMEM_LIMIT),
        name="sgu",
    )(proj, proj, lng, lnb, w, b_full, lvl)


def _out_proj_kernel(x_ref, a_ref, b_ref, c_ref, wa_ref, wb_ref, wc_ref, o_ref):
    o_ref[...] = (x_ref[...] + _dot(a_ref[...], wa_ref[...]) + _dot(b_ref[...], wb_ref[...])
                  + _dot(c_ref[...], wc_ref[...]))


def _out_proj(x2, mix_a, mix_b, mix_c, w_out, layer, tm, tn):
    t, d = x2.shape
    return pl.pallas_call(
        _out_proj_kernel,
        grid=(d // tn, t // tm),
        in_specs=[
            pl.BlockSpec((tm, tn), lambda j, i: (i, j)),
            pl.BlockSpec((tm, V_W), lambda j, i: (i, 0)),
            pl.BlockSpec((tm, V_W), lambda j, i: (i, 0)),
            pl.BlockSpec((tm, SGU_W), lambda j, i: (i, 0)),
            pl.BlockSpec((None, V_W, tn), lambda j, i: (layer, 0, j)),
            pl.BlockSpec((None, V_W, tn), lambda j, i: (layer, 1, j)),
            pl.BlockSpec((None, SGU_W, tn), lambda j, i: (layer, 2 * V_W // SGU_W, j)),
        ],
        out_specs=pl.BlockSpec((tm, tn), lambda j, i: (i, j)),
        out_shape=jax.ShapeDtypeStruct((t, d), F32),
        compiler_params=pltpu.CompilerParams(
            dimension_semantics=("parallel", "arbitrary"), vmem_limit_bytes=VMEM_LIMIT),
        name="out_proj",
    )(x2, mix_a, mix_b, mix_c, w_out, w_out, w_out)


def _ffn_kernel(x_ref, g_ref, wg_ref, wu_ref, wd_ref, gf_ref, o_ref, h_ref, *, final_norm):
    j = pl.program_id(1)

    @pl.when(j == 0)
    def _():
        x = x_ref[...]
        ms = jnp.mean(x * x, axis=-1, keepdims=True)
        h_ref[...] = (x * lax.rsqrt(ms + EPS) * g_ref[...]).astype(BF16)
        o_ref[...] = x

    h = h_ref[...]
    gate = _dot(h, wg_ref[...])
    up = _dot(h, wu_ref[...])
    act = (gate * jax.nn.sigmoid(gate) * up).astype(BF16)
    o_ref[...] += _dot(act, wd_ref[...])

    if final_norm:
        @pl.when(j == pl.num_programs(1) - 1)
        def _():
            y = o_ref[...]
            ms = jnp.mean(y * y, axis=-1, keepdims=True)
            o_ref[...] = y * lax.rsqrt(ms + EPS) * gf_ref[...]


def _ffn(x2, g, w_gu, w_down, g_final, layer, tm, tf, final_norm):
    t, d = x2.shape
    nf = D_FF // tf
    return pl.pallas_call(
        functools.partial(_ffn_kernel, final_norm=final_norm),
        grid=(t // tm, nf),
        in_specs=[
            pl.BlockSpec((tm, d), lambda i, j: (i, 0)),
            pl.BlockSpec((1, d), lambda i, j: (0, 0)),
            pl.BlockSpec((None, d, tf), lambda i, j: (layer, 0, j)),
            pl.BlockSpec((None, d, tf), lambda i, j: (layer, 0, nf + j)),
            pl.BlockSpec((None, tf, d), lambda i, j: (layer, j, 0)),
            pl.BlockSpec((1, d), lambda i, j: (0, 0)),
        ],
        out_specs=pl.BlockSpec((tm, d), lambda i, j: (i, 0)),
        out_shape=jax.ShapeDtypeStruct((t, d), F32),
        scratch_shapes=[pltpu.VMEM((tm, d), BF16)],
        compiler_params=pltpu.CompilerParams(
            dimension_semantics=("parallel", "arbitrary"), vmem_limit_bytes=VMEM_LIMIT),
        name="ffn",
    )(x2, g, w_gu, w_gu, w_down, g_final)


def _tiles(tokens):
    tm = 1024
    while tokens % tm:
        tm //= 2
    return tm


def kernel(x, norm_mix, w_in, gla_a2, gla_ab, gla_norm, ml_conv, ml_ib, ml_fb, ml_norm,
           sgu_ln_g, sgu_ln_b, sgu_w, sgu_b, w_out, norm_ffn, w_gu, w_down, norm_final):
    batch, seq, d = x.shape
    depth = w_in.shape[0]
    tokens = batch * seq
    tm = _tiles(tokens)

    src = _proj_column_map()
    col_idx = jnp.asarray(np.maximum(src, 0), jnp.int32)
    col_valid = jnp.asarray(src >= 0)
    lvl = jnp.asarray(_pair_level_matrix())
    ltri = jnp.asarray(np.tril(np.ones((CHUNK, CHUNK), np.float32)), BF16)

    w_in_b = jnp.where(col_valid[None, None, :], jnp.take(w_in, col_idx, axis=2), 0.0).astype(BF16)
    w_out_b = w_out.astype(BF16)
    w_gu_b = w_gu.astype(BF16)
    w_down_b = w_down.astype(BF16)

    xc = x.reshape(tokens, d)
    for l in range(depth):
        proj, gates = _in_proj(xc, norm_mix[l][None, :], w_in_b, l, tm, 1536)

        a2p = jnp.pad(_pad_heads(gla_a2[l]), ((0, LANE - GATE_RANK), (0, 0))).astype(BF16)
        abp = _pad_heads(gla_ab[l])[None, :]
        mix_a = _gla(proj, gates, a2p, abp, gla_norm[l][None, :], ltri, lvl, batch, seq)

        cw = jnp.concatenate([_pad_heads(ml_conv[l][:, :HEADS * DK]),
                              _pad_heads(ml_conv[l][:, HEADS * DK:])], axis=1)
        ifb = jnp.pad(jnp.concatenate([ml_ib[l], ml_fb[l]]), (0, LANE - 2 * HEADS))[None, :]
        mix_b = _mlstm(proj, gates, cw, ifb, ml_norm[l][None, :], ltri, lvl, batch, seq)

        b_full = jnp.repeat(sgu_b[l].T, SGU_CH, axis=1)
        mix_c = _sgu(proj, sgu_ln_g[l][None, :], sgu_ln_b[l][None, :], sgu_w[l], b_full, lvl, batch, seq)

        x1 = _out_proj(xc, mix_a, mix_b, mix_c, w_out_b, l, tm, 1024)
        xc = _ffn(x1, norm_ffn[l][None, :], w_gu_b, w_down_b, norm_final[None, :], l, tm, 512,
                  final_norm=(l == depth - 1))
    return xc.reshape(batch, seq, d)
```

```python
import functools

import numpy as np
import jax
import jax.numpy as jnp
from jax import lax
from jax.experimental import pallas as pl
from jax.experimental.pallas import tpu as pltpu

F32 = jnp.float32
BF16 = jnp.bfloat16

D_MODEL = 2048
HEADS = 4
DK = 96
DKP = 128
DV = 192
QK_W = HEADS * DKP
V_W = HEADS * DV
PAIR_K = 2 * DKP
PAIR_V = 2 * DV
SGU_W = 512
SGU_GROUPS = 4
SGU_CH = 128
GATE_RANK = 16
GATE_TAU = 16.0
CONV_WIDTH = 4
D_FF = 5632
EPS = 1e-6
CHUNK = 128
N_LEVELS = 7
LANE = 128

OFF_GLA_Q, OFF_GLA_K, OFF_ML_Q, OFF_ML_K = 0, 512, 1024, 1536
OFF_SGU_U, OFF_SGU_V = 2048, 2560
OFF_GLA_V, OFF_GLA_G, OFF_ML_V, OFF_ML_O = 3072, 3840, 4608, 5376
MAIN_W = 6144
OFF_GLA_A1, OFF_ML_IF = 6144, 6272
GATE_W = 2 * LANE
PROJ_W = MAIN_W + GATE_W
SGU_STEP = 4

VMEM_LIMIT = 56 * 1024 * 1024


def _relayout_w_in(w_in):
    gk = HEADS * DK
    sizes = (gk, gk, V_W, V_W, GATE_RANK, gk, gk, V_W, V_W, HEADS, HEADS, SGU_W, SGU_W)
    starts = np.concatenate([[0], np.cumsum(sizes)]).tolist()
    w = w_in.astype(BF16)
    (gq, gkk, gv, gg, ga1, mq, mk, mv, mo, mi, mf, su, sv) = [
        w[..., starts[n]:starts[n + 1]] for n in range(len(sizes))]
    pad_to = lambda a, n: jnp.pad(a, [(0, 0)] * (a.ndim - 1) + [(0, n - a.shape[-1])])
    cols = [_pad_heads(gq), _pad_heads(gkk), _pad_heads(mq), _pad_heads(mk), su, sv, gv, gg, mv, mo,
            pad_to(ga1, LANE), pad_to(jnp.concatenate([mi, mf], axis=-1), LANE)]
    out = jnp.concatenate(cols, axis=-1)
    assert out.shape[-1] == PROJ_W
    return out


def _pad_heads(a):
    lead = a.shape[:-1]
    a = a.reshape(lead + (HEADS, DK))
    a = jnp.pad(a, [(0, 0)] * len(lead) + [(0, 0), (0, DKP - DK)])
    return a.reshape(lead + (QK_W,))


def _pair_level_matrix():
    c = CHUNK
    t = np.arange(c)[:, None]
    s = np.arange(c)[None, :]
    x = np.bitwise_xor(t, s)
    lvl = np.floor(np.log2(np.maximum(x, 1))).astype(np.int32)
    lvl = np.where(s == t, -1, lvl)
    lvl = np.where(s > t, -2, lvl)
    return lvl.astype(np.int32)


def _dot(a, b):
    return jnp.dot(a, b, preferred_element_type=F32)


def _dot_nt(a, b):
    return lax.dot_general(a, b, (((1,), (1,)), ((), ())), preferred_element_type=F32)


def _split3(x):
    hi = x.astype(BF16)
    r = x - hi.astype(F32)
    mid = r.astype(BF16)
    lo = (r - mid.astype(F32)).astype(BF16)
    return hi, mid, lo


LOG2E = 1.4426950408889634
LN2 = 0.6931471805599453


def _log2_sigmoid(x):
    xl = x * LOG2E
    return jnp.minimum(xl, 0.0) - jnp.log2(1.0 + jnp.exp2(-jnp.abs(xl)))


def _log_sigmoid(x):
    return _log2_sigmoid(x) * LN2


def _pair_cols(c0, c1, rows):
    first = lax.broadcasted_iota(jnp.int32, (rows, LANE), 1) < DV - LANE
    b0 = jnp.broadcast_to(c0, (rows, LANE))
    b1 = jnp.broadcast_to(c1, (rows, LANE))
    return jnp.concatenate([b0, jnp.where(first, b0, b1), b1], axis=1)


def _head_rms_scale(o):
    rows = o.shape[0]
    o2 = o * o
    first = lax.broadcasted_iota(jnp.int32, (rows, LANE), 1) < DV - LANE
    mid = o2[:, LANE:2 * LANE]
    ss0 = jnp.sum(o2[:, 0:LANE] + jnp.where(first, mid, 0.0), axis=-1, keepdims=True)
    ss1 = jnp.sum(o2[:, 2 * LANE:3 * LANE] + jnp.where(first, 0.0, mid), axis=-1, keepdims=True)
    return _pair_cols(lax.rsqrt(ss0 / DV + EPS), lax.rsqrt(ss1 / DV + EPS), rows)


def _in_proj_kernel(x_ref, g_ref, w_ref, wg_ref, o_ref, og_ref, h_ref):
    @pl.when(pl.program_id(1) == 0)
    def _():
        x = x_ref[...]
        ms = jnp.mean(x * x, axis=-1, keepdims=True)
        h_ref[...] = (x * lax.rsqrt(ms + EPS) * g_ref[...]).astype(BF16)
        og_ref[...] = _dot(h_ref[...], wg_ref[...])

    o_ref[...] = _dot(h_ref[...], w_ref[...]).astype(o_ref.dtype)


def _in_proj(x2, g, w, layer, tm, tn):
    t, d = x2.shape
    return pl.pallas_call(
        _in_proj_kernel,
        grid=(t // tm, MAIN_W // tn),
        in_specs=[
            pl.BlockSpec((tm, d), lambda i, j: (i, 0)),
            pl.BlockSpec((1, d), lambda i, j: (0, 0)),
            pl.BlockSpec((None, d, tn), lambda i, j: (layer, 0, j)),
            pl.BlockSpec((None, d, GATE_W), lambda i, j: (layer, 0, MAIN_W // GATE_W)),
        ],
        out_specs=[pl.BlockSpec((tm, tn), lambda i, j: (i, j)),
                   pl.BlockSpec((tm, GATE_W), lambda i, j: (i, 0))],
        out_shape=[jax.ShapeDtypeStruct((t, MAIN_W), BF16),
                   jax.ShapeDtypeStruct((t, GATE_W), F32)],
        scratch_shapes=[pltpu.VMEM((tm, d), BF16)],
        compiler_params=pltpu.CompilerParams(
            dimension_semantics=("parallel", "arbitrary"), vmem_limit_bytes=VMEM_LIMIT),
        name="in_proj",
    )(x2, g, w, w)


def _gla_kernel(q_ref, k_ref, v_ref, g_ref, a1_ref, a2_ref, ab_ref, nrm_ref, ltri_ref, lvl_ref,
                o_ref, s_ref, cum_ref, la_ref):
    @pl.when(pl.program_id(1) == 0)
    def _():
        s_ref[...] = jnp.zeros_like(s_ref)
        la_ref[0:8, :] = jnp.zeros((8, QK_W), F32)
        la_ref[8 + CHUNK:16 + CHUNK, :] = jnp.zeros((8, QK_W), F32)

    q = q_ref[...].astype(F32) * (DK ** -0.5)
    k = k_ref[...].astype(F32)
    z = _dot(a1_ref[...].astype(BF16), a2_ref[...]) + ab_ref[...]
    log_a = _log2_sigmoid(z) * (1.0 / GATE_TAU)
    la_ref[8:8 + CHUNK, :] = log_a
    ltri = ltri_ref[...]
    hi, mid, lo = _split3(log_a)
    cum = _dot(ltri, hi) + _dot(ltri, mid) + _dot(ltri, lo)
    cum_ref[...] = cum

    lvl = lvl_ref[...]
    groups = CHUNK // 8
    rows8 = lambda g: slice(8 * g, 8 * g + 8)
    heads = [slice(h * DKP, (h + 1) * DKP) for h in range(HEADS)]
    row = lax.broadcasted_iota(jnp.int32, (CHUNK, QK_W), 0)

    def scores(xq, yk):
        return [_dot_nt(xq[:, hs], yk[:, hs]) for hs in heads]

    qb = q.astype(BF16)
    kb = k.astype(BF16)
    attn = [[jnp.where(lvl[rows8(g), :] == -1, p[rows8(g), :], 0.0) for g in range(groups)]
            for p in scores(qb, kb)]

    def merge(level, parts, q_groups):
        for h in range(HEADS):
            for i, g in enumerate(q_groups):
                attn[h][g] = jnp.where(lvl[rows8(g), :] == level, parts[h][rows8(i), :], attn[h][g])

    e = jnp.exp2(jnp.where((row & 1) == 1, log_a, 0.0))
    merge(0, scores((q * e).astype(BF16), kb), range(groups))
    nxt = la_ref[pl.ds(9, CHUNK), :]
    prv = la_ref[pl.ds(7, CHUNK), :]
    r4 = row & 3
    e = jnp.exp2(jnp.where(r4 == 0, nxt, jnp.where(r4 == 1, 0.0, jnp.where(r4 == 2, log_a, log_a + prv))))
    merge(1, scores((q * e).astype(BF16), (k * e).astype(BF16)), range(groups))
    sub8 = lax.broadcasted_iota(jnp.int32, (8, QK_W), 0)
    pieces = []
    for g in range(groups):
        d = cum[rows8(g), :] - cum_ref[8 * g + 3:8 * g + 4, :]
        pieces.append(jnp.where(sub8 < 4, -d, d))
    e = jnp.exp2(jnp.concatenate(pieces, axis=0))
    merge(2, scores((q * e).astype(BF16), (k * e).astype(BF16)), range(groups))
    for level in range(3, N_LEVELS):
        m = 1 << level
        xq, yk, q_groups = [], [], []
        for base in range(0, CHUNK, 2 * m):
            k_rows = slice(base, base + m)
            q_rows = slice(base + m, base + 2 * m)
            edge = cum_ref[base + m - 1:base + m, :]
            yk += [k[k_rows, :] * jnp.exp2(edge - cum[k_rows, :]), k[q_rows, :]]
            xq.append(q[q_rows, :] * jnp.exp2(cum[q_rows, :] - edge))
            q_groups += range((base + m) // 8, (base + 2 * m) // 8)
        merge(level, scores(jnp.concatenate(xq, axis=0).astype(BF16),
                            jnp.concatenate(yk, axis=0).astype(BF16)), q_groups)

    last = cum_ref[CHUNK - 1:CHUNK, :]
    q_dec = (q * jnp.exp2(cum)).astype(BF16)
    k_dec = k * jnp.exp2(last - cum)
    decay_all = jnp.exp2(last)

    lane = lax.broadcasted_iota(jnp.int32, (CHUNK, PAIR_V), 1)
    lo_mask = lane < DV

    for p in range(HEADS // 2):
        ks = slice(p * PAIR_K, (p + 1) * PAIR_K)
        vs = slice(p * PAIR_V, (p + 1) * PAIR_V)
        vb = v_ref[:, vs]
        zero = jnp.zeros_like(vb)
        v_lo = jnp.where(lo_mask, vb, zero)
        v_hi = jnp.where(lo_mask, zero, vb)
        v_blk = jnp.concatenate([v_lo, v_hi], axis=0)
        state = s_ref[p]
        lhs = jnp.concatenate([jnp.concatenate(attn[2 * p], axis=0).astype(BF16),
                               jnp.concatenate(attn[2 * p + 1], axis=0).astype(BF16),
                               q_dec[:, ks]], axis=1)
        rhs = jnp.concatenate([v_blk, state.astype(BF16)], axis=0)
        o = _dot(lhs, rhs)

        kd_t = k_dec[:, ks].T.astype(BF16)
        upd = jnp.concatenate([_dot(kd_t[0:DKP, :], v_lo), _dot(kd_t[DKP:PAIR_K, :], v_hi)], axis=0)
        dcols = []
        for h in (2 * p, 2 * p + 1):
            d_row = jnp.broadcast_to(decay_all[:, h * DKP:(h + 1) * DKP], (DKP, DKP))
            d_col = d_row.T
            dcols.append(jnp.concatenate([d_col] * (PAIR_V // LANE), axis=1))
        s_ref[p] = state * jnp.concatenate(dcols, axis=0) + upd

        gate = g_ref[:, vs].astype(F32)
        out = o * _head_rms_scale(o) * nrm_ref[:, vs] * (gate * jax.nn.sigmoid(gate))
        o_ref[:, vs] = out.astype(o_ref.dtype)


def _gla(proj, gates, a2p, abp, nrm, ltri, lvl, batch, seq):
    nc = seq // CHUNK
    tok = lambda w, cb: pl.BlockSpec((CHUNK, w), lambda b, c: (b * nc + c, cb))
    const = lambda shape: pl.BlockSpec(shape, lambda b, c: (0,) * len(shape))
    return pl.pallas_call(
        _gla_kernel,
        grid=(batch, nc),
        in_specs=[
            tok(QK_W, OFF_GLA_Q // QK_W),
            tok(QK_W, OFF_GLA_K // QK_W),
            tok(V_W, OFF_GLA_V // V_W),
            tok(V_W, OFF_GLA_G // V_W),
            tok(LANE, (OFF_GLA_A1 - MAIN_W) // LANE),
            const((LANE, QK_W)),
            const((1, QK_W)),
            const((1, V_W)),
            const((CHUNK, CHUNK)),
            const((CHUNK, CHUNK)),
        ],
        out_specs=pl.BlockSpec((CHUNK, V_W), lambda b, c: (b * nc + c, 0)),
        out_shape=jax.ShapeDtypeStruct((batch * seq, V_W), BF16),
        scratch_shapes=[
            pltpu.VMEM((HEADS // 2, PAIR_K, PAIR_V), F32),
            pltpu.VMEM((CHUNK, QK_W), F32),
            pltpu.VMEM((CHUNK + 16, QK_W), F32),
        ],
        compiler_params=pltpu.CompilerParams(
            dimension_semantics=("parallel", "arbitrary"), vmem_limit_bytes=VMEM_LIMIT),
        name="gla",
    )(proj, proj, proj, proj, gates, a2p, abp, nrm, ltri, lvl)


ST_W = PAIR_V + LANE
TAIL = 8


def _mlstm_kernel(q_ref, k_ref, v_ref, og_ref, if_ref, cw_ref, ifb_ref, nrm_ref, ltri_ref, lvl_ref,
                  o_ref, c_ref, m_ref, xe_ref):
    @pl.when(pl.program_id(1) == 0)
    def _():
        c_ref[...] = jnp.zeros_like(c_ref)
        m_ref[...] = jnp.zeros_like(m_ref)
        xe_ref[0:TAIL, :] = jnp.zeros((TAIL, 2 * QK_W), F32)

    xe_ref[TAIL:TAIL + CHUNK, 0:QK_W] = q_ref[...].astype(F32)
    xe_ref[TAIL:TAIL + CHUNK, QK_W:2 * QK_W] = k_ref[...].astype(F32)
    y = jnp.zeros((CHUNK, 2 * QK_W), F32)
    for j in range(CONV_WIDTH):
        y = y + cw_ref[j:j + 1, :] * xe_ref[pl.ds(TAIL - (CONV_WIDTH - 1) + j, CHUNK), :]
    xe_ref[0:TAIL, :] = xe_ref[CHUNK:CHUNK + TAIL, :]
    y = y * jax.nn.sigmoid(y)
    qm = y[:, 0:QK_W]
    km = y[:, QK_W:2 * QK_W] * (DK ** -0.5)
    qb = qm.astype(BF16)
    kb = km.astype(BF16)
    km_t = km.T

    slab = if_ref[...] + ifb_ref[...]
    lane_g = lax.broadcasted_iota(jnp.int32, (CHUNK, LANE), 1)
    gates = jnp.where(lane_g < HEADS, slab, _log_sigmoid(slab))
    ltri = ltri_ref[...]
    g_hi, g_mid, g_lo = _split3(gates)
    cum_col = _dot(ltri, g_hi) + _dot(ltri, g_mid) + _dot(ltri, g_lo)
    gates_t = gates.T[0:2 * HEADS, :]
    t_hi, t_mid, t_lo = _split3(gates_t)
    cum_row = _dot_nt(t_hi, ltri) + _dot_nt(t_mid, ltri) + _dot_nt(t_lo, ltri)

    lvl = lvl_ref[...]
    causal = lvl >= -1
    lane = lax.broadcasted_iota(jnp.int32, (CHUNK, PAIR_V), 1)
    lo_mask = lane < DV
    lane_e = lax.broadcasted_iota(jnp.int32, (CHUNK, LANE), 1)
    one_col = [jnp.where(lane_e == i, 1.0, 0.0).astype(BF16) for i in range(2)]
    lane_r = lax.broadcasted_iota(jnp.int32, (PAIR_K, LANE), 1)
    row_r = lax.broadcasted_iota(jnp.int32, (PAIR_K, LANE), 0)
    ones_blk = jnp.where(lane_r == jnp.where(row_r < CHUNK, 0, 1), 1.0, 0.0).astype(BF16)

    for p in range(HEADS // 2):
        ks = slice(p * PAIR_K, (p + 1) * PAIR_K)
        vs = slice(p * PAIR_V, (p + 1) * PAIR_V)
        vb = v_ref[:, vs]
        zero = jnp.zeros_like(vb)
        sc_parts, qs_parts, floor_parts, kw_parts, dprev_parts = [], [], [], [], []
        for h in (2 * p, 2 * p + 1):
            hs = slice(h * DKP, (h + 1) * DKP)
            cc = jnp.broadcast_to(cum_col[:, HEADS + h:HEADS + h + 1], (CHUNK, CHUNK))
            cum_r = cum_row[HEADS + h:HEADS + h + 1, :]
            ib_r = gates_t[h:h + 1, :]
            m_prev = m_ref[h:h + 1, :]
            dmat = jnp.where(causal, cc - cum_r + ib_r, -jnp.inf)
            inter = cc + m_prev
            m_t = jnp.maximum(inter, jnp.max(dmat, axis=-1, keepdims=True))
            w = jnp.exp(dmat - m_t)
            sc_inter = jnp.exp(inter - m_t)
            sc_parts.append((_dot_nt(qb[:, hs], kb[:, hs]) * w).astype(BF16))
            qs_parts.append((qm[:, hs] * sc_inter).astype(BF16))
            floor_parts.append(jnp.exp(-m_t))
            total = cum_r[:, CHUNK - 1:CHUNK]
            g_row = total - cum_r + ib_r
            m_new = jnp.maximum(total + m_prev, jnp.max(g_row, axis=-1, keepdims=True))
            wj = jnp.exp(g_row - m_new)
            dprev_parts.append(jnp.exp(total + m_prev - m_new))
            kw_parts.append((km_t[hs, :] * wj).astype(BF16))
            m_ref[h:h + 1, :] = m_new

        state = c_ref[p]
        st_m = state.astype(BF16)
        v_lo = jnp.where(lo_mask, vb, zero)
        v_hi = jnp.where(lo_mask, zero, vb)
        v_blk = jnp.concatenate([v_lo, v_hi], axis=0)
        rhs = jnp.concatenate([jnp.concatenate([v_blk, ones_blk], axis=1), st_m], axis=0)
        lhs = jnp.concatenate(sc_parts + qs_parts, axis=1)
        res = _dot(lhs, rhs)
        num = res[:, 0:PAIR_V]
        den = _pair_cols(res[:, PAIR_V:PAIR_V + 1], res[:, PAIR_V + 1:PAIR_V + 2], CHUNK)
        floor = _pair_cols(floor_parts[0][:, 0:1], floor_parts[1][:, 0:1], CHUNK)
        hid = num / jnp.maximum(jnp.abs(den), floor)

        upd = jnp.concatenate(
            [_dot(kw_parts[0], jnp.concatenate([v_lo, one_col[0]], axis=1)),
             _dot(kw_parts[1], jnp.concatenate([v_hi, one_col[1]], axis=1))], axis=0)
        d_rows = jnp.concatenate(
            [jnp.broadcast_to(jnp.concatenate([d] * (ST_W // LANE), axis=1), (DKP, ST_W))
             for d in dprev_parts], axis=0)
        c_ref[p] = d_rows * state + upd

        og = og_ref[:, vs].astype(F32)
        out = hid * _head_rms_scale(hid) * nrm_ref[:, vs] * jax.nn.sigmoid(og)
        o_ref[:, vs] = out.astype(o_ref.dtype)


def _mlstm(proj, gates, cw, ifb, nrm, ltri, lvl, batch, seq):
    nc = seq // CHUNK
    tok = lambda w, cb: pl.BlockSpec((CHUNK, w), lambda b, c: (b * nc + c, cb))
    const = lambda shape: pl.BlockSpec(shape, lambda b, c: (0,) * len(shape))
    return pl.pallas_call(
        _mlstm_kernel,
        grid=(batch, nc),
        in_specs=[
            tok(QK_W, OFF_ML_Q // QK_W),
            tok(QK_W, OFF_ML_K // QK_W),
            tok(V_W, OFF_ML_V // V_W),
            tok(V_W, OFF_ML_O // V_W),
            tok(LANE, (OFF_ML_IF - MAIN_W) // LANE),
            const((CONV_WIDTH, 2 * QK_W)),
            const((1, LANE)),
            const((1, V_W)),
            const((CHUNK, CHUNK)),
            const((CHUNK, CHUNK)),
        ],
        out_specs=pl.BlockSpec((CHUNK, V_W), lambda b, c: (b * nc + c, 0)),
        out_shape=jax.ShapeDtypeStruct((batch * seq, V_W), BF16),
        scratch_shapes=[
            pltpu.VMEM((HEADS // 2, PAIR_K, ST_W), F32),
            pltpu.VMEM((2 * HEADS, LANE), F32),
            pltpu.VMEM((TAIL + CHUNK, 2 * QK_W), F32),
        ],
        compiler_params=pltpu.CompilerParams(
            dimension_semantics=("parallel", "arbitrary"), vmem_limit_bytes=VMEM_LIMIT),
        name="mlstm",
    )(proj, proj, proj, proj, gates, cw, ifb, nrm, ltri, lvl)


def _sgu_kernel(u_ref, v_ref, lng_ref, lnb_ref, w_ref, b_ref, lvl_ref, o_ref):
    causal = lvl_ref[...] >= -1
    w_causal = [jnp.where(causal, w_ref[g], 0.0).astype(BF16) for g in range(SGU_GROUPS)]
    for n in range(u_ref.shape[0] // CHUNK):
        rows = slice(n * CHUNK, (n + 1) * CHUNK)
        u = jax.nn.gelu(u_ref[rows, :].astype(F32))
        v = jax.nn.gelu(v_ref[rows, :].astype(F32))
        mu = jnp.mean(v, axis=-1, keepdims=True)
        var = jnp.mean(jnp.square(v - mu), axis=-1, keepdims=True)
        vn = ((v - mu) * lax.rsqrt(var + EPS) * lng_ref[...] + lnb_ref[...]).astype(BF16)
        for g in range(SGU_GROUPS):
            gs = slice(g * SGU_CH, (g + 1) * SGU_CH)
            mixed = _dot(w_causal[g], vn[:, gs]) + b_ref[:, gs]
            o_ref[rows, gs] = (u[:, gs] * mixed).astype(o_ref.dtype)


def _sgu(proj, lng, lnb, w, b_full, lvl, batch, seq):
    step = CHUNK * SGU_STEP
    while (batch * seq) % step:
        step //= 2
    nb = batch * seq // step
    tok = lambda w_, cb: pl.BlockSpec((step, w_), lambda i: (i, cb))
    const = lambda shape: pl.BlockSpec(shape, lambda i: (0,) * len(shape))
    return pl.pallas_call(
        _sgu_kernel,
        grid=(nb,),
        in_specs=[
            tok(SGU_W, OFF_SGU_U // SGU_W),
            tok(SGU_W, OFF_SGU_V // SGU_W),
            const((1, SGU_W)),
            const((1, SGU_W)),
            const((SGU_GROUPS, CHUNK, CHUNK)),
            const((CHUNK, SGU_W)),
            const((CHUNK, CHUNK)),
        ],
        out_specs=pl.BlockSpec((step, SGU_W), lambda i: (i, 0)),
        out_shape=jax.ShapeDtypeStruct((batch * seq, SGU_W), BF16),
        compiler_params=pltpu.CompilerParams(
            dimension_semantics=("parallel",), vmem_limit_bytes=VMEM_LIMIT),
        name="sgu",
    )(proj, proj, lng, lnb, w, b_full, lvl)


def _out_proj_kernel(x_ref, a_ref, b_ref, c_ref, wa_ref, wb_ref, wc_ref, o_ref):
    o_ref[...] = (x_ref[...] + _dot(a_ref[...], wa_ref[...]) + _dot(b_ref[...], wb_ref[...])
                  + _dot(c_ref[...], wc_ref[...]))


def _out_proj(x2, mix_a, mix_b, mix_c, w_out, layer, tm, tn):
    t, d = x2.shape
    return pl.pallas_call(
        _out_proj_kernel,
        grid=(d // tn, t // tm),
        in_specs=[
            pl.BlockSpec((tm, tn), lambda j, i: (i, j)),
            pl.BlockSpec((tm, V_W), lambda j, i: (i, 0)),
            pl.BlockSpec((tm, V_W), lambda j, i: (i, 0)),
            pl.BlockSpec((tm, SGU_W), lambda j, i: (i, 0)),
            pl.BlockSpec((None, V_W, tn), lambda j, i: (layer, 0, j)),
            pl.BlockSpec((None, V_W, tn), lambda j, i: (layer, 1, j)),
            pl.BlockSpec((None, SGU_W, tn), lambda j, i: (layer, 2 * V_W // SGU_W, j)),
        ],
        out_specs=pl.BlockSpec((tm, tn), lambda j, i: (i, j)),
        out_shape=jax.ShapeDtypeStruct((t, d), F32),
        compiler_params=pltpu.CompilerParams(
            dimension_semantics=("parallel", "arbitrary"), vmem_limit_bytes=VMEM_LIMIT),
        name="out_proj",
    )(x2, mix_a, mix_b, mix_c, w_out, w_out, w_out)


def _ffn_kernel(x_ref, g_ref, wg_ref, wu_ref, wd_ref, gf_ref, o_ref, h_ref, *, final_norm):
    j = pl.program_id(1)

    @pl.when(j == 0)
    def _():
        x = x_ref[...]
        ms = jnp.mean(x * x, axis=-1, keepdims=True)
        h_ref[...] = (x * lax.rsqrt(ms + EPS) * g_ref[...]).astype(BF16)
        o_ref[...] = x

    h = h_ref[...]
    gate = _dot(h, wg_ref[...])
    up = _dot(h, wu_ref[...])
    act = (gate * jax.nn.sigmoid(gate) * up).astype(BF16)
    o_ref[...] += _dot(act, wd_ref[...])

    if final_norm:
        @pl.when(j == pl.num_programs(1) - 1)
        def _():
            y = o_ref[...]
            ms = jnp.mean(y * y, axis=-1, keepdims=True)
            o_ref[...] = y * lax.rsqrt(ms + EPS) * gf_ref[...]


def _ffn(x2, g, w_gu, w_down, g_final, layer, tm, tf, final_norm):
    t, d = x2.shape
    nf = D_FF // tf
    return pl.pallas_call(
        functools.partial(_ffn_kernel, final_norm=final_norm),
        grid=(t // tm, nf),
        in_specs=[
            pl.BlockSpec((tm, d), lambda i, j: (i, 0)),
            pl.BlockSpec((1, d), lambda i, j: (0, 0)),
            pl.BlockSpec((None, d, tf), lambda i, j: (layer, 0, j)),
            pl.BlockSpec((None, d, tf), lambda i, j: (layer, 0, nf + j)),
            pl.BlockSpec((None, tf, d), lambda i, j: (layer, j, 0)),
            pl.BlockSpec((1, d), lambda i, j: (0, 0)),
        ],
        out_specs=pl.BlockSpec((tm, d), lambda i, j: (i, 0)),
        out_shape=jax.ShapeDtypeStruct((t, d), F32),
        scratch_shapes=[pltpu.VMEM((tm, d), BF16)],
        compiler_params=pltpu.CompilerParams(
            dimension_semantics=("parallel", "arbitrary"), vmem_limit_bytes=VMEM_LIMIT),
        name="ffn",
    )(x2, g, w_gu, w_gu, w_down, g_final)


def _tiles(tokens):
    tm = 1024
    while tokens % tm:
        tm //= 2
    return tm


def kernel(x, norm_mix, w_in, gla_a2, gla_ab, gla_norm, ml_conv, ml_ib, ml_fb, ml_norm,
           sgu_ln_g, sgu_ln_b, sgu_w, sgu_b, w_out, norm_ffn, w_gu, w_down, norm_final):
    batch, seq, d = x.shape
    depth = w_in.shape[0]
    tokens = batch * seq
    tm = _tiles(tokens)

    lvl = jnp.asarray(_pair_level_matrix())
    ltri = jnp.asarray(np.tril(np.ones((CHUNK, CHUNK), np.float32)), BF16)

    w_in_b = _relayout_w_in(w_in)
    w_out_b = w_out.astype(BF16)
    w_gu_b = w_gu.astype(BF16)
    w_down_b = w_down.astype(BF16)

    xc = x.reshape(tokens, d)
    for l in range(depth):
        proj, gates = _in_proj(xc, norm_mix[l][None, :], w_in_b, l, tm, 1536)

        a2p = jnp.pad(_pad_heads(gla_a2[l]), ((0, LANE - GATE_RANK), (0, 0))).astype(BF16)
        abp = _pad_heads(gla_ab[l])[None, :]
        mix_a = _gla(proj, gates, a2p, abp, gla_norm[l][None, :], ltri, lvl, batch, seq)

        cw = jnp.concatenate([_pad_heads(ml_conv[l][:, :HEADS * DK]),
                              _pad_heads(ml_conv[l][:, HEADS * DK:])], axis=1)
        ifb = jnp.pad(jnp.concatenate([ml_ib[l], ml_fb[l]]), (0, LANE - 2 * HEADS))[None, :]
        mix_b = _mlstm(proj, gates, cw, ifb, ml_norm[l][None, :], ltri, lvl, batch, seq)

        b_full = jnp.repeat(sgu_b[l].T, SGU_CH, axis=1)
        mix_c = _sgu(proj, sgu_ln_g[l][None, :], sgu_ln_b[l][None, :], sgu_w[l], b_full, lvl, batch, seq)

        x1 = _out_proj(xc, mix_a, mix_b, mix_c, w_out_b, l, tm, 1024)
        xc = _ffn(x1, norm_ffn[l][None, :], w_gu_b, w_down_b, norm_final[None, :], l, tm, 512,
                  final_norm=(l == depth - 1))
    return xc.reshape(batch, seq, d)
```

```python
import functools

import numpy as np
import jax
import jax.numpy as jnp
from jax import lax
from jax.experimental import pallas as pl
from jax.experimental.pallas import tpu as pltpu

F32 = jnp.float32
BF16 = jnp.bfloat16

D_MODEL = 2048
HEADS = 4
DK = 96
DKP = 128
DV = 192
QK_W = HEADS * DKP
V_W = HEADS * DV
PAIR_K = 2 * DKP
PAIR_V = 2 * DV
SGU_W = 512
SGU_GROUPS = 4
SGU_CH = 128
GATE_RANK = 16
GATE_TAU = 16.0
CONV_WIDTH = 4
D_FF = 5632
EPS = 1e-6
CHUNK = 128
N_LEVELS = 7
LANE = 128

OFF_GLA_Q, OFF_GLA_K, OFF_ML_Q, OFF_ML_K = 0, 512, 1024, 1536
OFF_SGU_U, OFF_SGU_V = 2048, 2560
OFF_GLA_V, OFF_GLA_G, OFF_ML_V, OFF_ML_O = 3072, 3840, 4608, 5376
MAIN_W = 6144
OFF_GLA_A1, OFF_ML_IF = 6144, 6272
GATE_W = 2 * LANE
PROJ_W = MAIN_W + GATE_W
SGU_STEP = 4

VMEM_LIMIT = 56 * 1024 * 1024


def _relayout_w_in(w_in):
    gk = HEADS * DK
    sizes = (gk, gk, V_W, V_W, GATE_RANK, gk, gk, V_W, V_W, HEADS, HEADS, SGU_W, SGU_W)
    starts = np.concatenate([[0], np.cumsum(sizes)]).tolist()
    w = w_in.astype(BF16)
    (gq, gkk, gv, gg, ga1, mq, mk, mv, mo, mi, mf, su, sv) = [
        w[..., starts[n]:starts[n + 1]] for n in range(len(sizes))]
    pad_to = lambda a, n: jnp.pad(a, [(0, 0)] * (a.ndim - 1) + [(0, n - a.shape[-1])])
    cols = [_pad_heads(gq), _pad_heads(gkk), _pad_heads(mq), _pad_heads(mk), su, sv, gv, gg, mv, mo,
            pad_to(ga1, LANE), pad_to(jnp.concatenate([mi, mf], axis=-1), LANE)]
    out = jnp.concatenate(cols, axis=-1)
    assert out.shape[-1] == PROJ_W
    return out


def _pad_heads(a):
    lead = a.shape[:-1]
    a = a.reshape(lead + (HEADS, DK))
    a = jnp.pad(a, [(0, 0)] * len(lead) + [(0, 0), (0, DKP - DK)])
    return a.reshape(lead + (QK_W,))


def _pair_level_matrix():
    c = CHUNK
    t = np.arange(c)[:, None]
    s = np.arange(c)[None, :]
    x = np.bitwise_xor(t, s)
    lvl = np.floor(np.log2(np.maximum(x, 1))).astype(np.int32)
    lvl = np.where(s == t, -1, lvl)
    lvl = np.where(s > t, -2, lvl)
    return lvl.astype(np.int32)


def _dot(a, b):
    return jnp.dot(a, b, preferred_element_type=F32)


def _dot_nt(a, b):
    return lax.dot_general(a, b, (((1,), (1,)), ((), ())), preferred_element_type=F32)


def _split3(x):
    hi = x.astype(BF16)
    r = x - hi.astype(F32)
    mid = r.astype(BF16)
    lo = (r - mid.astype(F32)).astype(BF16)
    return hi, mid, lo


LOG2E = 1.4426950408889634
LN2 = 0.6931471805599453


def _log2_sigmoid(x):
    xl = x * LOG2E
    return jnp.minimum(xl, 0.0) - jnp.log2(1.0 + jnp.exp2(-jnp.abs(xl)))


def _log_sigmoid(x):
    return _log2_sigmoid(x) * LN2


def _pair_cols(c0, c1, rows):
    first = lax.broadcasted_iota(jnp.int32, (rows, LANE), 1) < DV - LANE
    b0 = jnp.broadcast_to(c0, (rows, LANE))
    b1 = jnp.broadcast_to(c1, (rows, LANE))
    return jnp.concatenate([b0, jnp.where(first, b0, b1), b1], axis=1)


def _head_rms_scale(o):
    rows = o.shape[0]
    o2 = o * o
    first = lax.broadcasted_iota(jnp.int32, (rows, LANE), 1) < DV - LANE
    mid = o2[:, LANE:2 * LANE]
    ss0 = jnp.sum(o2[:, 0:LANE] + jnp.where(first, mid, 0.0), axis=-1, keepdims=True)
    ss1 = jnp.sum(o2[:, 2 * LANE:3 * LANE] + jnp.where(first, 0.0, mid), axis=-1, keepdims=True)
    return _pair_cols(lax.rsqrt(ss0 / DV + EPS), lax.rsqrt(ss1 / DV + EPS), rows)


def _in_proj_kernel(x_ref, g_ref, w_ref, wg_ref, o_ref, og_ref, h_ref):
    @pl.when(pl.program_id(1) == 0)
    def _():
        x = x_ref[...]
        ms = jnp.mean(x * x, axis=-1, keepdims=True)
        h_ref[...] = (x * lax.rsqrt(ms + EPS) * g_ref[...]).astype(BF16)
        og_ref[...] = _dot(h_ref[...], wg_ref[...])

    o_ref[...] = _dot(h_ref[...], w_ref[...]).astype(o_ref.dtype)


def _in_proj(x2, g, w, layer, tm, tn):
    t, d = x2.shape
    return pl.pallas_call(
        _in_proj_kernel,
        grid=(t // tm, MAIN_W // tn),
        in_specs=[
            pl.BlockSpec((tm, d), lambda i, j: (i, 0)),
            pl.BlockSpec((1, d), lambda i, j: (0, 0)),
            pl.BlockSpec((None, d, tn), lambda i, j: (layer, 0, j)),
            pl.BlockSpec((None, d, GATE_W), lambda i, j: (layer, 0, MAIN_W // GATE_W)),
        ],
        out_specs=[pl.BlockSpec((tm, tn), lambda i, j: (i, j)),
                   pl.BlockSpec((tm, GATE_W), lambda i, j: (i, 0))],
        out_shape=[jax.ShapeDtypeStruct((t, MAIN_W), BF16),
                   jax.ShapeDtypeStruct((t, GATE_W), F32)],
        scratch_shapes=[pltpu.VMEM((tm, d), BF16)],
        compiler_params=pltpu.CompilerParams(
            dimension_semantics=("parallel", "arbitrary"), vmem_limit_bytes=VMEM_LIMIT),
        name="in_proj",
    )(x2, g, w, w)


def _gla_kernel(qk_ref, vg_ref, a1_ref, a2_ref, ab_ref, nrm_ref, ltri_ref, lvl_ref,
                o_ref, s_ref, cum_ref, la_ref):
    s_ref[...] = jnp.zeros_like(s_ref)
    la_ref[0:8, :] = jnp.zeros((8, QK_W), F32)
    la_ref[8 + CHUNK:16 + CHUNK, :] = jnp.zeros((8, QK_W), F32)

    def chunk(c, carry):
        rows = pl.ds(pl.multiple_of(c * CHUNK, CHUNK), CHUNK)
        _gla_chunk(rows, qk_ref, vg_ref, a1_ref, a2_ref, ab_ref, nrm_ref, ltri_ref, lvl_ref,
                   o_ref, s_ref, cum_ref, la_ref)
        return carry

    lax.fori_loop(0, qk_ref.shape[0] // CHUNK, chunk, 0)


def _gla_chunk(rows, qk_ref, vg_ref, a1_ref, a2_ref, ab_ref, nrm_ref, ltri_ref, lvl_ref,
               o_ref, s_ref, cum_ref, la_ref):
    q = qk_ref[rows, 0:QK_W].astype(F32) * (DK ** -0.5)
    k = qk_ref[rows, QK_W:2 * QK_W].astype(F32)
    z = _dot(a1_ref[rows, :].astype(BF16), a2_ref[...]) + ab_ref[...]
    log_a = _log2_sigmoid(z) * (1.0 / GATE_TAU)
    la_ref[8:8 + CHUNK, :] = log_a
    ltri = ltri_ref[...]
    hi, mid, lo = _split3(log_a)
    cum = _dot(ltri, hi) + _dot(ltri, mid) + _dot(ltri, lo)
    cum_ref[...] = cum

    lvl = lvl_ref[...]
    groups = CHUNK // 8
    rows8 = lambda g: slice(8 * g, 8 * g + 8)
    heads = [slice(h * DKP, (h + 1) * DKP) for h in range(HEADS)]
    row = lax.broadcasted_iota(jnp.int32, (CHUNK, QK_W), 0)

    def scores(xq, yk):
        return [_dot_nt(xq[:, hs], yk[:, hs]) for hs in heads]

    qb = q.astype(BF16)
    kb = k.astype(BF16)
    attn = [[jnp.where(lvl[rows8(g), :] == -1, p[rows8(g), :], 0.0) for g in range(groups)]
            for p in scores(qb, kb)]

    def merge(level, parts, q_groups):
        for h in range(HEADS):
            for i, g in enumerate(q_groups):
                attn[h][g] = jnp.where(lvl[rows8(g), :] == level, parts[h][rows8(i), :], attn[h][g])

    e = jnp.exp2(jnp.where((row & 1) == 1, log_a, 0.0))
    merge(0, scores((q * e).astype(BF16), kb), range(groups))
    nxt = la_ref[pl.ds(9, CHUNK), :]
    prv = la_ref[pl.ds(7, CHUNK), :]
    r4 = row & 3
    e = jnp.exp2(jnp.where(r4 == 0, nxt, jnp.where(r4 == 1, 0.0, jnp.where(r4 == 2, log_a, log_a + prv))))
    merge(1, scores((q * e).astype(BF16), (k * e).astype(BF16)), range(groups))
    sub8 = lax.broadcasted_iota(jnp.int32, (8, QK_W), 0)
    pieces = []
    for g in range(groups):
        d = cum[rows8(g), :] - cum_ref[8 * g + 3:8 * g + 4, :]
        pieces.append(jnp.where(sub8 < 4, -d, d))
    e = jnp.exp2(jnp.concatenate(pieces, axis=0))
    merge(2, scores((q * e).astype(BF16), (k * e).astype(BF16)), range(groups))
    for level in range(3, N_LEVELS):
        m = 1 << level
        xq, yk, q_groups = [], [], []
        for base in range(0, CHUNK, 2 * m):
            k_rows = slice(base, base + m)
            q_rows = slice(base + m, base + 2 * m)
            edge = cum_ref[base + m - 1:base + m, :]
            yk += [k[k_rows, :] * jnp.exp2(edge - cum[k_rows, :]), k[q_rows, :]]
            xq.append(q[q_rows, :] * jnp.exp2(cum[q_rows, :] - edge))
            q_groups += range((base + m) // 8, (base + 2 * m) // 8)
        merge(level, scores(jnp.concatenate(xq, axis=0).astype(BF16),
                            jnp.concatenate(yk, axis=0).astype(BF16)), q_groups)

    last = cum_ref[CHUNK - 1:CHUNK, :]
    q_dec = (q * jnp.exp2(cum)).astype(BF16)
    k_dec = k * jnp.exp2(last - cum)
    decay_all = jnp.exp2(last)

    lane = lax.broadcasted_iota(jnp.int32, (CHUNK, PAIR_V), 1)
    lo_mask = lane < DV

    for p in range(HEADS // 2):
        ks = slice(p * PAIR_K, (p + 1) * PAIR_K)
        vs = slice(p * PAIR_V, (p + 1) * PAIR_V)
        vb = vg_ref[rows, vs]
        zero = jnp.zeros_like(vb)
        v_lo = jnp.where(lo_mask, vb, zero)
        v_hi = jnp.where(lo_mask, zero, vb)
        v_blk = jnp.concatenate([v_lo, v_hi], axis=0)
        state = s_ref[p]
        lhs = jnp.concatenate([jnp.concatenate(attn[2 * p], axis=0).astype(BF16),
                               jnp.concatenate(attn[2 * p + 1], axis=0).astype(BF16),
                               q_dec[:, ks]], axis=1)
        rhs = jnp.concatenate([v_blk, state.astype(BF16)], axis=0)
        o = _dot(lhs, rhs)

        kd_t = k_dec[:, ks].T.astype(BF16)
        upd = jnp.concatenate([_dot(kd_t[0:DKP, :], v_lo), _dot(kd_t[DKP:PAIR_K, :], v_hi)], axis=0)
        dcols = []
        for h in (2 * p, 2 * p + 1):
            d_row = jnp.broadcast_to(decay_all[:, h * DKP:(h + 1) * DKP], (DKP, DKP))
            d_col = d_row.T
            dcols.append(jnp.concatenate([d_col] * (PAIR_V // LANE), axis=1))
        s_ref[p] = state * jnp.concatenate(dcols, axis=0) + upd

        gate = vg_ref[rows, V_W + p * PAIR_V:V_W + (p + 1) * PAIR_V].astype(F32)
        out = o * _head_rms_scale(o) * nrm_ref[:, vs] * (gate * jax.nn.sigmoid(gate))
        o_ref[rows, vs] = out.astype(o_ref.dtype)


def _gla(proj, gates, a2p, abp, nrm, ltri, lvl, batch, seq):
    tok = lambda w, cb: pl.BlockSpec((seq, w), lambda b: (b, cb))
    const = lambda shape: pl.BlockSpec(shape, lambda b: (0,) * len(shape))
    assert OFF_GLA_K == OFF_GLA_Q + QK_W and OFF_GLA_G == OFF_GLA_V + V_W
    return pl.pallas_call(
        _gla_kernel,
        grid=(batch,),
        in_specs=[
            tok(2 * QK_W, OFF_GLA_Q // (2 * QK_W)),
            tok(2 * V_W, OFF_GLA_V // (2 * V_W)),
            tok(LANE, (OFF_GLA_A1 - MAIN_W) // LANE),
            const((LANE, QK_W)),
            const((1, QK_W)),
            const((1, V_W)),
            const((CHUNK, CHUNK)),
            const((CHUNK, CHUNK)),
        ],
        out_specs=pl.BlockSpec((seq, V_W), lambda b: (b, 0)),
        out_shape=jax.ShapeDtypeStruct((batch * seq, V_W), BF16),
        scratch_shapes=[
            pltpu.VMEM((HEADS // 2, PAIR_K, PAIR_V), F32),
            pltpu.VMEM((CHUNK, QK_W), F32),
            pltpu.VMEM((CHUNK + 16, QK_W), F32),
        ],
        compiler_params=pltpu.CompilerParams(
            dimension_semantics=("parallel",), vmem_limit_bytes=VMEM_LIMIT),
        name="gla",
    )(proj, proj, gates, a2p, abp, nrm, ltri, lvl)


ST_W = PAIR_V + LANE
TAIL = 8


def _mlstm_kernel(qk_ref, vo_ref, if_ref, cw_ref, ifb_ref, nrm_ref, ltri_ref, lvl_ref,
                  o_ref, c_ref, m_ref, xe_ref):
    c_ref[...] = jnp.zeros_like(c_ref)
    m_ref[...] = jnp.zeros_like(m_ref)
    xe_ref[0:TAIL, :] = jnp.zeros((TAIL, 2 * QK_W), F32)

    def chunk(c, carry):
        rows = pl.ds(pl.multiple_of(c * CHUNK, CHUNK), CHUNK)
        _mlstm_chunk(rows, qk_ref, vo_ref, if_ref, cw_ref, ifb_ref, nrm_ref, ltri_ref, lvl_ref,
                     o_ref, c_ref, m_ref, xe_ref)
        return carry

    lax.fori_loop(0, qk_ref.shape[0] // CHUNK, chunk, 0)


def _mlstm_chunk(rows, qk_ref, vo_ref, if_ref, cw_ref, ifb_ref, nrm_ref, ltri_ref, lvl_ref,
                 o_ref, c_ref, m_ref, xe_ref):
    xe_ref[TAIL:TAIL + CHUNK, :] = qk_ref[rows, :].astype(F32)
    y = jnp.zeros((CHUNK, 2 * QK_W), F32)
    for j in range(CONV_WIDTH):
        y = y + cw_ref[j:j + 1, :] * xe_ref[pl.ds(TAIL - (CONV_WIDTH - 1) + j, CHUNK), :]
    xe_ref[0:TAIL, :] = xe_ref[CHUNK:CHUNK + TAIL, :]
    y = y * jax.nn.sigmoid(y)
    qm = y[:, 0:QK_W]
    km = y[:, QK_W:2 * QK_W] * (DK ** -0.5)
    qb = qm.astype(BF16)
    kb = km.astype(BF16)
    km_t = km.T

    slab = if_ref[rows, :] + ifb_ref[...]
    lane_g = lax.broadcasted_iota(jnp.int32, (CHUNK, LANE), 1)
    gates = jnp.where(lane_g < HEADS, slab, _log_sigmoid(slab))
    ltri = ltri_ref[...]
    g_hi, g_mid, g_lo = _split3(gates)
    cum_col = _dot(ltri, g_hi) + _dot(ltri, g_mid) + _dot(ltri, g_lo)
    gates_t = gates.T[0:2 * HEADS, :]
    t_hi, t_mid, t_lo = _split3(gates_t)
    cum_row = _dot_nt(t_hi, ltri) + _dot_nt(t_mid, ltri) + _dot_nt(t_lo, ltri)

    lvl = lvl_ref[...]
    causal = lvl >= -1
    lane = lax.broadcasted_iota(jnp.int32, (CHUNK, PAIR_V), 1)
    lo_mask = lane < DV
    lane_e = lax.broadcasted_iota(jnp.int32, (CHUNK, LANE), 1)
    one_col = [jnp.where(lane_e == i, 1.0, 0.0).astype(BF16) for i in range(2)]
    lane_r = lax.broadcasted_iota(jnp.int32, (PAIR_K, LANE), 1)
    row_r = lax.broadcasted_iota(jnp.int32, (PAIR_K, LANE), 0)
    ones_blk = jnp.where(lane_r == jnp.where(row_r < CHUNK, 0, 1), 1.0, 0.0).astype(BF16)

    for p in range(HEADS // 2):
        ks = slice(p * PAIR_K, (p + 1) * PAIR_K)
        vs = slice(p * PAIR_V, (p + 1) * PAIR_V)
        vb = vo_ref[rows, vs]
        zero = jnp.zeros_like(vb)
        sc_parts, qs_parts, floor_parts, kw_parts, dprev_parts = [], [], [], [], []
        for h in (2 * p, 2 * p + 1):
            hs = slice(h * DKP, (h + 1) * DKP)
            cc = jnp.broadcast_to(cum_col[:, HEADS + h:HEADS + h + 1], (CHUNK, CHUNK))
            cum_r = cum_row[HEADS + h:HEADS + h + 1, :]
            ib_r = gates_t[h:h + 1, :]
            m_prev = m_ref[h:h + 1, :]
            dmat = jnp.where(causal, cc - cum_r + ib_r, -jnp.inf)
            inter = cc + m_prev
            m_t = jnp.maximum(inter, jnp.max(dmat, axis=-1, keepdims=True))
            w = jnp.exp(dmat - m_t)
            sc_inter = jnp.exp(inter - m_t)
            sc_parts.append((_dot_nt(qb[:, hs], kb[:, hs]) * w).astype(BF16))
            qs_parts.append((qm[:, hs] * sc_inter).astype(BF16))
            floor_parts.append(jnp.exp(-m_t))
            total = cum_r[:, CHUNK - 1:CHUNK]
            g_row = total - cum_r + ib_r
            m_new = jnp.maximum(total + m_prev, jnp.max(g_row, axis=-1, keepdims=True))
            wj = jnp.exp(g_row - m_new)
            dprev_parts.append(jnp.exp(total + m_prev - m_new))
            kw_parts.append((km_t[hs, :] * wj).astype(BF16))
            m_ref[h:h + 1, :] = m_new

        state = c_ref[p]
        st_m = state.astype(BF16)
        v_lo = jnp.where(lo_mask, vb, zero)
        v_hi = jnp.where(lo_mask, zero, vb)
        v_blk = jnp.concatenate([v_lo, v_hi], axis=0)
        rhs = jnp.concatenate([jnp.concatenate([v_blk, ones_blk], axis=1), st_m], axis=0)
        lhs = jnp.concatenate(sc_parts + qs_parts, axis=1)
        res = _dot(lhs, rhs)
        num = res[:, 0:PAIR_V]
        den = _pair_cols(res[:, PAIR_V:PAIR_V + 1], res[:, PAIR_V + 1:PAIR_V + 2], CHUNK)
        floor = _pair_cols(floor_parts[0][:, 0:1], floor_parts[1][:, 0:1], CHUNK)
        hid = num / jnp.maximum(jnp.abs(den), floor)

        upd = jnp.concatenate(
            [_dot(kw_parts[0], jnp.concatenate([v_lo, one_col[0]], axis=1)),
             _dot(kw_parts[1], jnp.concatenate([v_hi, one_col[1]], axis=1))], axis=0)
        d_rows = jnp.concatenate(
            [jnp.broadcast_to(jnp.concatenate([d] * (ST_W // LANE), axis=1), (DKP, ST_W))
             for d in dprev_parts], axis=0)
        c_ref[p] = d_rows * state + upd

        og = vo_ref[rows, V_W + p * PAIR_V:V_W + (p + 1) * PAIR_V].astype(F32)
        out = hid * _head_rms_scale(hid) * nrm_ref[:, vs] * jax.nn.sigmoid(og)
        o_ref[rows, vs] = out.astype(o_ref.dtype)


def _mlstm(proj, gates, cw, ifb, nrm, ltri, lvl, batch, seq):
    tok = lambda w, cb: pl.BlockSpec((seq, w), lambda b: (b, cb))
    const = lambda shape: pl.BlockSpec(shape, lambda b: (0,) * len(shape))
    assert OFF_ML_K == OFF_ML_Q + QK_W and OFF_ML_O == OFF_ML_V + V_W
    return pl.pallas_call(
        _mlstm_kernel,
        grid=(batch,),
        in_specs=[
            tok(2 * QK_W, OFF_ML_Q // (2 * QK_W)),
            tok(2 * V_W, OFF_ML_V // (2 * V_W)),
            tok(LANE, (OFF_ML_IF - MAIN_W) // LANE),
            const((CONV_WIDTH, 2 * QK_W)),
            const((1, LANE)),
            const((1, V_W)),
            const((CHUNK, CHUNK)),
            const((CHUNK, CHUNK)),
        ],
        out_specs=pl.BlockSpec((seq, V_W), lambda b: (b, 0)),
        out_shape=jax.ShapeDtypeStruct((batch * seq, V_W), BF16),
        scratch_shapes=[
            pltpu.VMEM((HEADS // 2, PAIR_K, ST_W), F32),
            pltpu.VMEM((2 * HEADS, LANE), F32),
            pltpu.VMEM((TAIL + CHUNK, 2 * QK_W), F32),
        ],
        compiler_params=pltpu.CompilerParams(
            dimension_semantics=("parallel",), vmem_limit_bytes=VMEM_LIMIT),
        name="mlstm",
    )(proj, proj, gates, cw, ifb, nrm, ltri, lvl)


def _sgu_kernel(u_ref, v_ref, lng_ref, lnb_ref, w_ref, b_ref, lvl_ref, o_ref):
    causal = lvl_ref[...] >= -1
    w_causal = [jnp.where(causal, w_ref[g], 0.0).astype(BF16) for g in range(SGU_GROUPS)]
    for n in range(u_ref.shape[0] // CHUNK):
        rows = slice(n * CHUNK, (n + 1) * CHUNK)
        u = jax.nn.gelu(u_ref[rows, :].astype(F32))
        v = jax.nn.gelu(v_ref[rows, :].astype(F32))
        mu = jnp.mean(v, axis=-1, keepdims=True)
        var = jnp.mean(jnp.square(v - mu), axis=-1, keepdims=True)
        vn = ((v - mu) * lax.rsqrt(var + EPS) * lng_ref[...] + lnb_ref[...]).astype(BF16)
        for g in range(SGU_GROUPS):
            gs = slice(g * SGU_CH, (g + 1) * SGU_CH)
            mixed = _dot(w_causal[g], vn[:, gs]) + b_ref[:, gs]
            o_ref[rows, gs] = (u[:, gs] * mixed).astype(o_ref.dtype)


def _sgu(proj, lng, lnb, w, b_full, lvl, batch, seq):
    step = CHUNK * SGU_STEP
    while (batch * seq) % step:
        step //= 2
    nb = batch * seq // step
    tok = lambda w_, cb: pl.BlockSpec((step, w_), lambda i: (i, cb))
    const = lambda shape: pl.BlockSpec(shape, lambda i: (0,) * len(shape))
    return pl.pallas_call(
        _sgu_kernel,
        grid=(nb,),
        in_specs=[
            tok(SGU_W, OFF_SGU_U // SGU_W),
            tok(SGU_W, OFF_SGU_V // SGU_W),
            const((1, SGU_W)),
            const((1, SGU_W)),
            const((SGU_GROUPS, CHUNK, CHUNK)),
            const((CHUNK, SGU_W)),
            const((CHUNK, CHUNK)),
        ],
        out_specs=pl.BlockSpec((step, SGU_W), lambda i: (i, 0)),
        out_shape=jax.ShapeDtypeStruct((batch * seq, SGU_W), BF16),
        compiler_params=pltpu.CompilerParams(
            dimension_semantics=("parallel",), vmem_limit_bytes=VMEM_LIMIT),
        name="sgu",
    )(proj, proj, lng, lnb, w, b_full, lvl)


def _out_proj_kernel(x_ref, a_ref, b_ref, c_ref, wa_ref, wb_ref, wc_ref, o_ref):
    o_ref[...] = (x_ref[...] + _dot(a_ref[...], wa_ref[...]) + _dot(b_ref[...], wb_ref[...])
                  + _dot(c_ref[...], wc_ref[...]))


def _out_proj(x2, mix_a, mix_b, mix_c, w_out, layer, tm, tn):
    t, d = x2.shape
    return pl.pallas_call(
        _out_proj_kernel,
        grid=(d // tn, t // tm),
        in_specs=[
            pl.BlockSpec((tm, tn), lambda j, i: (i, j)),
            pl.BlockSpec((tm, V_W), lambda j, i: (i, 0)),
            pl.BlockSpec((tm, V_W), lambda j, i: (i, 0)),
            pl.BlockSpec((tm, SGU_W), lambda j, i: (i, 0)),
            pl.BlockSpec((None, V_W, tn), lambda j, i: (layer, 0, j)),
            pl.BlockSpec((None, V_W, tn), lambda j, i: (layer, 1, j)),
            pl.BlockSpec((None, SGU_W, tn), lambda j, i: (layer, 2 * V_W // SGU_W, j)),
        ],
        out_specs=pl.BlockSpec((tm, tn), lambda j, i: (i, j)),
        out_shape=jax.ShapeDtypeStruct((t, d), F32),
        compiler_params=pltpu.CompilerParams(
            dimension_semantics=("parallel", "arbitrary"), vmem_limit_bytes=VMEM_LIMIT),
        name="out_proj",
    )(x2, mix_a, mix_b, mix_c, w_out, w_out, w_out)


def _ffn_kernel(x_ref, g_ref, wg_ref, wu_ref, wd_ref, gf_ref, o_ref, h_ref, *, final_norm):
    j = pl.program_id(1)

    @pl.when(j == 0)
    def _():
        x = x_ref[...]
        ms = jnp.mean(x * x, axis=-1, keepdims=True)
        h_ref[...] = (x * lax.rsqrt(ms + EPS) * g_ref[...]).astype(BF16)
        o_ref[...] = x

    h = h_ref[...]
    gate = _dot(h, wg_ref[...])
    up = _dot(h, wu_ref[...])
    act = (gate * jax.nn.sigmoid(gate) * up).astype(BF16)
    o_ref[...] += _dot(act, wd_ref[...])

    if final_norm:
        @pl.when(j == pl.num_programs(1) - 1)
        def _():
            y = o_ref[...]
            ms = jnp.mean(y * y, axis=-1, keepdims=True)
            o_ref[...] = y * lax.rsqrt(ms + EPS) * gf_ref[...]


def _ffn(x2, g, w_gu, w_down, g_final, layer, tm, tf, final_norm):
    t, d = x2.shape
    nf = D_FF // tf
    return pl.pallas_call(
        functools.partial(_ffn_kernel, final_norm=final_norm),
        grid=(t // tm, nf),
        in_specs=[
            pl.BlockSpec((tm, d), lambda i, j: (i, 0)),
            pl.BlockSpec((1, d), lambda i, j: (0, 0)),
            pl.BlockSpec((None, d, tf), lambda i, j: (layer, 0, j)),
            pl.BlockSpec((None, d, tf), lambda i, j: (layer, 0, nf + j)),
            pl.BlockSpec((None, tf, d), lambda i, j: (layer, j, 0)),
            pl.BlockSpec((1, d), lambda i, j: (0, 0)),
        ],
        out_specs=pl.BlockSpec((tm, d), lambda i, j: (i, 0)),
        out_shape=jax.ShapeDtypeStruct((t, d), F32),
        scratch_shapes=[pltpu.VMEM((tm, d), BF16)],
        compiler_params=pltpu.CompilerParams(
            dimension_semantics=("parallel", "arbitrary"), vmem_limit_bytes=VMEM_LIMIT),
        name="ffn",
    )(x2, g, w_gu, w_gu, w_down, g_final)


def _tiles(tokens):
    tm = 1024
    while tokens % tm:
        tm //= 2
    return tm


def kernel(x, norm_mix, w_in, gla_a2, gla_ab, gla_norm, ml_conv, ml_ib, ml_fb, ml_norm,
           sgu_ln_g, sgu_ln_b, sgu_w, sgu_b, w_out, norm_ffn, w_gu, w_down, norm_final):
    batch, seq, d = x.shape
    depth = w_in.shape[0]
    tokens = batch * seq
    tm = _tiles(tokens)

    lvl = jnp.asarray(_pair_level_matrix())
    ltri = jnp.asarray(np.tril(np.ones((CHUNK, CHUNK), np.float32)), BF16)

    w_in_b = _relayout_w_in(w_in)
    w_out_b = w_out.astype(BF16)
    w_gu_b = w_gu.astype(BF16)
    w_down_b = w_down.astype(BF16)

    xc = x.reshape(tokens, d)
    for l in range(depth):
        proj, gates = _in_proj(xc, norm_mix[l][None, :], w_in_b, l, tm, 1536)

        a2p = jnp.pad(_pad_heads(gla_a2[l]), ((0, LANE - GATE_RANK), (0, 0))).astype(BF16)
        abp = _pad_heads(gla_ab[l])[None, :]
        mix_a = _gla(proj, gates, a2p, abp, gla_norm[l][None, :], ltri, lvl, batch, seq)

        cw = jnp.concatenate([_pad_heads(ml_conv[l][:, :HEADS * DK]),
                              _pad_heads(ml_conv[l][:, HEADS * DK:])], axis=1)
        ifb = jnp.pad(jnp.concatenate([ml_ib[l], ml_fb[l]]), (0, LANE - 2 * HEADS))[None, :]
        mix_b = _mlstm(proj, gates, cw, ifb, ml_norm[l][None, :], ltri, lvl, batch, seq)

        b_full = jnp.repeat(sgu_b[l].T, SGU_CH, axis=1)
        mix_c = _sgu(proj, sgu_ln_g[l][None, :], sgu_ln_b[l][None, :], sgu_w[l], b_full, lvl, batch, seq)

        x1 = _out_proj(xc, mix_a, mix_b, mix_c, w_out_b, l, tm, 1024)
        xc = _ffn(x1, norm_ffn[l][None, :], w_gu_b, w_down_b, norm_final[None, :], l, tm, 512,
                  final_norm=(l == depth - 1))
    return xc.reshape(batch, seq, d)
```

```python
import functools

import numpy as np
import jax
import jax.numpy as jnp
from jax import lax
from jax.experimental import pallas as pl
from jax.experimental.pallas import tpu as pltpu

F32 = jnp.float32
BF16 = jnp.bfloat16

D_MODEL = 2048
HEADS = 4
DK = 96
DKP = 128
DV = 192
QK_W = HEADS * DKP
V_W = HEADS * DV
PAIR_K = 2 * DKP
PAIR_V = 2 * DV
SGU_W = 512
SGU_GROUPS = 4
SGU_CH = 128
GATE_RANK = 16
GATE_TAU = 16.0
CONV_WIDTH = 4
D_FF = 5632
EPS = 1e-6
CHUNK = 128
N_LEVELS = 7
LANE = 128

OFF_GLA_Q, OFF_GLA_K, OFF_ML_Q, OFF_ML_K = 0, 512, 1024, 1536
OFF_SGU_U, OFF_SGU_V = 2048, 2560
OFF_GLA_V, OFF_GLA_G, OFF_ML_V, OFF_ML_O = 3072, 3840, 4608, 5376
MAIN_W = 6144
OFF_GLA_A1, OFF_ML_IF = 6144, 6272
GATE_W = 2 * LANE
PROJ_W = MAIN_W + GATE_W
SGU_STEP = 4

VMEM_LIMIT = 56 * 1024 * 1024


def _relayout_w_in(w_in):
    gk = HEADS * DK
    sizes = (gk, gk, V_W, V_W, GATE_RANK, gk, gk, V_W, V_W, HEADS, HEADS, SGU_W, SGU_W)
    starts = np.concatenate([[0], np.cumsum(sizes)]).tolist()
    w = w_in.astype(BF16)
    (gq, gkk, gv, gg, ga1, mq, mk, mv, mo, mi, mf, su, sv) = [
        w[..., starts[n]:starts[n + 1]] for n in range(len(sizes))]
    pad_to = lambda a, n: jnp.pad(a, [(0, 0)] * (a.ndim - 1) + [(0, n - a.shape[-1])])
    cols = [_pad_heads(gq), _pad_heads(gkk), _pad_heads(mq), _pad_heads(mk), su, sv, gv, gg, mv, mo,
            pad_to(ga1, LANE), pad_to(jnp.concatenate([mi, mf], axis=-1), LANE)]
    out = jnp.concatenate(cols, axis=-1)
    assert out.shape[-1] == PROJ_W
    return out


def _pad_heads(a):
    lead = a.shape[:-1]
    a = a.reshape(lead + (HEADS, DK))
    a = jnp.pad(a, [(0, 0)] * len(lead) + [(0, 0), (0, DKP - DK)])
    return a.reshape(lead + (QK_W,))


def _pair_level_matrix():
    c = CHUNK
    t = np.arange(c)[:, None]
    s = np.arange(c)[None, :]
    x = np.bitwise_xor(t, s)
    lvl = np.floor(np.log2(np.maximum(x, 1))).astype(np.int32)
    lvl = np.where(s == t, -1, lvl)
    lvl = np.where(s > t, -2, lvl)
    return lvl.astype(np.int32)


def _dot(a, b):
    return jnp.dot(a, b, preferred_element_type=F32)


def _dot_nt(a, b):
    return lax.dot_general(a, b, (((1,), (1,)), ((), ())), preferred_element_type=F32)


def _split3(x):
    hi = x.astype(BF16)
    r = x - hi.astype(F32)
    mid = r.astype(BF16)
    lo = (r - mid.astype(F32)).astype(BF16)
    return hi, mid, lo


LOG2E = 1.4426950408889634
LN2 = 0.6931471805599453


def _log2_sigmoid(x):
    xl = x * LOG2E
    return jnp.minimum(xl, 0.0) - jnp.log2(1.0 + jnp.exp2(-jnp.abs(xl)))


def _log_sigmoid(x):
    return _log2_sigmoid(x) * LN2


def _pair_cols(c0, c1, rows):
    first = lax.broadcasted_iota(jnp.int32, (rows, LANE), 1) < DV - LANE
    b0 = jnp.broadcast_to(c0, (rows, LANE))
    b1 = jnp.broadcast_to(c1, (rows, LANE))
    return jnp.concatenate([b0, jnp.where(first, b0, b1), b1], axis=1)


def _head_rms_scale(o):
    rows = o.shape[0]
    o2 = o * o
    first = lax.broadcasted_iota(jnp.int32, (rows, LANE), 1) < DV - LANE
    mid = o2[:, LANE:2 * LANE]
    ss0 = jnp.sum(o2[:, 0:LANE] + jnp.where(first, mid, 0.0), axis=-1, keepdims=True)
    ss1 = jnp.sum(o2[:, 2 * LANE:3 * LANE] + jnp.where(first, 0.0, mid), axis=-1, keepdims=True)
    return _pair_cols(lax.rsqrt(ss0 / DV + EPS), lax.rsqrt(ss1 / DV + EPS), rows)


def _in_proj_kernel(x_ref, g_ref, w_ref, wg_ref, o_ref, og_ref, h_ref):
    @pl.when(pl.program_id(1) == 0)
    def _():
        x = x_ref[...]
        ms = jnp.mean(x * x, axis=-1, keepdims=True)
        h_ref[...] = (x * lax.rsqrt(ms + EPS) * g_ref[...]).astype(BF16)
        og_ref[...] = _dot(h_ref[...], wg_ref[...])

    o_ref[...] = _dot(h_ref[...], w_ref[...]).astype(o_ref.dtype)


def _in_proj(x2, g, w, layer, tm, tn):
    t, d = x2.shape
    return pl.pallas_call(
        _in_proj_kernel,
        grid=(t // tm, MAIN_W // tn),
        in_specs=[
            pl.BlockSpec((tm, d), lambda i, j: (i, 0)),
            pl.BlockSpec((1, d), lambda i, j: (0, 0)),
            pl.BlockSpec((None, d, tn), lambda i, j: (layer, 0, j)),
            pl.BlockSpec((None, d, GATE_W), lambda i, j: (layer, 0, MAIN_W // GATE_W)),
        ],
        out_specs=[pl.BlockSpec((tm, tn), lambda i, j: (i, j)),
                   pl.BlockSpec((tm, GATE_W), lambda i, j: (i, 0))],
        out_shape=[jax.ShapeDtypeStruct((t, MAIN_W), BF16),
                   jax.ShapeDtypeStruct((t, GATE_W), F32)],
        scratch_shapes=[pltpu.VMEM((tm, d), BF16)],
        compiler_params=pltpu.CompilerParams(
            dimension_semantics=("parallel", "arbitrary"), vmem_limit_bytes=VMEM_LIMIT),
        name="in_proj",
    )(x2, g, w, w)


def _gla_kernel(qk_ref, vg_ref, a1_ref, a2_ref, ab_ref, nrm_ref, ltri_ref, lvl_ref,
                o_ref, s_ref, cum_ref, la_ref):
    s_ref[...] = jnp.zeros_like(s_ref)
    la_ref[0:8, :] = jnp.zeros((8, QK_W), F32)
    la_ref[8 + CHUNK:16 + CHUNK, :] = jnp.zeros((8, QK_W), F32)

    def chunk(c, carry):
        rows = pl.ds(pl.multiple_of(c * CHUNK, CHUNK), CHUNK)
        _gla_chunk(rows, qk_ref, vg_ref, a1_ref, a2_ref, ab_ref, nrm_ref, ltri_ref, lvl_ref,
                   o_ref, s_ref, cum_ref, la_ref)
        return carry

    lax.fori_loop(0, qk_ref.shape[0] // CHUNK, chunk, 0)


def _gla_chunk(rows, qk_ref, vg_ref, a1_ref, a2_ref, ab_ref, nrm_ref, ltri_ref, lvl_ref,
               o_ref, s_ref, cum_ref, la_ref):
    q = qk_ref[rows, 0:QK_W].astype(F32) * (DK ** -0.5)
    k = qk_ref[rows, QK_W:2 * QK_W].astype(F32)
    z = _dot(a1_ref[rows, :].astype(BF16), a2_ref[...]) + ab_ref[...]
    log_a = _log2_sigmoid(z) * (1.0 / GATE_TAU)
    la_ref[8:8 + CHUNK, :] = log_a
    ltri = ltri_ref[...]
    hi, mid, lo = _split3(log_a)
    cum = _dot(ltri, hi) + _dot(ltri, mid) + _dot(ltri, lo)
    cum_ref[...] = cum

    lvl = lvl_ref[...]
    groups = CHUNK // 8
    rows8 = lambda g: slice(8 * g, 8 * g + 8)
    heads = [slice(h * DKP, (h + 1) * DKP) for h in range(HEADS)]
    row = lax.broadcasted_iota(jnp.int32, (CHUNK, QK_W), 0)

    def scores(xq, yk):
        return [_dot_nt(xq[:, hs], yk[:, hs]) for hs in heads]

    qb = q.astype(BF16)
    kb = k.astype(BF16)
    attn = [[jnp.where(lvl[rows8(g), :] == -1, p[rows8(g), :], 0.0) for g in range(groups)]
            for p in scores(qb, kb)]

    def merge(level, parts, q_groups):
        for h in range(HEADS):
            for i, g in enumerate(q_groups):
                attn[h][g] = jnp.where(lvl[rows8(g), :] == level, parts[h][rows8(i), :], attn[h][g])

    e = jnp.exp2(jnp.where((row & 1) == 1, log_a, 0.0)).astype(BF16)
    merge(0, scores(qb * e, kb), range(groups))
    nxt = la_ref[pl.ds(9, CHUNK), :]
    prv = la_ref[pl.ds(7, CHUNK), :]
    r4 = row & 3
    e = jnp.exp2(jnp.where(r4 == 0, nxt, jnp.where(r4 == 1, 0.0, jnp.where(r4 == 2, log_a, log_a + prv))))
    e = e.astype(BF16)
    merge(1, scores(qb * e, kb * e), range(groups))
    sub8 = lax.broadcasted_iota(jnp.int32, (8, QK_W), 0)
    pieces = []
    for g in range(groups):
        d = cum[rows8(g), :] - cum_ref[8 * g + 3:8 * g + 4, :]
        pieces.append(jnp.where(sub8 < 4, -d, d))
    e = jnp.exp2(jnp.concatenate(pieces, axis=0)).astype(BF16)
    merge(2, scores(qb * e, kb * e), range(groups))
    for level in range(3, N_LEVELS):
        m = 1 << level
        xq, yk, q_groups = [], [], []
        for base in range(0, CHUNK, 2 * m):
            k_rows = slice(base, base + m)
            q_rows = slice(base + m, base + 2 * m)
            edge = cum_ref[base + m - 1:base + m, :]
            yk += [k[k_rows, :] * jnp.exp2(edge - cum[k_rows, :]), k[q_rows, :]]
            xq.append(q[q_rows, :] * jnp.exp2(cum[q_rows, :] - edge))
            q_groups += range((base + m) // 8, (base + 2 * m) // 8)
        merge(level, scores(jnp.concatenate(xq, axis=0).astype(BF16),
                            jnp.concatenate(yk, axis=0).astype(BF16)), q_groups)

    last = cum_ref[CHUNK - 1:CHUNK, :]
    q_dec = qb * jnp.exp2(cum).astype(BF16)
    k_dec = k * jnp.exp2(last - cum)
    decay_all = jnp.exp2(last)

    lane = lax.broadcasted_iota(jnp.int32, (CHUNK, PAIR_V), 1)
    lo_mask = lane < DV

    for p in range(HEADS // 2):
        ks = slice(p * PAIR_K, (p + 1) * PAIR_K)
        vs = slice(p * PAIR_V, (p + 1) * PAIR_V)
        vb = vg_ref[rows, vs]
        zero = jnp.zeros_like(vb)
        v_lo = jnp.where(lo_mask, vb, zero)
        v_hi = jnp.where(lo_mask, zero, vb)
        v_blk = jnp.concatenate([v_lo, v_hi], axis=0)
        state = s_ref[p]
        lhs = jnp.concatenate([jnp.concatenate(attn[2 * p], axis=0).astype(BF16),
                               jnp.concatenate(attn[2 * p + 1], axis=0).astype(BF16),
                               q_dec[:, ks]], axis=1)
        rhs = jnp.concatenate([v_blk, state.astype(BF16)], axis=0)
        o = _dot(lhs, rhs)

        kd_t = k_dec[:, ks].T.astype(BF16)
        upd = jnp.concatenate([_dot(kd_t[0:DKP, :], v_lo), _dot(kd_t[DKP:PAIR_K, :], v_hi)], axis=0)
        dcols = []
        for h in (2 * p, 2 * p + 1):
            d_row = jnp.broadcast_to(decay_all[:, h * DKP:(h + 1) * DKP], (DKP, DKP))
            d_col = d_row.T
            dcols.append(jnp.concatenate([d_col] * (PAIR_V // LANE), axis=1))
        s_ref[p] = state * jnp.concatenate(dcols, axis=0) + upd

        gate = vg_ref[rows, V_W + p * PAIR_V:V_W + (p + 1) * PAIR_V].astype(F32)
        out = o * _head_rms_scale(o) * nrm_ref[:, vs] * (gate * jax.nn.sigmoid(gate))
        o_ref[rows, vs] = out.astype(o_ref.dtype)


def _gla(proj, gates, a2p, abp, nrm, ltri, lvl, batch, seq):
    tok = lambda w, cb: pl.BlockSpec((seq, w), lambda b: (b, cb))
    const = lambda shape: pl.BlockSpec(shape, lambda b: (0,) * len(shape))
    assert OFF_GLA_K == OFF_GLA_Q + QK_W and OFF_GLA_G == OFF_GLA_V + V_W
    return pl.pallas_call(
        _gla_kernel,
        grid=(batch,),
        in_specs=[
            tok(2 * QK_W, OFF_GLA_Q // (2 * QK_W)),
            tok(2 * V_W, OFF_GLA_V // (2 * V_W)),
            tok(LANE, (OFF_GLA_A1 - MAIN_W) // LANE),
            const((LANE, QK_W)),
            const((1, QK_W)),
            const((1, V_W)),
            const((CHUNK, CHUNK)),
            const((CHUNK, CHUNK)),
        ],
        out_specs=pl.BlockSpec((seq, V_W), lambda b: (b, 0)),
        out_shape=jax.ShapeDtypeStruct((batch * seq, V_W), BF16),
        scratch_shapes=[
            pltpu.VMEM((HEADS // 2, PAIR_K, PAIR_V), F32),
            pltpu.VMEM((CHUNK, QK_W), F32),
            pltpu.VMEM((CHUNK + 16, QK_W), F32),
        ],
        compiler_params=pltpu.CompilerParams(
            dimension_semantics=("parallel",), vmem_limit_bytes=VMEM_LIMIT),
        name="gla",
    )(proj, proj, gates, a2p, abp, nrm, ltri, lvl)


ST_W = PAIR_V + LANE


def _conv_shift_matrix():
    c = CHUNK
    mat = np.zeros(((CONV_WIDTH - 1) * c, 2 * c), np.float32)
    for j in range(CONV_WIDTH - 1):
        for t in range(c):
            mat[j * c + t, c + t - (CONV_WIDTH - 1 - j)] = 1.0
    return mat


def _mlstm_kernel(qk_ref, vo_ref, if_ref, cw_ref, ifb_ref, nrm_ref, ltri_ref, lvl_ref, shift_ref,
                  o_ref, c_ref, m_ref, xp_ref):
    c_ref[...] = jnp.zeros_like(c_ref)
    m_ref[...] = jnp.zeros_like(m_ref)
    xp_ref[...] = jnp.zeros_like(xp_ref)

    def chunk(c, carry):
        rows = pl.ds(pl.multiple_of(c * CHUNK, CHUNK), CHUNK)
        _mlstm_chunk(rows, qk_ref, vo_ref, if_ref, cw_ref, ifb_ref, nrm_ref, ltri_ref, lvl_ref, shift_ref,
                     o_ref, c_ref, m_ref, xp_ref)
        return carry

    lax.fori_loop(0, qk_ref.shape[0] // CHUNK, chunk, 0)


def _mlstm_chunk(rows, qk_ref, vo_ref, if_ref, cw_ref, ifb_ref, nrm_ref, ltri_ref, lvl_ref, shift_ref,
                 o_ref, c_ref, m_ref, xp_ref):
    assert qk_ref.dtype == BF16
    x_cur = qk_ref[rows, :]
    shifted = _dot(shift_ref[...], jnp.concatenate([xp_ref[...], x_cur], axis=0))
    xp_ref[...] = x_cur
    y = cw_ref[CONV_WIDTH - 1:CONV_WIDTH, :] * x_cur.astype(F32)
    for j in range(CONV_WIDTH - 1):
        y = y + cw_ref[j:j + 1, :] * shifted[j * CHUNK:(j + 1) * CHUNK, :]
    y = y * jax.nn.sigmoid(y)
    qm = y[:, 0:QK_W]
    km = y[:, QK_W:2 * QK_W] * (DK ** -0.5)
    qb = qm.astype(BF16)
    kb = km.astype(BF16)
    km_t = km.T

    slab = if_ref[rows, :] + ifb_ref[...]
    lane_g = lax.broadcasted_iota(jnp.int32, (CHUNK, LANE), 1)
    gates = jnp.where(lane_g < HEADS, slab, _log_sigmoid(slab))
    ltri = ltri_ref[...]
    g_hi, g_mid, g_lo = _split3(gates)
    cum_col = _dot(ltri, g_hi) + _dot(ltri, g_mid) + _dot(ltri, g_lo)
    gates_t = gates.T[0:2 * HEADS, :]
    t_hi, t_mid, t_lo = _split3(gates_t)
    cum_row = _dot_nt(t_hi, ltri) + _dot_nt(t_mid, ltri) + _dot_nt(t_lo, ltri)

    lvl = lvl_ref[...]
    causal = lvl >= -1
    lane = lax.broadcasted_iota(jnp.int32, (CHUNK, PAIR_V), 1)
    lo_mask = lane < DV
    lane_e = lax.broadcasted_iota(jnp.int32, (CHUNK, LANE), 1)
    one_col = [jnp.where(lane_e == i, 1.0, 0.0).astype(BF16) for i in range(2)]
    lane_r = lax.broadcasted_iota(jnp.int32, (PAIR_K, LANE), 1)
    row_r = lax.broadcasted_iota(jnp.int32, (PAIR_K, LANE), 0)
    ones_blk = jnp.where(lane_r == jnp.where(row_r < CHUNK, 0, 1), 1.0, 0.0).astype(BF16)

    for p in range(HEADS // 2):
        ks = slice(p * PAIR_K, (p + 1) * PAIR_K)
        vs = slice(p * PAIR_V, (p + 1) * PAIR_V)
        vb = vo_ref[rows, vs]
        zero = jnp.zeros_like(vb)
        sc_parts, qs_parts, floor_parts, kw_parts, dprev_parts = [], [], [], [], []
        for h in (2 * p, 2 * p + 1):
            hs = slice(h * DKP, (h + 1) * DKP)
            cc = jnp.broadcast_to(cum_col[:, HEADS + h:HEADS + h + 1], (CHUNK, CHUNK))
            cum_r = cum_row[HEADS + h:HEADS + h + 1, :]
            ib_r = gates_t[h:h + 1, :]
            m_prev = m_ref[h:h + 1, :]
            dmat = jnp.where(causal, cc - cum_r + ib_r, -jnp.inf)
            inter = cc + m_prev
            m_t = jnp.maximum(inter, jnp.max(dmat, axis=-1, keepdims=True))
            w = jnp.exp(dmat - m_t)
            sc_inter = jnp.exp(inter - m_t)
            sc_parts.append((_dot_nt(qb[:, hs], kb[:, hs]) * w).astype(BF16))
            qs_parts.append((qm[:, hs] * sc_inter).astype(BF16))
            floor_parts.append(jnp.exp(-m_t))
            total = cum_r[:, CHUNK - 1:CHUNK]
            g_row = total - cum_r + ib_r
            m_new = jnp.maximum(total + m_prev, jnp.max(g_row, axis=-1, keepdims=True))
            wj = jnp.exp(g_row - m_new)
            dprev_parts.append(jnp.exp(total + m_prev - m_new))
            kw_parts.append((km_t[hs, :] * wj).astype(BF16))
            m_ref[h:h + 1, :] = m_new

        state = c_ref[p]
        st_m = state.astype(BF16)
        v_lo = jnp.where(lo_mask, vb, zero)
        v_hi = jnp.where(lo_mask, zero, vb)
        v_blk = jnp.concatenate([v_lo, v_hi], axis=0)
        rhs = jnp.concatenate([jnp.concatenate([v_blk, ones_blk], axis=1), st_m], axis=0)
        lhs = jnp.concatenate(sc_parts + qs_parts, axis=1)
        res = _dot(lhs, rhs)
        num = res[:, 0:PAIR_V]
        den = _pair_cols(res[:, PAIR_V:PAIR_V + 1], res[:, PAIR_V + 1:PAIR_V + 2], CHUNK)
        floor = _pair_cols(floor_parts[0][:, 0:1], floor_parts[1][:, 0:1], CHUNK)
        hid = num / jnp.maximum(jnp.abs(den), floor)

        upd = jnp.concatenate(
            [_dot(kw_parts[0], jnp.concatenate([v_lo, one_col[0]], axis=1)),
             _dot(kw_parts[1], jnp.concatenate([v_hi, one_col[1]], axis=1))], axis=0)
        d_rows = jnp.concatenate(
            [jnp.broadcast_to(jnp.concatenate([d] * (ST_W // LANE), axis=1), (DKP, ST_W))
             for d in dprev_parts], axis=0)
        c_ref[p] = d_rows * state + upd

        og = vo_ref[rows, V_W + p * PAIR_V:V_W + (p + 1) * PAIR_V].astype(F32)
        out = hid * _head_rms_scale(hid) * nrm_ref[:, vs] * jax.nn.sigmoid(og)
        o_ref[rows, vs] = out.astype(o_ref.dtype)


def _mlstm(proj, gates, cw, ifb, nrm, ltri, lvl, shift, batch, seq):
    tok = lambda w, cb: pl.BlockSpec((seq, w), lambda b: (b, cb))
    const = lambda shape: pl.BlockSpec(shape, lambda b: (0,) * len(shape))
    assert OFF_ML_K == OFF_ML_Q + QK_W and OFF_ML_O == OFF_ML_V + V_W
    return pl.pallas_call(
        _mlstm_kernel,
        grid=(batch,),
        in_specs=[
            tok(2 * QK_W, OFF_ML_Q // (2 * QK_W)),
            tok(2 * V_W, OFF_ML_V // (2 * V_W)),
            tok(LANE, (OFF_ML_IF - MAIN_W) // LANE),
            const((CONV_WIDTH, 2 * QK_W)),
            const((1, LANE)),
            const((1, V_W)),
            const((CHUNK, CHUNK)),
            const((CHUNK, CHUNK)),
            const(((CONV_WIDTH - 1) * CHUNK, 2 * CHUNK)),
        ],
        out_specs=pl.BlockSpec((seq, V_W), lambda b: (b, 0)),
        out_shape=jax.ShapeDtypeStruct((batch * seq, V_W), BF16),
        scratch_shapes=[
            pltpu.VMEM((HEADS // 2, PAIR_K, ST_W), F32),
            pltpu.VMEM((2 * HEADS, LANE), F32),
            pltpu.VMEM((CHUNK, 2 * QK_W), BF16),
        ],
        compiler_params=pltpu.CompilerParams(
            dimension_semantics=("parallel",), vmem_limit_bytes=VMEM_LIMIT),
        name="mlstm",
    )(proj, proj, gates, cw, ifb, nrm, ltri, lvl, shift)


def _sgu_kernel(u_ref, v_ref, lng_ref, lnb_ref, w_ref, b_ref, lvl_ref, o_ref):
    causal = lvl_ref[...] >= -1
    w_causal = [jnp.where(causal, w_ref[g], 0.0).astype(BF16) for g in range(SGU_GROUPS)]
    for n in range(u_ref.shape[0] // CHUNK):
        rows = slice(n * CHUNK, (n + 1) * CHUNK)
        u = jax.nn.gelu(u_ref[rows, :].astype(F32))
        v = jax.nn.gelu(v_ref[rows, :].astype(F32))
        mu = jnp.mean(v, axis=-1, keepdims=True)
        var = jnp.mean(jnp.square(v - mu), axis=-1, keepdims=True)
        vn = ((v - mu) * lax.rsqrt(var + EPS) * lng_ref[...] + lnb_ref[...]).astype(BF16)
        for g in range(SGU_GROUPS):
            gs = slice(g * SGU_CH, (g + 1) * SGU_CH)
            mixed = _dot(w_causal[g], vn[:, gs]) + b_ref[:, gs]
            o_ref[rows, gs] = (u[:, gs] * mixed).astype(o_ref.dtype)


def _sgu(proj, lng, lnb, w, b_full, lvl, batch, seq):
    step = CHUNK * SGU_STEP
    while (batch * seq) % step:
        step //= 2
    nb = batch * seq // step
    tok = lambda w_, cb: pl.BlockSpec((step, w_), lambda i: (i, cb))
    const = lambda shape: pl.BlockSpec(shape, lambda i: (0,) * len(shape))
    return pl.pallas_call(
        _sgu_kernel,
        grid=(nb,),
        in_specs=[
            tok(SGU_W, OFF_SGU_U // SGU_W),
            tok(SGU_W, OFF_SGU_V // SGU_W),
            const((1, SGU_W)),
            const((1, SGU_W)),
            const((SGU_GROUPS, CHUNK, CHUNK)),
            const((CHUNK, SGU_W)),
            const((CHUNK, CHUNK)),
        ],
        out_specs=pl.BlockSpec((step, SGU_W), lambda i: (i, 0)),
        out_shape=jax.ShapeDtypeStruct((batch * seq, SGU_W), BF16),
        compiler_params=pltpu.CompilerParams(
            dimension_semantics=("parallel",), vmem_limit_bytes=VMEM_LIMIT),
        name="sgu",
    )(proj, proj, lng, lnb, w, b_full, lvl)


def _out_proj_kernel(x_ref, a_ref, b_ref, c_ref, wa_ref, wb_ref, wc_ref, o_ref):
    o_ref[...] = (x_ref[...] + _dot(a_ref[...], wa_ref[...]) + _dot(b_ref[...], wb_ref[...])
                  + _dot(c_ref[...], wc_ref[...]))


def _out_proj(x2, mix_a, mix_b, mix_c, w_out, layer, tm, tn):
    t, d = x2.shape
    return pl.pallas_call(
        _out_proj_kernel,
        grid=(d // tn, t // tm),
        in_specs=[
            pl.BlockSpec((tm, tn), lambda j, i: (i, j)),
            pl.BlockSpec((tm, V_W), lambda j, i: (i, 0)),
            pl.BlockSpec((tm, V_W), lambda j, i: (i, 0)),
            pl.BlockSpec((tm, SGU_W), lambda j, i: (i, 0)),
            pl.BlockSpec((None, V_W, tn), lambda j, i: (layer, 0, j)),
            pl.BlockSpec((None, V_W, tn), lambda j, i: (layer, 1, j)),
            pl.BlockSpec((None, SGU_W, tn), lambda j, i: (layer, 2 * V_W // SGU_W, j)),
        ],
        out_specs=pl.BlockSpec((tm, tn), lambda j, i: (i, j)),
        out_shape=jax.ShapeDtypeStruct((t, d), F32),
        compiler_params=pltpu.CompilerParams(
            dimension_semantics=("parallel", "arbitrary"), vmem_limit_bytes=VMEM_LIMIT),
        name="out_proj",
    )(x2, mix_a, mix_b, mix_c, w_out, w_out, w_out)


def _ffn_kernel(x_ref, g_ref, wg_ref, wu_ref, wd_ref, gf_ref, o_ref, h_ref, *, final_norm):
    j = pl.program_id(1)

    @pl.when(j == 0)
    def _():
        x = x_ref[...]
        ms = jnp.mean(x * x, axis=-1, keepdims=True)
        h_ref[...] = (x * lax.rsqrt(ms + EPS) * g_ref[...]).astype(BF16)
        o_ref[...] = x

    h = h_ref[...]
    gate = _dot(h, wg_ref[...])
    up = _dot(h, wu_ref[...])
    act = (gate * jax.nn.sigmoid(gate) * up).astype(BF16)
    o_ref[...] += _dot(act, wd_ref[...])

    if final_norm:
        @pl.when(j == pl.num_programs(1) - 1)
        def _():
            y = o_ref[...]
            ms = jnp.mean(y * y, axis=-1, keepdims=True)
            o_ref[...] = y * lax.rsqrt(ms + EPS) * gf_ref[...]


def _ffn(x2, g, w_gu, w_down, g_final, layer, tm, tf, final_norm):
    t, d = x2.shape
    nf = D_FF // tf
    return pl.pallas_call(
        functools.partial(_ffn_kernel, final_norm=final_norm),
        grid=(t // tm, nf),
        in_specs=[
            pl.BlockSpec((tm, d), lambda i, j: (i, 0)),
            pl.BlockSpec((1, d), lambda i, j: (0, 0)),
            pl.BlockSpec((None, d, tf), lambda i, j: (layer, 0, j)),
            pl.BlockSpec((None, d, tf), lambda i, j: (layer, 0, nf + j)),
            pl.BlockSpec((None, tf, d), lambda i, j: (layer, j, 0)),
            pl.BlockSpec((1, d), lambda i, j: (0, 0)),
        ],
        out_specs=pl.BlockSpec((tm, d), lambda i, j: (i, 0)),
        out_shape=jax.ShapeDtypeStruct((t, d), F32),
        scratch_shapes=[pltpu.VMEM((tm, d), BF16)],
        compiler_params=pltpu.CompilerParams(
            dimension_semantics=("parallel", "arbitrary"), vmem_limit_bytes=VMEM_LIMIT),
        name="ffn",
    )(x2, g, w_gu, w_gu, w_down, g_final)


def _tiles(tokens):
    tm = 1024
    while tokens % tm:
        tm //= 2
    return tm


def kernel(x, norm_mix, w_in, gla_a2, gla_ab, gla_norm, ml_conv, ml_ib, ml_fb, ml_norm,
           sgu_ln_g, sgu_ln_b, sgu_w, sgu_b, w_out, norm_ffn, w_gu, w_down, norm_final):
    batch, seq, d = x.shape
    depth = w_in.shape[0]
    tokens = batch * seq
    tm = _tiles(tokens)

    lvl = jnp.asarray(_pair_level_matrix())
    ltri = jnp.asarray(np.tril(np.ones((CHUNK, CHUNK), np.float32)), BF16)
    shift = jnp.asarray(_conv_shift_matrix(), BF16)

    w_in_b = _relayout_w_in(w_in)
    w_out_b = w_out.astype(BF16)
    w_gu_b = w_gu.astype(BF16)
    w_down_b = w_down.astype(BF16)

    xc = x.reshape(tokens, d)
    for l in range(depth):
        proj, gates = _in_proj(xc, norm_mix[l][None, :], w_in_b, l, tm, 1536)

        a2p = jnp.pad(_pad_heads(gla_a2[l]), ((0, LANE - GATE_RANK), (0, 0))).astype(BF16)
        abp = _pad_heads(gla_ab[l])[None, :]
        mix_a = _gla(proj, gates, a2p, abp, gla_norm[l][None, :], ltri, lvl, batch, seq)

        cw = jnp.concatenate([_pad_heads(ml_conv[l][:, :HEADS * DK]),
                              _pad_heads(ml_conv[l][:, HEADS * DK:])], axis=1)
        ifb = jnp.pad(jnp.concatenate([ml_ib[l], ml_fb[l]]), (0, LANE - 2 * HEADS))[None, :]
        mix_b = _mlstm(proj, gates, cw, ifb, ml_norm[l][None, :], ltri, lvl, shift, batch, seq)

        b_full = jnp.repeat(sgu_b[l].T, SGU_CH, axis=1)
        mix_c = _sgu(proj, sgu_ln_g[l][None, :], sgu_ln_b[l][None, :], sgu_w[l], b_full, lvl, batch, seq)

        x1 = _out_proj(xc, mix_a, mix_b, mix_c, w_out_b, l, tm // 2, d)
        xc = _ffn(x1, norm_ffn[l][None, :], w_gu_b, w_down_b, norm_final[None, :], l, tm, 512,
                  final_norm=(l == depth - 1))
    return xc.reshape(batch, seq, d)
```

```python
import functools

import numpy as np
import jax
import jax.numpy as jnp
from jax import lax
from jax.experimental import pallas as pl
from jax.experimental.pallas import tpu as pltpu

F32 = jnp.float32
BF16 = jnp.bfloat16

D_MODEL = 2048
HEADS = 4
DK = 96
DKP = 128
DV = 192
QK_W = HEADS * DKP
V_W = HEADS * DV
PAIR_K = 2 * DKP
PAIR_V = 2 * DV
SGU_W = 512
SGU_GROUPS = 4
SGU_CH = 128
GATE_RANK = 16
GATE_TAU = 16.0
CONV_WIDTH = 4
D_FF = 5632
EPS = 1e-6
CHUNK = 128
N_LEVELS = 7
LANE = 128

OFF_GLA_Q, OFF_GLA_K, OFF_ML_Q, OFF_ML_K = 0, 512, 1024, 1536
OFF_SGU_U, OFF_SGU_V = 2048, 2560
OFF_GLA_V, OFF_GLA_G, OFF_ML_V, OFF_ML_O = 3072, 3840, 4608, 5376
MAIN_W = 6144
OFF_GLA_A1, OFF_ML_IF = 6144, 6272
GATE_W = 2 * LANE
PROJ_W = MAIN_W + GATE_W
SGU_STEP = 4

VMEM_LIMIT = 56 * 1024 * 1024


def _relayout_w_in(w_in):
    gk = HEADS * DK
    sizes = (gk, gk, V_W, V_W, GATE_RANK, gk, gk, V_W, V_W, HEADS, HEADS, SGU_W, SGU_W)
    starts = np.concatenate([[0], np.cumsum(sizes)]).tolist()
    w = w_in.astype(BF16)
    (gq, gkk, gv, gg, ga1, mq, mk, mv, mo, mi, mf, su, sv) = [
        w[..., starts[n]:starts[n + 1]] for n in range(len(sizes))]
    pad_to = lambda a, n: jnp.pad(a, [(0, 0)] * (a.ndim - 1) + [(0, n - a.shape[-1])])
    cols = [_pad_heads(gq), _pad_heads(gkk), _pad_heads(mq), _pad_heads(mk), su, sv, gv, gg, mv, mo,
            pad_to(ga1, LANE), pad_to(jnp.concatenate([mi, mf], axis=-1), LANE)]
    out = jnp.concatenate(cols, axis=-1)
    assert out.shape[-1] == PROJ_W
    return out


def _pad_heads(a):
    lead = a.shape[:-1]
    a = a.reshape(lead + (HEADS, DK))
    a = jnp.pad(a, [(0, 0)] * len(lead) + [(0, 0), (0, DKP - DK)])
    return a.reshape(lead + (QK_W,))


def _pair_level_matrix():
    c = CHUNK
    t = np.arange(c)[:, None]
    s = np.arange(c)[None, :]
    x = np.bitwise_xor(t, s)
    lvl = np.floor(np.log2(np.maximum(x, 1))).astype(np.int32)
    lvl = np.where(s == t, -1, lvl)
    lvl = np.where(s > t, -2, lvl)
    return lvl.astype(np.int32)


def _dot(a, b):
    return jnp.dot(a, b, preferred_element_type=F32)


def _dot_nt(a, b):
    return lax.dot_general(a, b, (((1,), (1,)), ((), ())), preferred_element_type=F32)


def _split3(x):
    hi = x.astype(BF16)
    r = x - hi.astype(F32)
    mid = r.astype(BF16)
    lo = (r - mid.astype(F32)).astype(BF16)
    return hi, mid, lo


LOG2E = 1.4426950408889634
LN2 = 0.6931471805599453


def _log2_sigmoid(x):
    xl = x * LOG2E
    return jnp.minimum(xl, 0.0) - jnp.log2(1.0 + jnp.exp2(-jnp.abs(xl)))


def _log_sigmoid(x):
    return _log2_sigmoid(x) * LN2


def _pair_cols(c0, c1, rows):
    first = lax.broadcasted_iota(jnp.int32, (rows, LANE), 1) < DV - LANE
    b0 = jnp.broadcast_to(c0, (rows, LANE))
    b1 = jnp.broadcast_to(c1, (rows, LANE))
    return jnp.concatenate([b0, jnp.where(first, b0, b1), b1], axis=1)


def _head_rms_scale(o):
    rows = o.shape[0]
    o2 = o * o
    first = lax.broadcasted_iota(jnp.int32, (rows, LANE), 1) < DV - LANE
    mid = o2[:, LANE:2 * LANE]
    ss0 = jnp.sum(o2[:, 0:LANE] + jnp.where(first, mid, 0.0), axis=-1, keepdims=True)
    ss1 = jnp.sum(o2[:, 2 * LANE:3 * LANE] + jnp.where(first, 0.0, mid), axis=-1, keepdims=True)
    return _pair_cols(lax.rsqrt(ss0 / DV + EPS), lax.rsqrt(ss1 / DV + EPS), rows)


def _in_proj_kernel(x_ref, g_ref, w_ref, wg_ref, o_ref, og_ref, h_ref):
    @pl.when(pl.program_id(1) == 0)
    def _():
        x = x_ref[...]
        ms = jnp.mean(x * x, axis=-1, keepdims=True)
        h_ref[...] = (x * lax.rsqrt(ms + EPS) * g_ref[...]).astype(BF16)
        og_ref[...] = _dot(h_ref[...], wg_ref[...])

    o_ref[...] = _dot(h_ref[...], w_ref[...]).astype(o_ref.dtype)


def _in_proj(x2, g, w, layer, tm, tn):
    t, d = x2.shape
    return pl.pallas_call(
        _in_proj_kernel,
        grid=(t // tm, MAIN_W // tn),
        in_specs=[
            pl.BlockSpec((tm, d), lambda i, j: (i, 0)),
            pl.BlockSpec((1, d), lambda i, j: (0, 0)),
            pl.BlockSpec((None, d, tn), lambda i, j: (layer, 0, j)),
            pl.BlockSpec((None, d, GATE_W), lambda i, j: (layer, 0, MAIN_W // GATE_W)),
        ],
        out_specs=[pl.BlockSpec((tm, tn), lambda i, j: (i, j)),
                   pl.BlockSpec((tm, GATE_W), lambda i, j: (i, 0))],
        out_shape=[jax.ShapeDtypeStruct((t, MAIN_W), BF16),
                   jax.ShapeDtypeStruct((t, GATE_W), F32)],
        scratch_shapes=[pltpu.VMEM((tm, d), BF16)],
        compiler_params=pltpu.CompilerParams(
            dimension_semantics=("parallel", "arbitrary"), vmem_limit_bytes=VMEM_LIMIT),
        name="in_proj",
    )(x2, g, w, w)


def _gla_kernel(qk_ref, vg_ref, a1_ref, a2_ref, ab_ref, nrm_ref, ltri_ref, lvl_ref,
                o_ref, s_ref, cum_ref, la_ref):
    s_ref[...] = jnp.zeros_like(s_ref)
    la_ref[0:8, :] = jnp.zeros((8, QK_W), F32)
    la_ref[8 + CHUNK:16 + CHUNK, :] = jnp.zeros((8, QK_W), F32)

    def chunk(c, carry):
        rows = pl.ds(pl.multiple_of(c * CHUNK, CHUNK), CHUNK)
        _gla_chunk(rows, qk_ref, vg_ref, a1_ref, a2_ref, ab_ref, nrm_ref, ltri_ref, lvl_ref,
                   o_ref, s_ref, cum_ref, la_ref)
        return carry

    lax.fori_loop(0, qk_ref.shape[0] // CHUNK, chunk, 0, unroll=2)


def _gla_chunk(rows, qk_ref, vg_ref, a1_ref, a2_ref, ab_ref, nrm_ref, ltri_ref, lvl_ref,
               o_ref, s_ref, cum_ref, la_ref):
    q = qk_ref[rows, 0:QK_W].astype(F32) * (DK ** -0.5)
    k = qk_ref[rows, QK_W:2 * QK_W].astype(F32)
    z = _dot(a1_ref[rows, :].astype(BF16), a2_ref[...]) + ab_ref[...]
    log_a = _log2_sigmoid(z) * (1.0 / GATE_TAU)
    la_ref[8:8 + CHUNK, :] = log_a
    ltri = ltri_ref[...]
    hi, mid, lo = _split3(log_a)
    cum = _dot(ltri, hi) + _dot(ltri, mid) + _dot(ltri, lo)
    cum_ref[...] = cum

    lvl = lvl_ref[...]
    groups = CHUNK // 8
    rows8 = lambda g: slice(8 * g, 8 * g + 8)
    heads = [slice(h * DKP, (h + 1) * DKP) for h in range(HEADS)]
    row = lax.broadcasted_iota(jnp.int32, (CHUNK, QK_W), 0)

    def scores(xq, yk):
        return [_dot_nt(xq[:, hs], yk[:, hs]) for hs in heads]

    qb = q.astype(BF16)
    kb = k.astype(BF16)
    attn = [[jnp.where(lvl[rows8(g), :] == -1, p[rows8(g), :], 0.0) for g in range(groups)]
            for p in scores(qb, kb)]

    def merge(level, parts, q_groups):
        for h in range(HEADS):
            for i, g in enumerate(q_groups):
                attn[h][g] = jnp.where(lvl[rows8(g), :] == level, parts[h][rows8(i), :], attn[h][g])

    e = jnp.exp2(jnp.where((row & 1) == 1, log_a, 0.0))
    merge(0, scores((q * e).astype(BF16), kb), range(groups))
    nxt = la_ref[pl.ds(9, CHUNK), :]
    prv = la_ref[pl.ds(7, CHUNK), :]
    r4 = row & 3
    e = jnp.exp2(jnp.where(r4 == 0, nxt, jnp.where(r4 == 1, 0.0, jnp.where(r4 == 2, log_a, log_a + prv))))
    merge(1, scores((q * e).astype(BF16), (k * e).astype(BF16)), range(groups))
    sub8 = lax.broadcasted_iota(jnp.int32, (8, QK_W), 0)
    pieces = []
    for g in range(groups):
        d = cum[rows8(g), :] - cum_ref[8 * g + 3:8 * g + 4, :]
        pieces.append(jnp.where(sub8 < 4, -d, d))
    e = jnp.exp2(jnp.concatenate(pieces, axis=0))
    merge(2, scores((q * e).astype(BF16), (k * e).astype(BF16)), range(groups))
    for level in range(3, N_LEVELS):
        m = 1 << level
        xq, yk, q_groups = [], [], []
        for base in range(0, CHUNK, 2 * m):
            k_rows = slice(base, base + m)
            q_rows = slice(base + m, base + 2 * m)
            edge = cum_ref[base + m - 1:base + m, :]
            yk += [k[k_rows, :] * jnp.exp2(edge - cum[k_rows, :]), k[q_rows, :]]
            xq.append(q[q_rows, :] * jnp.exp2(cum[q_rows, :] - edge))
            q_groups += range((base + m) // 8, (base + 2 * m) // 8)
        merge(level, scores(jnp.concatenate(xq, axis=0).astype(BF16),
                            jnp.concatenate(yk, axis=0).astype(BF16)), q_groups)

    last = cum_ref[CHUNK - 1:CHUNK, :]
    q_dec = (q * jnp.exp2(cum)).astype(BF16)
    k_dec = k * jnp.exp2(last - cum)
    decay_all = jnp.exp2(last)

    lane = lax.broadcasted_iota(jnp.int32, (CHUNK, PAIR_V), 1)
    lo_mask = lane < DV

    for p in range(HEADS // 2):
        ks = slice(p * PAIR_K, (p + 1) * PAIR_K)
        vs = slice(p * PAIR_V, (p + 1) * PAIR_V)
        vb = vg_ref[rows, vs]
        zero = jnp.zeros_like(vb)
        v_lo = jnp.where(lo_mask, vb, zero)
        v_hi = jnp.where(lo_mask, zero, vb)
        v_blk = jnp.concatenate([v_lo, v_hi], axis=0)
        state = s_ref[p]
        lhs = jnp.concatenate([jnp.concatenate(attn[2 * p], axis=0).astype(BF16),
                               jnp.concatenate(attn[2 * p + 1], axis=0).astype(BF16),
                               q_dec[:, ks]], axis=1)
        rhs = jnp.concatenate([v_blk, state.astype(BF16)], axis=0)
        o = _dot(lhs, rhs)

        kd_t = k_dec[:, ks].T.astype(BF16)
        upd = jnp.concatenate([_dot(kd_t[0:DKP, :], v_lo), _dot(kd_t[DKP:PAIR_K, :], v_hi)], axis=0)
        dcols = []
        for h in (2 * p, 2 * p + 1):
            d_row = jnp.broadcast_to(decay_all[:, h * DKP:(h + 1) * DKP], (DKP, DKP))
            d_col = d_row.T
            dcols.append(jnp.concatenate([d_col] * (PAIR_V // LANE), axis=1))
        s_ref[p] = state * jnp.concatenate(dcols, axis=0) + upd

        gate = vg_ref[rows, V_W + p * PAIR_V:V_W + (p + 1) * PAIR_V].astype(F32)
        out = o * _head_rms_scale(o) * nrm_ref[:, vs] * (gate * jax.nn.sigmoid(gate))
        o_ref[rows, vs] = out.astype(o_ref.dtype)


def _gla(proj, gates, a2p, abp, nrm, ltri, lvl, batch, seq):
    tok = lambda w, cb: pl.BlockSpec((seq, w), lambda b: (b, cb))
    const = lambda shape: pl.BlockSpec(shape, lambda b: (0,) * len(shape))
    assert OFF_GLA_K == OFF_GLA_Q + QK_W and OFF_GLA_G == OFF_GLA_V + V_W
    return pl.pallas_call(
        _gla_kernel,
        grid=(batch,),
        in_specs=[
            tok(2 * QK_W, OFF_GLA_Q // (2 * QK_W)),
            tok(2 * V_W, OFF_GLA_V // (2 * V_W)),
            tok(LANE, (OFF_GLA_A1 - MAIN_W) // LANE),
            const((LANE, QK_W)),
            const((1, QK_W)),
            const((1, V_W)),
            const((CHUNK, CHUNK)),
            const((CHUNK, CHUNK)),
        ],
        out_specs=pl.BlockSpec((seq, V_W), lambda b: (b, 0)),
        out_shape=jax.ShapeDtypeStruct((batch * seq, V_W), BF16),
        scratch_shapes=[
            pltpu.VMEM((HEADS // 2, PAIR_K, PAIR_V), F32),
            pltpu.VMEM((CHUNK, QK_W), F32),
            pltpu.VMEM((CHUNK + 16, QK_W), F32),
        ],
        compiler_params=pltpu.CompilerParams(
            dimension_semantics=("parallel",), vmem_limit_bytes=VMEM_LIMIT),
        name="gla",
    )(proj, proj, gates, a2p, abp, nrm, ltri, lvl)


ST_W = PAIR_V + LANE
TAIL = 8


def _mlstm_kernel(qk_ref, vo_ref, if_ref, cw_ref, ifb_ref, nrm_ref, ltri_ref, lvl_ref,
                  o_ref, c_ref, m_ref, xe_ref):
    c_ref[...] = jnp.zeros_like(c_ref)
    m_ref[...] = jnp.zeros_like(m_ref)
    xe_ref[0:TAIL, :] = jnp.zeros((TAIL, 2 * QK_W), F32)

    def chunk(c, carry):
        rows = pl.ds(pl.multiple_of(c * CHUNK, CHUNK), CHUNK)
        _mlstm_chunk(rows, qk_ref, vo_ref, if_ref, cw_ref, ifb_ref, nrm_ref, ltri_ref, lvl_ref,
                     o_ref, c_ref, m_ref, xe_ref)
        return carry

    lax.fori_loop(0, qk_ref.shape[0] // CHUNK, chunk, 0, unroll=2)


def _mlstm_chunk(rows, qk_ref, vo_ref, if_ref, cw_ref, ifb_ref, nrm_ref, ltri_ref, lvl_ref,
                 o_ref, c_ref, m_ref, xe_ref):
    xe_ref[TAIL:TAIL + CHUNK, :] = qk_ref[rows, :].astype(F32)
    y = jnp.zeros((CHUNK, 2 * QK_W), F32)
    for j in range(CONV_WIDTH):
        y = y + cw_ref[j:j + 1, :] * xe_ref[pl.ds(TAIL - (CONV_WIDTH - 1) + j, CHUNK), :]
    xe_ref[0:TAIL, :] = xe_ref[CHUNK:CHUNK + TAIL, :]
    y = y * jax.nn.sigmoid(y)
    qm = y[:, 0:QK_W]
    km = y[:, QK_W:2 * QK_W] * (DK ** -0.5)
    qb = qm.astype(BF16)
    kb = km.astype(BF16)
    km_t = km.T

    slab = if_ref[rows, :] + ifb_ref[...]
    lane_g = lax.broadcasted_iota(jnp.int32, (CHUNK, LANE), 1)
    gates = jnp.where(lane_g < HEADS, slab, _log_sigmoid(slab))
    ltri = ltri_ref[...]
    g_hi, g_mid, g_lo = _split3(gates)
    cum_col = _dot(ltri, g_hi) + _dot(ltri, g_mid) + _dot(ltri, g_lo)
    gates_t = gates.T[0:2 * HEADS, :]
    t_hi, t_mid, t_lo = _split3(gates_t)
    cum_row = _dot_nt(t_hi, ltri) + _dot_nt(t_mid, ltri) + _dot_nt(t_lo, ltri)

    lvl = lvl_ref[...]
    causal = lvl >= -1
    lane = lax.broadcasted_iota(jnp.int32, (CHUNK, PAIR_V), 1)
    lo_mask = lane < DV
    lane_e = lax.broadcasted_iota(jnp.int32, (CHUNK, LANE), 1)
    one_col = [jnp.where(lane_e == i, 1.0, 0.0).astype(BF16) for i in range(2)]
    lane_r = lax.broadcasted_iota(jnp.int32, (PAIR_K, LANE), 1)
    row_r = lax.broadcasted_iota(jnp.int32, (PAIR_K, LANE), 0)
    ones_blk = jnp.where(lane_r == jnp.where(row_r < CHUNK, 0, 1), 1.0, 0.0).astype(BF16)

    for p in range(HEADS // 2):
        ks = slice(p * PAIR_K, (p + 1) * PAIR_K)
        vs = slice(p * PAIR_V, (p + 1) * PAIR_V)
        vb = vo_ref[rows, vs]
        zero = jnp.zeros_like(vb)
        sc_parts, qs_parts, floor_parts, kw_parts, dprev_parts = [], [], [], [], []
        for h in (2 * p, 2 * p + 1):
            hs = slice(h * DKP, (h + 1) * DKP)
            cc = jnp.broadcast_to(cum_col[:, HEADS + h:HEADS + h + 1], (CHUNK, CHUNK))
            cum_r = cum_row[HEADS + h:HEADS + h + 1, :]
            ib_r = gates_t[h:h + 1, :]
            m_prev = m_ref[h:h + 1, :]
            dmat = jnp.where(causal, cc - cum_r + ib_r, -jnp.inf)
            inter = cc + m_prev
            m_t = jnp.maximum(inter, jnp.max(dmat, axis=-1, keepdims=True))
            w = jnp.exp(dmat - m_t)
            sc_inter = jnp.exp(inter - m_t)
            sc_parts.append((_dot_nt(qb[:, hs], kb[:, hs]) * w).astype(BF16))
            qs_parts.append((qm[:, hs] * sc_inter).astype(BF16))
            floor_parts.append(jnp.exp(-m_t))
            total = cum_r[:, CHUNK - 1:CHUNK]
            g_row = total - cum_r + ib_r
            m_new = jnp.maximum(total + m_prev, jnp.max(g_row, axis=-1, keepdims=True))
            wj = jnp.exp(g_row - m_new)
            dprev_parts.append(jnp.exp(total + m_prev - m_new))
            kw_parts.append((km_t[hs, :] * wj).astype(BF16))
            m_ref[h:h + 1, :] = m_new

        state = c_ref[p]
        st_m = state.astype(BF16)
        v_lo = jnp.where(lo_mask, vb, zero)
        v_hi = jnp.where(lo_mask, zero, vb)
        v_blk = jnp.concatenate([v_lo, v_hi], axis=0)
        rhs = jnp.concatenate([jnp.concatenate([v_blk, ones_blk], axis=1), st_m], axis=0)
        lhs = jnp.concatenate(sc_parts + qs_parts, axis=1)
        res = _dot(lhs, rhs)
        num = res[:, 0:PAIR_V]
        den = _pair_cols(res[:, PAIR_V:PAIR_V + 1], res[:, PAIR_V + 1:PAIR_V + 2], CHUNK)
        floor = _pair_cols(floor_parts[0][:, 0:1], floor_parts[1][:, 0:1], CHUNK)
        hid = num / jnp.maximum(jnp.abs(den), floor)

        upd = jnp.concatenate(
            [_dot(kw_parts[0], jnp.concatenate([v_lo, one_col[0]], axis=1)),
             _dot(kw_parts[1], jnp.concatenate([v_hi, one_col[1]], axis=1))], axis=0)
        d_rows = jnp.concatenate(
            [jnp.broadcast_to(jnp.concatenate([d] * (ST_W // LANE), axis=1), (DKP, ST_W))
             for d in dprev_parts], axis=0)
        c_ref[p] = d_rows * state + upd

        og = vo_ref[rows, V_W + p * PAIR_V:V_W + (p + 1) * PAIR_V].astype(F32)
        out = hid * _head_rms_scale(hid) * nrm_ref[:, vs] * jax.nn.sigmoid(og)
        o_ref[rows, vs] = out.astype(o_ref.dtype)


def _mlstm(proj, gates, cw, ifb, nrm, ltri, lvl, batch, seq):
    tok = lambda w, cb: pl.BlockSpec((seq, w), lambda b: (b, cb))
    const = lambda shape: pl.BlockSpec(shape, lambda b: (0,) * len(shape))
    assert OFF_ML_K == OFF_ML_Q + QK_W and OFF_ML_O == OFF_ML_V + V_W
    return pl.pallas_call(
        _mlstm_kernel,
        grid=(batch,),
        in_specs=[
            tok(2 * QK_W, OFF_ML_Q // (2 * QK_W)),
            tok(2 * V_W, OFF_ML_V // (2 * V_W)),
            tok(LANE, (OFF_ML_IF - MAIN_W) // LANE),
            const((CONV_WIDTH, 2 * QK_W)),
            const((1, LANE)),
            const((1, V_W)),
            const((CHUNK, CHUNK)),
            const((CHUNK, CHUNK)),
        ],
        out_specs=pl.BlockSpec((seq, V_W), lambda b: (b, 0)),
        out_shape=jax.ShapeDtypeStruct((batch * seq, V_W), BF16),
        scratch_shapes=[
            pltpu.VMEM((HEADS // 2, PAIR_K, ST_W), F32),
            pltpu.VMEM((2 * HEADS, LANE), F32),
            pltpu.VMEM((TAIL + CHUNK, 2 * QK_W), F32),
        ],
        compiler_params=pltpu.CompilerParams(
            dimension_semantics=("parallel",), vmem_limit_bytes=VMEM_LIMIT),
        name="mlstm",
    )(proj, proj, gates, cw, ifb, nrm, ltri, lvl)


def _sgu_kernel(u_ref, v_ref, lng_ref, lnb_ref, w_ref, b_ref, lvl_ref, o_ref):
    causal = lvl_ref[...] >= -1
    w_causal = [jnp.where(causal, w_ref[g], 0.0).astype(BF16) for g in range(SGU_GROUPS)]
    for n in range(u_ref.shape[0] // CHUNK):
        rows = slice(n * CHUNK, (n + 1) * CHUNK)
        u = jax.nn.gelu(u_ref[rows, :].astype(F32))
        v = jax.nn.gelu(v_ref[rows, :].astype(F32))
        mu = jnp.mean(v, axis=-1, keepdims=True)
        var = jnp.mean(jnp.square(v - mu), axis=-1, keepdims=True)
        vn = ((v - mu) * lax.rsqrt(var + EPS) * lng_ref[...] + lnb_ref[...]).astype(BF16)
        for g in range(SGU_GROUPS):
            gs = slice(g * SGU_CH, (g + 1) * SGU_CH)
            mixed = _dot(w_causal[g], vn[:, gs]) + b_ref[:, gs]
            o_ref[rows, gs] = (u[:, gs] * mixed).astype(o_ref.dtype)


def _sgu(proj, lng, lnb, w, b_full, lvl, batch, seq):
    step = CHUNK * SGU_STEP
    while (batch * seq) % step:
        step //= 2
    nb = batch * seq // step
    tok = lambda w_, cb: pl.BlockSpec((step, w_), lambda i: (i, cb))
    const = lambda shape: pl.BlockSpec(shape, lambda i: (0,) * len(shape))
    return pl.pallas_call(
        _sgu_kernel,
        grid=(nb,),
        in_specs=[
            tok(SGU_W, OFF_SGU_U // SGU_W),
            tok(SGU_W, OFF_SGU_V // SGU_W),
            const((1, SGU_W)),
            const((1, SGU_W)),
            const((SGU_GROUPS, CHUNK, CHUNK)),
            const((CHUNK, SGU_W)),
            const((CHUNK, CHUNK)),
        ],
        out_specs=pl.BlockSpec((step, SGU_W), lambda i: (i, 0)),
        out_shape=jax.ShapeDtypeStruct((batch * seq, SGU_W), BF16),
        compiler_params=pltpu.CompilerParams(
            dimension_semantics=("parallel",), vmem_limit_bytes=VMEM_LIMIT),
        name="sgu",
    )(proj, proj, lng, lnb, w, b_full, lvl)


def _out_proj_kernel(x_ref, a_ref, b_ref, c_ref, wa_ref, wb_ref, wc_ref, o_ref):
    o_ref[...] = (x_ref[...] + _dot(a_ref[...], wa_ref[...]) + _dot(b_ref[...], wb_ref[...])
                  + _dot(c_ref[...], wc_ref[...]))


def _out_proj(x2, mix_a, mix_b, mix_c, w_out, layer, tm, tn):
    t, d = x2.shape
    return pl.pallas_call(
        _out_proj_kernel,
        grid=(d // tn, t // tm),
        in_specs=[
            pl.BlockSpec((tm, tn), lambda j, i: (i, j)),
            pl.BlockSpec((tm, V_W), lambda j, i: (i, 0)),
            pl.BlockSpec((tm, V_W), lambda j, i: (i, 0)),
            pl.BlockSpec((tm, SGU_W), lambda j, i: (i, 0)),
            pl.BlockSpec((None, V_W, tn), lambda j, i: (layer, 0, j)),
            pl.BlockSpec((None, V_W, tn), lambda j, i: (layer, 1, j)),
            pl.BlockSpec((None, SGU_W, tn), lambda j, i: (layer, 2 * V_W // SGU_W, j)),
        ],
        out_specs=pl.BlockSpec((tm, tn), lambda j, i: (i, j)),
        out_shape=jax.ShapeDtypeStruct((t, d), F32),
        compiler_params=pltpu.CompilerParams(
            dimension_semantics=("parallel", "arbitrary"), vmem_limit_bytes=VMEM_LIMIT),
        name="out_proj",
    )(x2, mix_a, mix_b, mix_c, w_out, w_out, w_out)


def _ffn_kernel(x_ref, g_ref, wg_ref, wu_ref, wd_ref, gf_ref, o_ref, h_ref, *, final_norm):
    j = pl.program_id(1)

    @pl.when(j == 0)
    def _():
        x = x_ref[...]
        ms = jnp.mean(x * x, axis=-1, keepdims=True)
        h_ref[...] = (x * lax.rsqrt(ms + EPS) * g_ref[...]).astype(BF16)
        o_ref[...] = x

    h = h_ref[...]
    gate = _dot(h, wg_ref[...])
    up = _dot(h, wu_ref[...])
    act = (gate * jax.nn.sigmoid(gate) * up).astype(BF16)
    o_ref[...] += _dot(act, wd_ref[...])

    if final_norm:
        @pl.when(j == pl.num_programs(1) - 1)
        def _():
            y = o_ref[...]
            ms = jnp.mean(y * y, axis=-1, keepdims=True)
            o_ref[...] = y * lax.rsqrt(ms + EPS) * gf_ref[...]


def _ffn(x2, g, w_gu, w_down, g_final, layer, tm, tf, final_norm):
    t, d = x2.shape
    nf = D_FF // tf
    return pl.pallas_call(
        functools.partial(_ffn_kernel, final_norm=final_norm),
        grid=(t // tm, nf),
        in_specs=[
            pl.BlockSpec((tm, d), lambda i, j: (i, 0)),
            pl.BlockSpec((1, d), lambda i, j: (0, 0)),
            pl.BlockSpec((None, d, tf), lambda i, j: (layer, 0, j)),
            pl.BlockSpec((None, d, tf), lambda i, j: (layer, 0, nf + j)),
            pl.BlockSpec((None, tf, d), lambda i, j: (layer, j, 0)),
            pl.BlockSpec((1, d), lambda i, j: (0, 0)),
        ],
        out_specs=pl.BlockSpec((tm, d), lambda i, j: (i, 0)),
        out_shape=jax.ShapeDtypeStruct((t, d), F32),
        scratch_shapes=[pltpu.VMEM((tm, d), BF16)],
        compiler_params=pltpu.CompilerParams(
            dimension_semantics=("parallel", "arbitrary"), vmem_limit_bytes=VMEM_LIMIT),
        name="ffn",
    )(x2, g, w_gu, w_gu, w_down, g_final)


def _tiles(tokens):
    tm = 1024
    while tokens % tm:
        tm //= 2
    return tm


def kernel(x, norm_mix, w_in, gla_a2, gla_ab, gla_norm, ml_conv, ml_ib, ml_fb, ml_norm,
           sgu_ln_g, sgu_ln_b, sgu_w, sgu_b, w_out, norm_ffn, w_gu, w_down, norm_final):
    batch, seq, d = x.shape
    depth = w_in.shape[0]
    tokens = batch * seq
    tm = _tiles(tokens)

    lvl = jnp.asarray(_pair_level_matrix())
    ltri = jnp.asarray(np.tril(np.ones((CHUNK, CHUNK), np.float32)), BF16)

    w_in_b = _relayout_w_in(w_in)
    w_out_b = w_out.astype(BF16)
    w_gu_b = w_gu.astype(BF16)
    w_down_b = w_down.astype(BF16)

    xc = x.reshape(tokens, d)
    for l in range(depth):
        proj, gates = _in_proj(xc, norm_mix[l][None, :], w_in_b, l, tm, 1536)

        a2p = jnp.pad(_pad_heads(gla_a2[l]), ((0, LANE - GATE_RANK), (0, 0))).astype(BF16)
        abp = _pad_heads(gla_ab[l])[None, :]
        mix_a = _gla(proj, gates, a2p, abp, gla_norm[l][None, :], ltri, lvl, batch, seq)

        cw = jnp.concatenate([_pad_heads(ml_conv[l][:, :HEADS * DK]),
                              _pad_heads(ml_conv[l][:, HEADS * DK:])], axis=1)
        ifb = jnp.pad(jnp.concatenate([ml_ib[l], ml_fb[l]]), (0, LANE - 2 * HEADS))[None, :]
        mix_b = _mlstm(proj, gates, cw, ifb, ml_norm[l][None, :], ltri, lvl, batch, seq)

        b_full = jnp.repeat(sgu_b[l].T, SGU_CH, axis=1)
        mix_c = _sgu(proj, sgu_ln_g[l][None, :], sgu_ln_b[l][None, :], sgu_w[l], b_full, lvl, batch, seq)

        x1 = _out_proj(xc, mix_a, mix_b, mix_c, w_out_b, l, tm // 2, d)
        xc = _ffn(x1, norm_ffn[l][None, :], w_gu_b, w_down_b, norm_final[None, :], l, tm, 512,
                  final_norm=(l == depth - 1))
    return xc.reshape(batch, seq, d)
```

```python
import functools

import numpy as np
import jax
import jax.numpy as jnp
from jax import lax
from jax.experimental import pallas as pl
from jax.experimental.pallas import tpu as pltpu

F32 = jnp.float32
BF16 = jnp.bfloat16

D_MODEL = 2048
HEADS = 4
DK = 96
DKP = 128
DV = 192
QK_W = HEADS * DKP
V_W = HEADS * DV
PAIR_K = 2 * DKP
PAIR_V = 2 * DV
SGU_W = 512
SGU_GROUPS = 4
SGU_CH = 128
GATE_RANK = 16
GATE_TAU = 16.0
CONV_WIDTH = 4
D_FF = 5632
EPS = 1e-6
CHUNK = 128
N_LEVELS = 7
LANE = 128

OFF_GLA_Q, OFF_GLA_K, OFF_ML_Q, OFF_ML_K = 0, 512, 1024, 1536
OFF_SGU_U, OFF_SGU_V = 2048, 2560
OFF_GLA_V, OFF_GLA_G, OFF_ML_V, OFF_ML_O = 3072, 3840, 4608, 5376
MAIN_W = 6144
OFF_GLA_A1, OFF_ML_IF = 6144, 6272
GATE_W = 2 * LANE
PROJ_W = MAIN_W + GATE_W
SGU_STEP = 4

VMEM_LIMIT = 58 * 1024 * 1024


def _relayout_w_in(w_in):
    gk = HEADS * DK
    sizes = (gk, gk, V_W, V_W, GATE_RANK, gk, gk, V_W, V_W, HEADS, HEADS, SGU_W, SGU_W)
    starts = np.concatenate([[0], np.cumsum(sizes)]).tolist()
    w = w_in.astype(BF16)
    (gq, gkk, gv, gg, ga1, mq, mk, mv, mo, mi, mf, su, sv) = [
        w[..., starts[n]:starts[n + 1]] for n in range(len(sizes))]
    pad_to = lambda a, n: jnp.pad(a, [(0, 0)] * (a.ndim - 1) + [(0, n - a.shape[-1])])
    cols = [_pad_heads(gq), _pad_heads(gkk), _pad_heads(mq), _pad_heads(mk), su, sv, gv, gg, mv, mo,
            pad_to(ga1, LANE), pad_to(jnp.concatenate([mi, mf], axis=-1), LANE)]
    out = jnp.concatenate(cols, axis=-1)
    assert out.shape[-1] == PROJ_W
    return out


def _pad_heads(a):
    lead = a.shape[:-1]
    a = a.reshape(lead + (HEADS, DK))
    a = jnp.pad(a, [(0, 0)] * len(lead) + [(0, 0), (0, DKP - DK)])
    return a.reshape(lead + (QK_W,))


def _pair_level_matrix():
    c = CHUNK
    t = np.arange(c)[:, None]
    s = np.arange(c)[None, :]
    x = np.bitwise_xor(t, s)
    lvl = np.floor(np.log2(np.maximum(x, 1))).astype(np.int32)
    lvl = np.where(s == t, -1, lvl)
    lvl = np.where(s > t, -2, lvl)
    return lvl.astype(np.int32)


def _dot(a, b):
    return jnp.dot(a, b, preferred_element_type=F32)


def _dot_nt(a, b):
    return lax.dot_general(a, b, (((1,), (1,)), ((), ())), preferred_element_type=F32)


def _split3(x):
    hi = x.astype(BF16)
    r = x - hi.astype(F32)
    mid = r.astype(BF16)
    lo = (r - mid.astype(F32)).astype(BF16)
    return hi, mid, lo


LOG2E = 1.4426950408889634
LN2 = 0.6931471805599453


def _log2_sigmoid(x):
    xl = x * LOG2E
    return jnp.minimum(xl, 0.0) - jnp.log2(1.0 + jnp.exp2(-jnp.abs(xl)))


def _log_sigmoid(x):
    return _log2_sigmoid(x) * LN2


def _pair_cols(c0, c1, rows):
    first = lax.broadcasted_iota(jnp.int32, (rows, LANE), 1) < DV - LANE
    b0 = jnp.broadcast_to(c0, (rows, LANE))
    b1 = jnp.broadcast_to(c1, (rows, LANE))
    return jnp.concatenate([b0, jnp.where(first, b0, b1), b1], axis=1)


def _head_rms_scale(o):
    rows = o.shape[0]
    o2 = o * o
    first = lax.broadcasted_iota(jnp.int32, (rows, LANE), 1) < DV - LANE
    mid = o2[:, LANE:2 * LANE]
    ss0 = jnp.sum(o2[:, 0:LANE] + jnp.where(first, mid, 0.0), axis=-1, keepdims=True)
    ss1 = jnp.sum(o2[:, 2 * LANE:3 * LANE] + jnp.where(first, 0.0, mid), axis=-1, keepdims=True)
    return _pair_cols(lax.rsqrt(ss0 / DV + EPS), lax.rsqrt(ss1 / DV + EPS), rows)


def _cast_plan(w, layer, steps):
    rows, cols = w.shape[1], w.shape[2]
    nblk = max(n for n in range(1, steps + 1) if rows % n == 0 and (rows // n) % 16 == 0)
    return w, layer, rows, cols, nblk


def _cast_specs(plans, step_of):
    in_specs, out_specs, out_shapes, args = [], [], [], []
    for w, layer, rows, cols, nblk in plans:
        blk = lambda *g, nblk=nblk: jnp.minimum(step_of(*g), nblk - 1)
        in_specs.append(pl.BlockSpec((None, rows // nblk, cols), lambda *g, blk=blk, layer=layer: (layer, blk(*g), 0)))
        out_specs.append(pl.BlockSpec((rows // nblk, cols), lambda *g, blk=blk: (blk(*g), 0)))
        out_shapes.append(jax.ShapeDtypeStruct((rows, cols), BF16))
        args.append(w)
    return in_specs, out_specs, out_shapes, args


def _run_casts(step, plans, src_refs, dst_refs):
    for (_, _, _, _, nblk), src, dst in zip(plans, src_refs, dst_refs):
        @pl.when(step < nblk)
        def _(src=src, dst=dst):
            dst[...] = src[...].astype(BF16)


def _in_proj_kernel(x_ref, g_ref, w_ref, wg_ref, *rest, plans):
    n = len(plans)
    o_ref, og_ref, h_ref = rest[n], rest[n + 1], rest[2 * n + 2]
    _run_casts(pl.program_id(0) * pl.num_programs(1) + pl.program_id(1), plans,
               rest[0:n], rest[n + 2:2 * n + 2])

    @pl.when(pl.program_id(1) == 0)
    def _():
        x = x_ref[...]
        ms = jnp.mean(x * x, axis=-1, keepdims=True)
        h_ref[...] = (x * lax.rsqrt(ms + EPS) * g_ref[...]).astype(BF16)
        og_ref[...] = _dot(h_ref[...], wg_ref[...])

    o_ref[...] = _dot(h_ref[...], w_ref[...]).astype(o_ref.dtype)


def _in_proj(x2, g, w, layer, tm, tn, casts=()):
    t, d = x2.shape
    ni, nj = t // tm, MAIN_W // tn
    plans = [_cast_plan(cw, cl, ni * nj) for cw, cl in casts]
    c_in, c_out, c_shapes, c_args = _cast_specs(plans, lambda i, j: i * nj + j)
    return pl.pallas_call(
        functools.partial(_in_proj_kernel, plans=plans),
        grid=(ni, nj),
        in_specs=[
            pl.BlockSpec((tm, d), lambda i, j: (i, 0)),
            pl.BlockSpec((1, d), lambda i, j: (0, 0)),
            pl.BlockSpec((None, d, tn), lambda i, j: (layer, 0, j)),
            pl.BlockSpec((None, d, GATE_W), lambda i, j: (layer, 0, MAIN_W // GATE_W)),
        ] + c_in,
        out_specs=[pl.BlockSpec((tm, tn), lambda i, j: (i, j)),
                   pl.BlockSpec((tm, GATE_W), lambda i, j: (i, 0))] + c_out,
        out_shape=[jax.ShapeDtypeStruct((t, MAIN_W), BF16),
                   jax.ShapeDtypeStruct((t, GATE_W), F32)] + c_shapes,
        scratch_shapes=[pltpu.VMEM((tm, d), BF16)],
        compiler_params=pltpu.CompilerParams(
            dimension_semantics=("arbitrary", "arbitrary"), vmem_limit_bytes=VMEM_LIMIT),
        name="in_proj",
    )(x2, g, w, w, *c_args)


def _gla_kernel(qk_ref, vg_ref, a1_ref, a2_ref, ab_ref, nrm_ref, ltri_ref, lvl_ref,
                o_ref, s_ref, cum_ref, la_ref):
    s_ref[...] = jnp.zeros_like(s_ref)
    la_ref[0:8, :] = jnp.zeros((8, QK_W), F32)
    la_ref[8 + CHUNK:16 + CHUNK, :] = jnp.zeros((8, QK_W), F32)

    def chunk(c, carry):
        rows = pl.ds(pl.multiple_of(c * CHUNK, CHUNK), CHUNK)
        _gla_chunk(rows, qk_ref, vg_ref, a1_ref, a2_ref, ab_ref, nrm_ref, ltri_ref, lvl_ref,
                   o_ref, s_ref, cum_ref, la_ref)
        return carry

    lax.fori_loop(0, qk_ref.shape[0] // CHUNK, chunk, 0, unroll=2)


def _gla_chunk(rows, qk_ref, vg_ref, a1_ref, a2_ref, ab_ref, nrm_ref, ltri_ref, lvl_ref,
               o_ref, s_ref, cum_ref, la_ref):
    q = qk_ref[rows, 0:QK_W].astype(F32) * (DK ** -0.5)
    k = qk_ref[rows, QK_W:2 * QK_W].astype(F32)
    z = _dot(a1_ref[rows, :].astype(BF16), a2_ref[...]) + ab_ref[...]
    log_a = _log2_sigmoid(z) * (1.0 / GATE_TAU)
    la_ref[8:8 + CHUNK, :] = log_a
    ltri = ltri_ref[...]
    hi, mid, lo = _split3(log_a)
    cum = _dot(ltri, hi) + _dot(ltri, mid) + _dot(ltri, lo)
    cum_ref[...] = cum

    lvl = lvl_ref[...]
    groups = CHUNK // 8
    rows8 = lambda g: slice(8 * g, 8 * g + 8)
    heads = [slice(h * DKP, (h + 1) * DKP) for h in range(HEADS)]
    row = lax.broadcasted_iota(jnp.int32, (CHUNK, QK_W), 0)

    def scores(xq, yk):
        return [_dot_nt(xq[:, hs], yk[:, hs]) for hs in heads]

    qb = q.astype(BF16)
    kb = k.astype(BF16)
    attn = [[jnp.where(lvl[rows8(g), :] == -1, p[rows8(g), :], 0.0) for g in range(groups)]
            for p in scores(qb, kb)]

    def merge(level, parts, q_groups):
        for h in range(HEADS):
            for i, g in enumerate(q_groups):
                attn[h][g] = jnp.where(lvl[rows8(g), :] == level, parts[h][rows8(i), :], attn[h][g])

    e = jnp.exp2(jnp.where((row & 1) == 1, log_a, 0.0))
    merge(0, scores((q * e).astype(BF16), kb), range(groups))
    nxt = la_ref[pl.ds(9, CHUNK), :]
    prv = la_ref[pl.ds(7, CHUNK), :]
    r4 = row & 3
    e = jnp.exp2(jnp.where(r4 == 0, nxt, jnp.where(r4 == 1, 0.0, jnp.where(r4 == 2, log_a, log_a + prv))))
    merge(1, scores((q * e).astype(BF16), (k * e).astype(BF16)), range(groups))
    sub8 = lax.broadcasted_iota(jnp.int32, (8, QK_W), 0)
    pieces = []
    for g in range(groups):
        d = cum[rows8(g), :] - cum_ref[8 * g + 3:8 * g + 4, :]
        pieces.append(jnp.where(sub8 < 4, -d, d))
    e = jnp.exp2(jnp.concatenate(pieces, axis=0))
    merge(2, scores((q * e).astype(BF16), (k * e).astype(BF16)), range(groups))
    for level in range(3, N_LEVELS):
        m = 1 << level
        xq, yk, q_groups = [], [], []
        for base in range(0, CHUNK, 2 * m):
            k_rows = slice(base, base + m)
            q_rows = slice(base + m, base + 2 * m)
            edge = cum_ref[base + m - 1:base + m, :]
            yk += [k[k_rows, :] * jnp.exp2(edge - cum[k_rows, :]), k[q_rows, :]]
            xq.append(q[q_rows, :] * jnp.exp2(cum[q_rows, :] - edge))
            q_groups += range((base + m) // 8, (base + 2 * m) // 8)
        merge(level, scores(jnp.concatenate(xq, axis=0).astype(BF16),
                            jnp.concatenate(yk, axis=0).astype(BF16)), q_groups)

    last = cum_ref[CHUNK - 1:CHUNK, :]
    q_dec = (q * jnp.exp2(cum)).astype(BF16)
    k_dec = k * jnp.exp2(last - cum)
    decay_all = jnp.exp2(last)

    lane = lax.broadcasted_iota(jnp.int32, (CHUNK, PAIR_V), 1)
    lo_mask = lane < DV

    for p in range(HEADS // 2):
        ks = slice(p * PAIR_K, (p + 1) * PAIR_K)
        vs = slice(p * PAIR_V, (p + 1) * PAIR_V)
        vb = vg_ref[rows, vs]
        zero = jnp.zeros_like(vb)
        v_lo = jnp.where(lo_mask, vb, zero)
        v_hi = jnp.where(lo_mask, zero, vb)
        v_blk = jnp.concatenate([v_lo, v_hi], axis=0)
        state = s_ref[p]
        lhs = jnp.concatenate([jnp.concatenate(attn[2 * p], axis=0).astype(BF16),
                               jnp.concatenate(attn[2 * p + 1], axis=0).astype(BF16),
                               q_dec[:, ks]], axis=1)
        rhs = jnp.concatenate([v_blk, state.astype(BF16)], axis=0)
        o = _dot(lhs, rhs)

        kd_t = k_dec[:, ks].T.astype(BF16)
        upd = jnp.concatenate([_dot(kd_t[0:DKP, :], v_lo), _dot(kd_t[DKP:PAIR_K, :], v_hi)], axis=0)
        dcols = []
        for h in (2 * p, 2 * p + 1):
            d_row = jnp.broadcast_to(decay_all[:, h * DKP:(h + 1) * DKP], (DKP, DKP))
            d_col = d_row.T
            dcols.append(jnp.concatenate([d_col] * (PAIR_V // LANE), axis=1))
        s_ref[p] = state * jnp.concatenate(dcols, axis=0) + upd

        gate = vg_ref[rows, V_W + p * PAIR_V:V_W + (p + 1) * PAIR_V].astype(F32)
        out = o * _head_rms_scale(o) * nrm_ref[:, vs] * (gate * jax.nn.sigmoid(gate))
        o_ref[rows, vs] = out.astype(o_ref.dtype)


def _gla(proj, gates, a2p, abp, nrm, ltri, lvl, batch, seq):
    tok = lambda w, cb: pl.BlockSpec((seq, w), lambda b: (b, cb))
    const = lambda shape: pl.BlockSpec(shape, lambda b: (0,) * len(shape))
    assert OFF_GLA_K == OFF_GLA_Q + QK_W and OFF_GLA_G == OFF_GLA_V + V_W
    return pl.pallas_call(
        _gla_kernel,
        grid=(batch,),
        in_specs=[
            tok(2 * QK_W, OFF_GLA_Q // (2 * QK_W)),
            tok(2 * V_W, OFF_GLA_V // (2 * V_W)),
            tok(LANE, (OFF_GLA_A1 - MAIN_W) // LANE),
            const((LANE, QK_W)),
            const((1, QK_W)),
            const((1, V_W)),
            const((CHUNK, CHUNK)),
            const((CHUNK, CHUNK)),
        ],
        out_specs=pl.BlockSpec((seq, V_W), lambda b: (b, 0)),
        out_shape=jax.ShapeDtypeStruct((batch * seq, V_W), BF16),
        scratch_shapes=[
            pltpu.VMEM((HEADS // 2, PAIR_K, PAIR_V), F32),
            pltpu.VMEM((CHUNK, QK_W), F32),
            pltpu.VMEM((CHUNK + 16, QK_W), F32),
        ],
        compiler_params=pltpu.CompilerParams(
            dimension_semantics=("parallel",), vmem_limit_bytes=VMEM_LIMIT),
        name="gla",
    )(proj, proj, gates, a2p, abp, nrm, ltri, lvl)


ST_W = PAIR_V + LANE
TAIL = 8


def _mlstm_kernel(qk_ref, vo_ref, if_ref, cw_ref, ifb_ref, nrm_ref, ltri_ref, lvl_ref,
                  o_ref, c_ref, m_ref, xe_ref):
    c_ref[...] = jnp.zeros_like(c_ref)
    m_ref[...] = jnp.zeros_like(m_ref)
    xe_ref[0:TAIL, :] = jnp.zeros((TAIL, 2 * QK_W), F32)

    def chunk(c, carry):
        rows = pl.ds(pl.multiple_of(c * CHUNK, CHUNK), CHUNK)
        _mlstm_chunk(rows, qk_ref, vo_ref, if_ref, cw_ref, ifb_ref, nrm_ref, ltri_ref, lvl_ref,
                     o_ref, c_ref, m_ref, xe_ref)
        return carry

    lax.fori_loop(0, qk_ref.shape[0] // CHUNK, chunk, 0, unroll=2)


def _mlstm_chunk(rows, qk_ref, vo_ref, if_ref, cw_ref, ifb_ref, nrm_ref, ltri_ref, lvl_ref,
                 o_ref, c_ref, m_ref, xe_ref):
    xe_ref[TAIL:TAIL + CHUNK, :] = qk_ref[rows, :].astype(F32)
    y = jnp.zeros((CHUNK, 2 * QK_W), F32)
    for j in range(CONV_WIDTH):
        y = y + cw_ref[j:j + 1, :] * xe_ref[pl.ds(TAIL - (CONV_WIDTH - 1) + j, CHUNK), :]
    xe_ref[0:TAIL, :] = xe_ref[CHUNK:CHUNK + TAIL, :]
    y = y * jax.nn.sigmoid(y)
    qm = y[:, 0:QK_W]
    km = y[:, QK_W:2 * QK_W] * (DK ** -0.5)
    qb = qm.astype(BF16)
    kb = km.astype(BF16)
    km_t = km.T

    slab = if_ref[rows, :] + ifb_ref[...]
    lane_g = lax.broadcasted_iota(jnp.int32, (CHUNK, LANE), 1)
    gates = jnp.where(lane_g < HEADS, slab, _log_sigmoid(slab))
    ltri = ltri_ref[...]
    g_hi, g_mid, g_lo = _split3(gates)
    cum_col = _dot(ltri, g_hi) + _dot(ltri, g_mid) + _dot(ltri, g_lo)
    gates_t = gates.T[0:2 * HEADS, :]
    t_hi, t_mid, t_lo = _split3(gates_t)
    cum_row = _dot_nt(t_hi, ltri) + _dot_nt(t_mid, ltri) + _dot_nt(t_lo, ltri)

    lvl = lvl_ref[...]
    causal = lvl >= -1
    lane = lax.broadcasted_iota(jnp.int32, (CHUNK, PAIR_V), 1)
    lo_mask = lane < DV
    lane_e = lax.broadcasted_iota(jnp.int32, (CHUNK, LANE), 1)
    one_col = [jnp.where(lane_e == i, 1.0, 0.0).astype(BF16) for i in range(2)]
    lane_r = lax.broadcasted_iota(jnp.int32, (PAIR_K, LANE), 1)
    row_r = lax.broadcasted_iota(jnp.int32, (PAIR_K, LANE), 0)
    ones_blk = jnp.where(lane_r == jnp.where(row_r < CHUNK, 0, 1), 1.0, 0.0).astype(BF16)

    for p in range(HEADS // 2):
        ks = slice(p * PAIR_K, (p + 1) * PAIR_K)
        vs = slice(p * PAIR_V, (p + 1) * PAIR_V)
        vb = vo_ref[rows, vs]
        zero = jnp.zeros_like(vb)
        sc_parts, qs_parts, floor_parts, kw_parts, dprev_parts = [], [], [], [], []
        for h in (2 * p, 2 * p + 1):
            hs = slice(h * DKP, (h + 1) * DKP)
            cc = jnp.broadcast_to(cum_col[:, HEADS + h:HEADS + h + 1], (CHUNK, CHUNK))
            cum_r = cum_row[HEADS + h:HEADS + h + 1, :]
            ib_r = gates_t[h:h + 1, :]
            m_prev = m_ref[h:h + 1, :]
            dmat = jnp.where(causal, cc - cum_r + ib_r, -jnp.inf)
            inter = cc + m_prev
            m_t = jnp.maximum(inter, jnp.max(dmat, axis=-1, keepdims=True))
            w = jnp.exp(dmat - m_t)
            sc_inter = jnp.exp(inter - m_t)
            sc_parts.append((_dot_nt(qb[:, hs], kb[:, hs]) * w).astype(BF16))
            qs_parts.append((qm[:, hs] * sc_inter).astype(BF16))
            floor_parts.append(jnp.exp(-m_t))
            total = cum_r[:, CHUNK - 1:CHUNK]
            g_row = total - cum_r + ib_r
            m_new = jnp.maximum(total + m_prev, jnp.max(g_row, axis=-1, keepdims=True))
            wj = jnp.exp(g_row - m_new)
            dprev_parts.append(jnp.exp(total + m_prev - m_new))
            kw_parts.append((km_t[hs, :] * wj).astype(BF16))
            m_ref[h:h + 1, :] = m_new

        state = c_ref[p]
        st_m = state.astype(BF16)
        v_lo = jnp.where(lo_mask, vb, zero)
        v_hi = jnp.where(lo_mask, zero, vb)
        v_blk = jnp.concatenate([v_lo, v_hi], axis=0)
        rhs = jnp.concatenate([jnp.concatenate([v_blk, ones_blk], axis=1), st_m], axis=0)
        lhs = jnp.concatenate(sc_parts + qs_parts, axis=1)
        res = _dot(lhs, rhs)
        num = res[:, 0:PAIR_V]
        den = _pair_cols(res[:, PAIR_V:PAIR_V + 1], res[:, PAIR_V + 1:PAIR_V + 2], CHUNK)
        floor = _pair_cols(floor_parts[0][:, 0:1], floor_parts[1][:, 0:1], CHUNK)
        hid = num / jnp.maximum(jnp.abs(den), floor)

        upd = jnp.concatenate(
            [_dot(kw_parts[0], jnp.concatenate([v_lo, one_col[0]], axis=1)),
             _dot(kw_parts[1], jnp.concatenate([v_hi, one_col[1]], axis=1))], axis=0)
        d_rows = jnp.concatenate(
            [jnp.broadcast_to(jnp.concatenate([d] * (ST_W // LANE), axis=1), (DKP, ST_W))
             for d in dprev_parts], axis=0)
        c_ref[p] = d_rows * state + upd

        og = vo_ref[rows, V_W + p * PAIR_V:V_W + (p + 1) * PAIR_V].astype(F32)
        out = hid * _head_rms_scale(hid) * nrm_ref[:, vs] * jax.nn.sigmoid(og)
        o_ref[rows, vs] = out.astype(o_ref.dtype)


def _mlstm(proj, gates, cw, ifb, nrm, ltri, lvl, batch, seq):
    tok = lambda w, cb: pl.BlockSpec((seq, w), lambda b: (b, cb))
    const = lambda shape: pl.BlockSpec(shape, lambda b: (0,) * len(shape))
    assert OFF_ML_K == OFF_ML_Q + QK_W and OFF_ML_O == OFF_ML_V + V_W
    return pl.pallas_call(
        _mlstm_kernel,
        grid=(batch,),
        in_specs=[
            tok(2 * QK_W, OFF_ML_Q // (2 * QK_W)),
            tok(2 * V_W, OFF_ML_V // (2 * V_W)),
            tok(LANE, (OFF_ML_IF - MAIN_W) // LANE),
            const((CONV_WIDTH, 2 * QK_W)),
            const((1, LANE)),
            const((1, V_W)),
            const((CHUNK, CHUNK)),
            const((CHUNK, CHUNK)),
        ],
        out_specs=pl.BlockSpec((seq, V_W), lambda b: (b, 0)),
        out_shape=jax.ShapeDtypeStruct((batch * seq, V_W), BF16),
        scratch_shapes=[
            pltpu.VMEM((HEADS // 2, PAIR_K, ST_W), F32),
            pltpu.VMEM((2 * HEADS, LANE), F32),
            pltpu.VMEM((TAIL + CHUNK, 2 * QK_W), F32),
        ],
        compiler_params=pltpu.CompilerParams(
            dimension_semantics=("parallel",), vmem_limit_bytes=VMEM_LIMIT),
        name="mlstm",
    )(proj, proj, gates, cw, ifb, nrm, ltri, lvl)


def _sgu_kernel(u_ref, v_ref, lng_ref, lnb_ref, w_ref, b_ref, lvl_ref, o_ref):
    causal = lvl_ref[...] >= -1
    w_causal = [jnp.where(causal, w_ref[g], 0.0).astype(BF16) for g in range(SGU_GROUPS)]
    for n in range(u_ref.shape[0] // CHUNK):
        rows = slice(n * CHUNK, (n + 1) * CHUNK)
        u = jax.nn.gelu(u_ref[rows, :].astype(F32))
        v = jax.nn.gelu(v_ref[rows, :].astype(F32))
        mu = jnp.mean(v, axis=-1, keepdims=True)
        var = jnp.mean(jnp.square(v - mu), axis=-1, keepdims=True)
        vn = ((v - mu) * lax.rsqrt(var + EPS) * lng_ref[...] + lnb_ref[...]).astype(BF16)
        for g in range(SGU_GROUPS):
            gs = slice(g * SGU_CH, (g + 1) * SGU_CH)
            mixed = _dot(w_causal[g], vn[:, gs]) + b_ref[:, gs]
            o_ref[rows, gs] = (u[:, gs] * mixed).astype(o_ref.dtype)


def _sgu(proj, lng, lnb, w, b_full, lvl, batch, seq):
    step = CHUNK * SGU_STEP
    while (batch * seq) % step:
        step //= 2
    nb = batch * seq // step
    tok = lambda w_, cb: pl.BlockSpec((step, w_), lambda i: (i, cb))
    const = lambda shape: pl.BlockSpec(shape, lambda i: (0,) * len(shape))
    return pl.pallas_call(
        _sgu_kernel,
        grid=(nb,),
        in_specs=[
            tok(SGU_W, OFF_SGU_U // SGU_W),
            tok(SGU_W, OFF_SGU_V // SGU_W),
            const((1, SGU_W)),
            const((1, SGU_W)),
            const((SGU_GROUPS, CHUNK, CHUNK)),
            const((CHUNK, SGU_W)),
            const((CHUNK, CHUNK)),
        ],
        out_specs=pl.BlockSpec((step, SGU_W), lambda i: (i, 0)),
        out_shape=jax.ShapeDtypeStruct((batch * seq, SGU_W), BF16),
        compiler_params=pltpu.CompilerParams(
            dimension_semantics=("parallel",), vmem_limit_bytes=VMEM_LIMIT),
        name="sgu",
    )(proj, proj, lng, lnb, w, b_full, lvl)


def _out_proj_kernel(x_ref, a_ref, b_ref, c_ref, wa_ref, wb_ref, wc_ref, o_ref):
    o_ref[...] = (x_ref[...] + _dot(a_ref[...], wa_ref[...]) + _dot(b_ref[...], wb_ref[...])
                  + _dot(c_ref[...], wc_ref[...]))


def _out_proj(x2, mix_a, mix_b, mix_c, w_out, tm, tn):
    t, d = x2.shape
    return pl.pallas_call(
        _out_proj_kernel,
        grid=(d // tn, t // tm),
        in_specs=[
            pl.BlockSpec((tm, tn), lambda j, i: (i, j)),
            pl.BlockSpec((tm, V_W), lambda j, i: (i, 0)),
            pl.BlockSpec((tm, V_W), lambda j, i: (i, 0)),
            pl.BlockSpec((tm, SGU_W), lambda j, i: (i, 0)),
            pl.BlockSpec((V_W, tn), lambda j, i: (0, j)),
            pl.BlockSpec((V_W, tn), lambda j, i: (1, j)),
            pl.BlockSpec((SGU_W, tn), lambda j, i: (2 * V_W // SGU_W, j)),
        ],
        out_specs=pl.BlockSpec((tm, tn), lambda j, i: (i, j)),
        out_shape=jax.ShapeDtypeStruct((t, d), F32),
        compiler_params=pltpu.CompilerParams(
            dimension_semantics=("parallel", "arbitrary"), vmem_limit_bytes=VMEM_LIMIT),
        name="out_proj",
    )(x2, mix_a, mix_b, mix_c, w_out, w_out, w_out)


def _ffn_kernel(x_ref, g_ref, wg_ref, wu_ref, wd_ref, gf_ref, *rest, final_norm, plans):
    n = len(plans)
    o_ref, h_ref = rest[n], rest[2 * n + 1]
    j = pl.program_id(1)
    _run_casts(pl.program_id(0) * pl.num_programs(1) + j, plans, rest[0:n], rest[n + 1:2 * n + 1])

    @pl.when(j == 0)
    def _():
        x = x_ref[...]
        ms = jnp.mean(x * x, axis=-1, keepdims=True)
        h_ref[...] = (x * lax.rsqrt(ms + EPS) * g_ref[...]).astype(BF16)
        o_ref[...] = x

    h = h_ref[...]
    gate = _dot(h, wg_ref[...])
    up = _dot(h, wu_ref[...])
    act = (gate * jax.nn.sigmoid(gate) * up).astype(BF16)
    o_ref[...] += _dot(act, wd_ref[...])

    if final_norm:
        @pl.when(j == pl.num_programs(1) - 1)
        def _():
            y = o_ref[...]
            ms = jnp.mean(y * y, axis=-1, keepdims=True)
            o_ref[...] = y * lax.rsqrt(ms + EPS) * gf_ref[...]


def _ffn(x2, g, w_gu, w_down, g_final, tm, tf, final_norm, casts=()):
    t, d = x2.shape
    nf = D_FF // tf
    plans = [_cast_plan(cw, cl, (t // tm) * nf) for cw, cl in casts]
    c_in, c_out, c_shapes, c_args = _cast_specs(plans, lambda i, j: i * nf + j)
    return pl.pallas_call(
        functools.partial(_ffn_kernel, final_norm=final_norm, plans=plans),
        grid=(t // tm, nf),
        in_specs=[
            pl.BlockSpec((tm, d), lambda i, j: (i, 0)),
            pl.BlockSpec((1, d), lambda i, j: (0, 0)),
            pl.BlockSpec((d, tf), lambda i, j: (0, j)),
            pl.BlockSpec((d, tf), lambda i, j: (0, nf + j)),
            pl.BlockSpec((tf, d), lambda i, j: (j, 0)),
            pl.BlockSpec((1, d), lambda i, j: (0, 0)),
        ] + c_in,
        out_specs=[pl.BlockSpec((tm, d), lambda i, j: (i, 0))] + c_out,
        out_shape=[jax.ShapeDtypeStruct((t, d), F32)] + c_shapes,
        scratch_shapes=[pltpu.VMEM((tm, d), BF16)],
        compiler_params=pltpu.CompilerParams(
            dimension_semantics=("arbitrary", "arbitrary"), vmem_limit_bytes=VMEM_LIMIT),
        name="ffn",
    )(x2, g, w_gu, w_gu, w_down, g_final, *c_args)


def _tiles(tokens):
    tm = 1024
    while tokens % tm:
        tm //= 2
    return tm


def kernel(x, norm_mix, w_in, gla_a2, gla_ab, gla_norm, ml_conv, ml_ib, ml_fb, ml_norm,
           sgu_ln_g, sgu_ln_b, sgu_w, sgu_b, w_out, norm_ffn, w_gu, w_down, norm_final):
    batch, seq, d = x.shape
    depth = w_in.shape[0]
    tokens = batch * seq
    tm = _tiles(tokens)

    lvl = jnp.asarray(_pair_level_matrix())
    ltri = jnp.asarray(np.tril(np.ones((CHUNK, CHUNK), np.float32)), BF16)

    w_in_b = _relayout_w_in(w_in)
    dense = (w_out, w_gu, w_down)

    xc = x.reshape(tokens, d)
    for l in range(depth):
        if l == 0:
            proj, gates, w_out_b, w_gu_b, w_down_b = _in_proj(
                xc, norm_mix[l][None, :], w_in_b, l, tm, 1536, casts=[(w, 0) for w in dense])
        else:
            proj, gates = _in_proj(xc, norm_mix[l][None, :], w_in_b, l, tm, 1536)

        a2p = jnp.pad(_pad_heads(gla_a2[l]), ((0, LANE - GATE_RANK), (0, 0))).astype(BF16)
        abp = _pad_heads(gla_ab[l])[None, :]
        mix_a = _gla(proj, gates, a2p, abp, gla_norm[l][None, :], ltri, lvl, batch, seq)

        cw = jnp.concatenate([_pad_heads(ml_conv[l][:, :HEADS * DK]),
                              _pad_heads(ml_conv[l][:, HEADS * DK:])], axis=1)
        ifb = jnp.pad(jnp.concatenate([ml_ib[l], ml_fb[l]]), (0, LANE - 2 * HEADS))[None, :]
        mix_b = _mlstm(proj, gates, cw, ifb, ml_norm[l][None, :], ltri, lvl, batch, seq)

        b_full = jnp.repeat(sgu_b[l].T, SGU_CH, axis=1)
        mix_c = _sgu(proj, sgu_ln_g[l][None, :], sgu_ln_b[l][None, :], sgu_w[l], b_full, lvl, batch, seq)

        x1 = _out_proj(xc, mix_a, mix_b, mix_c, w_out_b, tm // 2, d)
        nxt = [(w, l + 1) for w in dense] if l + 1 < depth else []
        xc, *cast = _ffn(x1, norm_ffn[l][None, :], w_gu_b, w_down_b, norm_final[None, :], tm, 512,
                         final_norm=(l == depth - 1), casts=nxt)
        if cast:
            w_out_b, w_gu_b, w_down_b = cast
    return xc.reshape(batch, seq, d)
```

```python
import functools

import numpy as np
import jax
import jax.numpy as jnp
from jax import lax
from jax.experimental import pallas as pl
from jax.experimental.pallas import tpu as pltpu

F32 = jnp.float32
BF16 = jnp.bfloat16

D_MODEL = 2048
HEADS = 4
DK = 96
DKP = 128
DV = 192
QK_W = HEADS * DKP
V_W = HEADS * DV
PAIR_K = 2 * DKP
PAIR_V = 2 * DV
SGU_W = 512
SGU_GROUPS = 4
SGU_CH = 128
GATE_RANK = 16
GATE_TAU = 16.0
CONV_WIDTH = 4
D_FF = 5632
EPS = 1e-6
CHUNK = 128
N_LEVELS = 7
LANE = 128

OFF_GLA_Q, OFF_GLA_K, OFF_ML_Q, OFF_ML_K = 0, 512, 1024, 1536
OFF_SGU_U, OFF_SGU_V = 2048, 2560
OFF_GLA_V, OFF_GLA_G, OFF_ML_V, OFF_ML_O = 3072, 3840, 4608, 5376
MAIN_W = 6144
OFF_GLA_A1, OFF_ML_IF = 6144, 6272
GATE_W = 2 * LANE
PROJ_W = MAIN_W + GATE_W
SGU_STEP = 4

VMEM_LIMIT = 58 * 1024 * 1024


def _column_moves():
    gk = HEADS * DK
    sizes = (gk, gk, V_W, V_W, GATE_RANK, gk, gk, V_W, V_W, HEADS, HEADS, SGU_W, SGU_W)
    starts = np.concatenate([[0], np.cumsum(sizes)]).tolist()
    (gq, gkk, gv, gg, ga1, mq, mk, mv, mo, mi, mf, su, sv) = starts[:-1]
    moves = []
    for dst, src in ((OFF_GLA_Q, gq), (OFF_GLA_K, gkk), (OFF_ML_Q, mq), (OFF_ML_K, mk)):
        moves += [(dst + h * DKP, src + h * DK, DK) for h in range(HEADS)]
    moves += [(OFF_SGU_U, su, SGU_W), (OFF_SGU_V, sv, SGU_W), (OFF_GLA_V, gv, V_W), (OFF_GLA_G, gg, V_W),
              (OFF_ML_V, mv, V_W), (OFF_ML_O, mo, V_W), (OFF_GLA_A1, ga1, GATE_RANK),
              (OFF_ML_IF, mi, 2 * HEADS)]
    assert mf == mi + HEADS and starts[-1] == sum(sizes)
    return moves


def _relayout_kernel(w_ref, o_ref):
    o_ref[...] = jnp.zeros_like(o_ref)
    for dst, src, n in _column_moves():
        o_ref[:, dst:dst + n] = w_ref[:, src:src + n].astype(BF16)


def _relayout_w_in(w_in, rows=256):
    depth, d, p = w_in.shape
    return pl.pallas_call(
        _relayout_kernel,
        grid=(depth, d // rows),
        in_specs=[pl.BlockSpec((None, rows, p), lambda l, i: (l, i, 0))],
        out_specs=pl.BlockSpec((None, rows, PROJ_W), lambda l, i: (l, i, 0)),
        out_shape=jax.ShapeDtypeStruct((depth, d, PROJ_W), BF16),
        compiler_params=pltpu.CompilerParams(
            dimension_semantics=("parallel", "parallel"), vmem_limit_bytes=VMEM_LIMIT),
        name="relayout",
    )(w_in)


def _pad_heads(a):
    lead = a.shape[:-1]
    a = a.reshape(lead + (HEADS, DK))
    a = jnp.pad(a, [(0, 0)] * len(lead) + [(0, 0), (0, DKP - DK)])
    return a.reshape(lead + (QK_W,))


def _pair_level_matrix():
    c = CHUNK
    t = np.arange(c)[:, None]
    s = np.arange(c)[None, :]
    x = np.bitwise_xor(t, s)
    lvl = np.floor(np.log2(np.maximum(x, 1))).astype(np.int32)
    lvl = np.where(s == t, -1, lvl)
    lvl = np.where(s > t, -2, lvl)
    return lvl.astype(np.int32)


def _dot(a, b):
    return jnp.dot(a, b, preferred_element_type=F32)


def _dot_nt(a, b):
    return lax.dot_general(a, b, (((1,), (1,)), ((), ())), preferred_element_type=F32)


def _split3(x):
    hi = x.astype(BF16)
    r = x - hi.astype(F32)
    mid = r.astype(BF16)
    lo = (r - mid.astype(F32)).astype(BF16)
    return hi, mid, lo


LOG2E = 1.4426950408889634
LN2 = 0.6931471805599453


def _log2_sigmoid(x):
    xl = x * LOG2E
    return jnp.minimum(xl, 0.0) - jnp.log2(1.0 + jnp.exp2(-jnp.abs(xl)))


def _log_sigmoid(x):
    return _log2_sigmoid(x) * LN2


def _pair_cols(c0, c1, rows):
    first = lax.broadcasted_iota(jnp.int32, (rows, LANE), 1) < DV - LANE
    b0 = jnp.broadcast_to(c0, (rows, LANE))
    b1 = jnp.broadcast_to(c1, (rows, LANE))
    return jnp.concatenate([b0, jnp.where(first, b0, b1), b1], axis=1)


def _head_rms_scale(o):
    rows = o.shape[0]
    o2 = o * o
    first = lax.broadcasted_iota(jnp.int32, (rows, LANE), 1) < DV - LANE
    mid = o2[:, LANE:2 * LANE]
    ss0 = jnp.sum(o2[:, 0:LANE] + jnp.where(first, mid, 0.0), axis=-1, keepdims=True)
    ss1 = jnp.sum(o2[:, 2 * LANE:3 * LANE] + jnp.where(first, 0.0, mid), axis=-1, keepdims=True)
    return _pair_cols(lax.rsqrt(ss0 / DV + EPS), lax.rsqrt(ss1 / DV + EPS), rows)


def _cast_plan(w, layer, steps):
    rows, cols = w.shape[1], w.shape[2]
    nblk = max(n for n in range(1, steps + 1) if rows % n == 0 and (rows // n) % 16 == 0)
    return w, layer, rows, cols, nblk


def _cast_specs(plans, step_of):
    in_specs, out_specs, out_shapes, args = [], [], [], []
    for w, layer, rows, cols, nblk in plans:
        blk = lambda *g, nblk=nblk: jnp.minimum(step_of(*g), nblk - 1)
        in_specs.append(pl.BlockSpec((None, rows // nblk, cols), lambda *g, blk=blk, layer=layer: (layer, blk(*g), 0)))
        out_specs.append(pl.BlockSpec((rows // nblk, cols), lambda *g, blk=blk: (blk(*g), 0)))
        out_shapes.append(jax.ShapeDtypeStruct((rows, cols), BF16))
        args.append(w)
    return in_specs, out_specs, out_shapes, args


def _run_casts(step, plans, src_refs, dst_refs):
    for (_, _, _, _, nblk), src, dst in zip(plans, src_refs, dst_refs):
        @pl.when(step < nblk)
        def _(src=src, dst=dst):
            dst[...] = src[...].astype(BF16)


def _in_proj_kernel(x_ref, g_ref, w_ref, wg_ref, *rest, plans):
    n = len(plans)
    o_ref, og_ref, h_ref = rest[n], rest[n + 1], rest[2 * n + 2]
    _run_casts(pl.program_id(0) * pl.num_programs(1) + pl.program_id(1), plans,
               rest[0:n], rest[n + 2:2 * n + 2])

    @pl.when(pl.program_id(1) == 0)
    def _():
        x = x_ref[...]
        ms = jnp.mean(x * x, axis=-1, keepdims=True)
        h_ref[...] = (x * lax.rsqrt(ms + EPS) * g_ref[...]).astype(BF16)
        og_ref[...] = _dot(h_ref[...], wg_ref[...])

    o_ref[...] = _dot(h_ref[...], w_ref[...]).astype(o_ref.dtype)


def _in_proj(x2, g, w, layer, tm, tn, casts=()):
    t, d = x2.shape
    ni, nj = t // tm, MAIN_W // tn
    plans = [_cast_plan(cw, cl, ni * nj) for cw, cl in casts]
    c_in, c_out, c_shapes, c_args = _cast_specs(plans, lambda i, j: i * nj + j)
    return pl.pallas_call(
        functools.partial(_in_proj_kernel, plans=plans),
        grid=(ni, nj),
        in_specs=[
            pl.BlockSpec((tm, d), lambda i, j: (i, 0)),
            pl.BlockSpec((1, d), lambda i, j: (0, 0)),
            pl.BlockSpec((None, d, tn), lambda i, j: (layer, 0, j)),
            pl.BlockSpec((None, d, GATE_W), lambda i, j: (layer, 0, MAIN_W // GATE_W)),
        ] + c_in,
        out_specs=[pl.BlockSpec((tm, tn), lambda i, j: (i, j)),
                   pl.BlockSpec((tm, GATE_W), lambda i, j: (i, 0))] + c_out,
        out_shape=[jax.ShapeDtypeStruct((t, MAIN_W), BF16),
                   jax.ShapeDtypeStruct((t, GATE_W), F32)] + c_shapes,
        scratch_shapes=[pltpu.VMEM((tm, d), BF16)],
        compiler_params=pltpu.CompilerParams(
            dimension_semantics=("arbitrary", "arbitrary"), vmem_limit_bytes=VMEM_LIMIT),
        name="in_proj",
    )(x2, g, w, w, *c_args)


def _gla_kernel(qk_ref, vg_ref, a1_ref, a2_ref, ab_ref, nrm_ref, ltri_ref, lvl_ref,
                o_ref, s_ref, cum_ref, la_ref):
    s_ref[...] = jnp.zeros_like(s_ref)
    la_ref[0:8, :] = jnp.zeros((8, QK_W), F32)
    la_ref[8 + CHUNK:16 + CHUNK, :] = jnp.zeros((8, QK_W), F32)

    def chunk(c, carry):
        rows = pl.ds(pl.multiple_of(c * CHUNK, CHUNK), CHUNK)
        _gla_chunk(rows, qk_ref, vg_ref, a1_ref, a2_ref, ab_ref, nrm_ref, ltri_ref, lvl_ref,
                   o_ref, s_ref, cum_ref, la_ref)
        return carry

    lax.fori_loop(0, qk_ref.shape[0] // CHUNK, chunk, 0)


def _gla_chunk(rows, qk_ref, vg_ref, a1_ref, a2_ref, ab_ref, nrm_ref, ltri_ref, lvl_ref,
               o_ref, s_ref, cum_ref, la_ref):
    q = qk_ref[rows, 0:QK_W].astype(F32) * (DK ** -0.5)
    k = qk_ref[rows, QK_W:2 * QK_W].astype(F32)
    z = _dot(a1_ref[rows, :].astype(BF16), a2_ref[...]) + ab_ref[...]
    log_a = _log2_sigmoid(z) * (1.0 / GATE_TAU)
    la_ref[8:8 + CHUNK, :] = log_a
    ltri = ltri_ref[...]
    hi, mid, lo = _split3(log_a)
    cum = _dot(ltri, hi) + _dot(ltri, mid) + _dot(ltri, lo)
    cum_ref[...] = cum

    lvl = lvl_ref[...]
    groups = CHUNK // 8
    rows8 = lambda g: slice(8 * g, 8 * g + 8)
    heads = [slice(h * DKP, (h + 1) * DKP) for h in range(HEADS)]
    row = lax.broadcasted_iota(jnp.int32, (CHUNK, QK_W), 0)

    def scores(xq, yk):
        return [_dot_nt(xq[:, hs], yk[:, hs]) for hs in heads]

    qb = q.astype(BF16)
    kb = k.astype(BF16)
    attn = [[jnp.where(lvl[rows8(g), :] == -1, p[rows8(g), :], 0.0) for g in range(groups)]
            for p in scores(qb, kb)]

    def merge(level, parts, q_groups):
        for h in range(HEADS):
            for i, g in enumerate(q_groups):
                attn[h][g] = jnp.where(lvl[rows8(g), :] == level, parts[h][rows8(i), :], attn[h][g])

    e = jnp.exp2(jnp.where((row & 1) == 1, log_a, 0.0))
    merge(0, scores((q * e).astype(BF16), kb), range(groups))
    nxt = la_ref[pl.ds(9, CHUNK), :]
    prv = la_ref[pl.ds(7, CHUNK), :]
    r4 = row & 3
    e = jnp.exp2(jnp.where(r4 == 0, nxt, jnp.where(r4 == 1, 0.0, jnp.where(r4 == 2, log_a, log_a + prv))))
    merge(1, scores((q * e).astype(BF16), (k * e).astype(BF16)), range(groups))
    sub8 = lax.broadcasted_iota(jnp.int32, (8, QK_W), 0)
    pieces = []
    for g in range(groups):
        d = cum[rows8(g), :] - cum_ref[8 * g + 3:8 * g + 4, :]
        pieces.append(jnp.where(sub8 < 4, -d, d))
    e = jnp.exp2(jnp.concatenate(pieces, axis=0))
    merge(2, scores((q * e).astype(BF16), (k * e).astype(BF16)), range(groups))
    for level in range(3, N_LEVELS):
        m = 1 << level
        xq, yk, q_groups = [], [], []
        for base in range(0, CHUNK, 2 * m):
            k_rows = slice(base, base + m)
            q_rows = slice(base + m, base + 2 * m)
            edge = cum_ref[base + m - 1:base + m, :]
            yk += [k[k_rows, :] * jnp.exp2(edge - cum[k_rows, :]), k[q_rows, :]]
            xq.append(q[q_rows, :] * jnp.exp2(cum[q_rows, :] - edge))
            q_groups += range((base + m) // 8, (base + 2 * m) // 8)
        merge(level, scores(jnp.concatenate(xq, axis=0).astype(BF16),
                            jnp.concatenate(yk, axis=0).astype(BF16)), q_groups)

    last = cum_ref[CHUNK - 1:CHUNK, :]
    q_dec = (q * jnp.exp2(cum)).astype(BF16)
    k_dec = k * jnp.exp2(last - cum)
    decay_all = jnp.exp2(last)

    lane = lax.broadcasted_iota(jnp.int32, (CHUNK, PAIR_V), 1)
    lo_mask = lane < DV

    for p in range(HEADS // 2):
        ks = slice(p * PAIR_K, (p + 1) * PAIR_K)
        vs = slice(p * PAIR_V, (p + 1) * PAIR_V)
        vb = vg_ref[rows, vs]
        zero = jnp.zeros_like(vb)
        v_lo = jnp.where(lo_mask, vb, zero)
        v_hi = jnp.where(lo_mask, zero, vb)
        v_blk = jnp.concatenate([v_lo, v_hi], axis=0)
        state = s_ref[p]
        lhs = jnp.concatenate([jnp.concatenate(attn[2 * p], axis=0).astype(BF16),
                               jnp.concatenate(attn[2 * p + 1], axis=0).astype(BF16),
                               q_dec[:, ks]], axis=1)
        rhs = jnp.concatenate([v_blk, state.astype(BF16)], axis=0)
        o = _dot(lhs, rhs)

        kd_t = k_dec[:, ks].T.astype(BF16)
        upd = jnp.concatenate([_dot(kd_t[0:DKP, :], v_lo), _dot(kd_t[DKP:PAIR_K, :], v_hi)], axis=0)
        dcols = []
        for h in (2 * p, 2 * p + 1):
            d_row = jnp.broadcast_to(decay_all[:, h * DKP:(h + 1) * DKP], (DKP, DKP))
            d_col = d_row.T
            dcols.append(jnp.concatenate([d_col] * (PAIR_V // LANE), axis=1))
        s_ref[p] = state * jnp.concatenate(dcols, axis=0) + upd

        gate = vg_ref[rows, V_W + p * PAIR_V:V_W + (p + 1) * PAIR_V].astype(F32)
        out = o * _head_rms_scale(o) * nrm_ref[:, vs] * (gate * jax.nn.sigmoid(gate))
        o_ref[rows, vs] = out.astype(o_ref.dtype)


def _gla(proj, gates, a2p, abp, nrm, ltri, lvl, batch, seq):
    tok = lambda w, cb: pl.BlockSpec((seq, w), lambda b: (b, cb))
    const = lambda shape: pl.BlockSpec(shape, lambda b: (0,) * len(shape))
    assert OFF_GLA_K == OFF_GLA_Q + QK_W and OFF_GLA_G == OFF_GLA_V + V_W
    return pl.pallas_call(
        _gla_kernel,
        grid=(batch,),
        in_specs=[
            tok(2 * QK_W, OFF_GLA_Q // (2 * QK_W)),
            tok(2 * V_W, OFF_GLA_V // (2 * V_W)),
            tok(LANE, (OFF_GLA_A1 - MAIN_W) // LANE),
            const((LANE, QK_W)),
            const((1, QK_W)),
            const((1, V_W)),
            const((CHUNK, CHUNK)),
            const((CHUNK, CHUNK)),
        ],
        out_specs=pl.BlockSpec((seq, V_W), lambda b: (b, 0)),
        out_shape=jax.ShapeDtypeStruct((batch * seq, V_W), BF16),
        scratch_shapes=[
            pltpu.VMEM((HEADS // 2, PAIR_K, PAIR_V), F32),
            pltpu.VMEM((CHUNK, QK_W), F32),
            pltpu.VMEM((CHUNK + 16, QK_W), F32),
        ],
        compiler_params=pltpu.CompilerParams(
            dimension_semantics=("parallel",), vmem_limit_bytes=VMEM_LIMIT),
        name="gla",
    )(proj, proj, gates, a2p, abp, nrm, ltri, lvl)


ST_W = PAIR_V + LANE
TAIL = 8


def _mlstm_kernel(qk_ref, vo_ref, if_ref, cw_ref, ifb_ref, nrm_ref, ltri_ref, lvl_ref,
                  o_ref, c_ref, m_ref, xe_ref):
    c_ref[...] = jnp.zeros_like(c_ref)
    m_ref[...] = jnp.zeros_like(m_ref)
    xe_ref[0:TAIL, :] = jnp.zeros((TAIL, 2 * QK_W), F32)

    def chunk(c, carry):
        rows = pl.ds(pl.multiple_of(c * CHUNK, CHUNK), CHUNK)
        _mlstm_chunk(rows, qk_ref, vo_ref, if_ref, cw_ref, ifb_ref, nrm_ref, ltri_ref, lvl_ref,
                     o_ref, c_ref, m_ref, xe_ref)
        return carry

    lax.fori_loop(0, qk_ref.shape[0] // CHUNK, chunk, 0, unroll=2)


def _mlstm_chunk(rows, qk_ref, vo_ref, if_ref, cw_ref, ifb_ref, nrm_ref, ltri_ref, lvl_ref,
                 o_ref, c_ref, m_ref, xe_ref):
    xe_ref[TAIL:TAIL + CHUNK, :] = qk_ref[rows, :].astype(F32)
    y = jnp.zeros((CHUNK, 2 * QK_W), F32)
    for j in range(CONV_WIDTH):
        y = y + cw_ref[j:j + 1, :] * xe_ref[pl.ds(TAIL - (CONV_WIDTH - 1) + j, CHUNK), :]
    xe_ref[0:TAIL, :] = xe_ref[CHUNK:CHUNK + TAIL, :]
    y = y * jax.nn.sigmoid(y)
    qm = y[:, 0:QK_W]
    km = y[:, QK_W:2 * QK_W] * (DK ** -0.5)
    qb = qm.astype(BF16)
    kb = km.astype(BF16)
    km_t = km.T

    slab = if_ref[rows, :] + ifb_ref[...]
    lane_g = lax.broadcasted_iota(jnp.int32, (CHUNK, LANE), 1)
    gates = jnp.where(lane_g < HEADS, slab, _log_sigmoid(slab))
    ltri = ltri_ref[...]
    g_hi, g_mid, g_lo = _split3(gates)
    cum_col = _dot(ltri, g_hi) + _dot(ltri, g_mid) + _dot(ltri, g_lo)
    gates_t = gates.T[0:2 * HEADS, :]
    t_hi, t_mid, t_lo = _split3(gates_t)
    cum_row = _dot_nt(t_hi, ltri) + _dot_nt(t_mid, ltri) + _dot_nt(t_lo, ltri)

    lvl = lvl_ref[...]
    causal = lvl >= -1
    lane = lax.broadcasted_iota(jnp.int32, (CHUNK, PAIR_V), 1)
    lo_mask = lane < DV
    lane_e = lax.broadcasted_iota(jnp.int32, (CHUNK, LANE), 1)
    one_col = [jnp.where(lane_e == i, 1.0, 0.0).astype(BF16) for i in range(2)]
    lane_r = lax.broadcasted_iota(jnp.int32, (PAIR_K, LANE), 1)
    row_r = lax.broadcasted_iota(jnp.int32, (PAIR_K, LANE), 0)
    ones_blk = jnp.where(lane_r == jnp.where(row_r < CHUNK, 0, 1), 1.0, 0.0).astype(BF16)

    for p in range(HEADS // 2):
        ks = slice(p * PAIR_K, (p + 1) * PAIR_K)
        vs = slice(p * PAIR_V, (p + 1) * PAIR_V)
        vb = vo_ref[rows, vs]
        zero = jnp.zeros_like(vb)
        sc_parts, qs_parts, floor_parts, kw_parts, dprev_parts = [], [], [], [], []
        for h in (2 * p, 2 * p + 1):
            hs = slice(h * DKP, (h + 1) * DKP)
            cc = jnp.broadcast_to(cum_col[:, HEADS + h:HEADS + h + 1], (CHUNK, CHUNK))
            cum_r = cum_row[HEADS + h:HEADS + h + 1, :]
            ib_r = gates_t[h:h + 1, :]
            m_prev = m_ref[h:h + 1, :]
            dmat = jnp.where(causal, cc - cum_r + ib_r, -jnp.inf)
            inter = cc + m_prev
            m_t = jnp.maximum(inter, jnp.max(dmat, axis=-1, keepdims=True))
            w = jnp.exp(dmat - m_t)
            sc_inter = jnp.exp(inter - m_t)
            sc_parts.append((_dot_nt(qb[:, hs], kb[:, hs]) * w).astype(BF16))
            qs_parts.append((qm[:, hs] * sc_inter).astype(BF16))
            floor_parts.append(jnp.exp(-m_t))
            total = cum_r[:, CHUNK - 1:CHUNK]
            g_row = total - cum_r + ib_r
            m_new = jnp.maximum(total + m_prev, jnp.max(g_row, axis=-1, keepdims=True))
            wj = jnp.exp(g_row - m_new)
            dprev_parts.append(jnp.exp(total + m_prev - m_new))
            kw_parts.append((km_t[hs, :] * wj).astype(BF16))
            m_ref[h:h + 1, :] = m_new

        state = c_ref[p]
        st_m = state.astype(BF16)
        v_lo = jnp.where(lo_mask, vb, zero)
        v_hi = jnp.where(lo_mask, zero, vb)
        v_blk = jnp.concatenate([v_lo, v_hi], axis=0)
        rhs = jnp.concatenate([jnp.concatenate([v_blk, ones_blk], axis=1), st_m], axis=0)
        lhs = jnp.concatenate(sc_parts + qs_parts, axis=1)
        res = _dot(lhs, rhs)
        num = res[:, 0:PAIR_V]
        den = _pair_cols(res[:, PAIR_V:PAIR_V + 1], res[:, PAIR_V + 1:PAIR_V + 2], CHUNK)
        floor = _pair_cols(floor_parts[0][:, 0:1], floor_parts[1][:, 0:1], CHUNK)
        hid = num / jnp.maximum(jnp.abs(den), floor)

        upd = jnp.concatenate(
            [_dot(kw_parts[0], jnp.concatenate([v_lo, one_col[0]], axis=1)),
             _dot(kw_parts[1], jnp.concatenate([v_hi, one_col[1]], axis=1))], axis=0)
        d_rows = jnp.concatenate(
            [jnp.broadcast_to(jnp.concatenate([d] * (ST_W // LANE), axis=1), (DKP, ST_W))
             for d in dprev_parts], axis=0)
        c_ref[p] = d_rows * state + upd

        og = vo_ref[rows, V_W + p * PAIR_V:V_W + (p + 1) * PAIR_V].astype(F32)
        out = hid * _head_rms_scale(hid) * nrm_ref[:, vs] * jax.nn.sigmoid(og)
        o_ref[rows, vs] = out.astype(o_ref.dtype)


def _mlstm(proj, gates, cw, ifb, nrm, ltri, lvl, batch, seq):
    tok = lambda w, cb: pl.BlockSpec((seq, w), lambda b: (b, cb))
    const = lambda shape: pl.BlockSpec(shape, lambda b: (0,) * len(shape))
    assert OFF_ML_K == OFF_ML_Q + QK_W and OFF_ML_O == OFF_ML_V + V_W
    return pl.pallas_call(
        _mlstm_kernel,
        grid=(batch,),
        in_specs=[
            tok(2 * QK_W, OFF_ML_Q // (2 * QK_W)),
            tok(2 * V_W, OFF_ML_V // (2 * V_W)),
            tok(LANE, (OFF_ML_IF - MAIN_W) // LANE),
            const((CONV_WIDTH, 2 * QK_W)),
            const((1, LANE)),
            const((1, V_W)),
            const((CHUNK, CHUNK)),
            const((CHUNK, CHUNK)),
        ],
        out_specs=pl.BlockSpec((seq, V_W), lambda b: (b, 0)),
        out_shape=jax.ShapeDtypeStruct((batch * seq, V_W), BF16),
        scratch_shapes=[
            pltpu.VMEM((HEADS // 2, PAIR_K, ST_W), F32),
            pltpu.VMEM((2 * HEADS, LANE), F32),
            pltpu.VMEM((TAIL + CHUNK, 2 * QK_W), F32),
        ],
        compiler_params=pltpu.CompilerParams(
            dimension_semantics=("parallel",), vmem_limit_bytes=VMEM_LIMIT),
        name="mlstm",
    )(proj, proj, gates, cw, ifb, nrm, ltri, lvl)


def _sgu_kernel(u_ref, v_ref, lng_ref, lnb_ref, w_ref, b_ref, lvl_ref, o_ref):
    causal = lvl_ref[...] >= -1
    w_causal = [jnp.where(causal, w_ref[g], 0.0).astype(BF16) for g in range(SGU_GROUPS)]
    for n in range(u_ref.shape[0] // CHUNK):
        rows = slice(n * CHUNK, (n + 1) * CHUNK)
        u = jax.nn.gelu(u_ref[rows, :].astype(F32))
        v = jax.nn.gelu(v_ref[rows, :].astype(F32))
        mu = jnp.mean(v, axis=-1, keepdims=True)
        var = jnp.mean(jnp.square(v - mu), axis=-1, keepdims=True)
        vn = ((v - mu) * lax.rsqrt(var + EPS) * lng_ref[...] + lnb_ref[...]).astype(BF16)
        for g in range(SGU_GROUPS):
            gs = slice(g * SGU_CH, (g + 1) * SGU_CH)
            mixed = _dot(w_causal[g], vn[:, gs]) + b_ref[:, gs]
            o_ref[rows, gs] = (u[:, gs] * mixed).astype(o_ref.dtype)


def _sgu(proj, lng, lnb, w, b_full, lvl, batch, seq):
    step = CHUNK * SGU_STEP
    while (batch * seq) % step:
        step //= 2
    nb = batch * seq // step
    tok = lambda w_, cb: pl.BlockSpec((step, w_), lambda i: (i, cb))
    const = lambda shape: pl.BlockSpec(shape, lambda i: (0,) * len(shape))
    return pl.pallas_call(
        _sgu_kernel,
        grid=(nb,),
        in_specs=[
            tok(SGU_W, OFF_SGU_U // SGU_W),
            tok(SGU_W, OFF_SGU_V // SGU_W),
            const((1, SGU_W)),
            const((1, SGU_W)),
            const((SGU_GROUPS, CHUNK, CHUNK)),
            const((CHUNK, SGU_W)),
            const((CHUNK, CHUNK)),
        ],
        out_specs=pl.BlockSpec((step, SGU_W), lambda i: (i, 0)),
        out_shape=jax.ShapeDtypeStruct((batch * seq, SGU_W), BF16),
        compiler_params=pltpu.CompilerParams(
            dimension_semantics=("parallel",), vmem_limit_bytes=VMEM_LIMIT),
        name="sgu",
    )(proj, proj, lng, lnb, w, b_full, lvl)


def _out_proj_kernel(x_ref, a_ref, b_ref, c_ref, wa_ref, wb_ref, wc_ref, o_ref):
    o_ref[...] = (x_ref[...] + _dot(a_ref[...], wa_ref[...]) + _dot(b_ref[...], wb_ref[...])
                  + _dot(c_ref[...], wc_ref[...]))


def _out_proj(x2, mix_a, mix_b, mix_c, w_out, tm, tn):
    t, d = x2.shape
    return pl.pallas_call(
        _out_proj_kernel,
        grid=(d // tn, t // tm),
        in_specs=[
            pl.BlockSpec((tm, tn), lambda j, i: (i, j)),
            pl.BlockSpec((tm, V_W), lambda j, i: (i, 0)),
            pl.BlockSpec((tm, V_W), lambda j, i: (i, 0)),
            pl.BlockSpec((tm, SGU_W), lambda j, i: (i, 0)),
            pl.BlockSpec((V_W, tn), lambda j, i: (0, j)),
            pl.BlockSpec((V_W, tn), lambda j, i: (1, j)),
            pl.BlockSpec((SGU_W, tn), lambda j, i: (2 * V_W // SGU_W, j)),
        ],
        out_specs=pl.BlockSpec((tm, tn), lambda j, i: (i, j)),
        out_shape=jax.ShapeDtypeStruct((t, d), F32),
        compiler_params=pltpu.CompilerParams(
            dimension_semantics=("parallel", "arbitrary"), vmem_limit_bytes=VMEM_LIMIT),
        name="out_proj",
    )(x2, mix_a, mix_b, mix_c, w_out, w_out, w_out)


def _ffn_kernel(x_ref, g_ref, wg_ref, wu_ref, wd_ref, gf_ref, *rest, final_norm, plans):
    n = len(plans)
    o_ref, h_ref = rest[n], rest[2 * n + 1]
    j = pl.program_id(1)
    _run_casts(pl.program_id(0) * pl.num_programs(1) + j, plans, rest[0:n], rest[n + 1:2 * n + 1])

    @pl.when(j == 0)
    def _():
        x = x_ref[...]
        ms = jnp.mean(x * x, axis=-1, keepdims=True)
        h_ref[...] = (x * lax.rsqrt(ms + EPS) * g_ref[...]).astype(BF16)
        o_ref[...] = x

    h = h_ref[...]
    gate = _dot(h, wg_ref[...])
    up = _dot(h, wu_ref[...])
    act = (gate * jax.nn.sigmoid(gate) * up).astype(BF16)
    o_ref[...] += _dot(act, wd_ref[...])

    if final_norm:
        @pl.when(j == pl.num_programs(1) - 1)
        def _():
            y = o_ref[...]
            ms = jnp.mean(y * y, axis=-1, keepdims=True)
            o_ref[...] = y * lax.rsqrt(ms + EPS) * gf_ref[...]


def _ffn(x2, g, w_gu, w_down, g_final, tm, tf, final_norm, casts=()):
    t, d = x2.shape
    nf = D_FF // tf
    plans = [_cast_plan(cw, cl, (t // tm) * nf) for cw, cl in casts]
    c_in, c_out, c_shapes, c_args = _cast_specs(plans, lambda i, j: i * nf + j)
    return pl.pallas_call(
        functools.partial(_ffn_kernel, final_norm=final_norm, plans=plans),
        grid=(t // tm, nf),
        in_specs=[
            pl.BlockSpec((tm, d), lambda i, j: (i, 0)),
            pl.BlockSpec((1, d), lambda i, j: (0, 0)),
            pl.BlockSpec((d, tf), lambda i, j: (0, j)),
            pl.BlockSpec((d, tf), lambda i, j: (0, nf + j)),
            pl.BlockSpec((tf, d), lambda i, j: (j, 0)),
            pl.BlockSpec((1, d), lambda i, j: (0, 0)),
        ] + c_in,
        out_specs=[pl.BlockSpec((tm, d), lambda i, j: (i, 0))] + c_out,
        out_shape=[jax.ShapeDtypeStruct((t, d), F32)] + c_shapes,
        scratch_shapes=[pltpu.VMEM((tm, d), BF16)],
        compiler_params=pltpu.CompilerParams(
            dimension_semantics=("arbitrary", "arbitrary"), vmem_limit_bytes=VMEM_LIMIT),
        name="ffn",
    )(x2, g, w_gu, w_gu, w_down, g_final, *c_args)


def _tiles(tokens):
    tm = 1024
    while tokens % tm:
        tm //= 2
    return tm


def kernel(x, norm_mix, w_in, gla_a2, gla_ab, gla_norm, ml_conv, ml_ib, ml_fb, ml_norm,
           sgu_ln_g, sgu_ln_b, sgu_w, sgu_b, w_out, norm_ffn, w_gu, w_down, norm_final):
    batch, seq, d = x.shape
    depth = w_in.shape[0]
    tokens = batch * seq
    tm = _tiles(tokens)

    lvl = jnp.asarray(_pair_level_matrix())
    ltri = jnp.asarray(np.tril(np.ones((CHUNK, CHUNK), np.float32)), BF16)

    w_in_b = _relayout_w_in(w_in)
    dense = (w_out, w_gu, w_down)

    xc = x.reshape(tokens, d)
    for l in range(depth):
        if l == 0:
            proj, gates, w_out_b, w_gu_b, w_down_b = _in_proj(
                xc, norm_mix[l][None, :], w_in_b, l, tm, 1536, casts=[(w, 0) for w in dense])
        else:
            proj, gates = _in_proj(xc, norm_mix[l][None, :], w_in_b, l, tm, 1536)

        a2p = jnp.pad(_pad_heads(gla_a2[l]), ((0, LANE - GATE_RANK), (0, 0))).astype(BF16)
        abp = _pad_heads(gla_ab[l])[None, :]
        mix_a = _gla(proj, gates, a2p, abp, gla_norm[l][None, :], ltri, lvl, batch, seq)

        cw = jnp.concatenate([_pad_heads(ml_conv[l][:, :HEADS * DK]),
                              _pad_heads(ml_conv[l][:, HEADS * DK:])], axis=1)
        ifb = jnp.pad(jnp.concatenate([ml_ib[l], ml_fb[l]]), (0, LANE - 2 * HEADS))[None, :]
        mix_b = _mlstm(proj, gates, cw, ifb, ml_norm[l][None, :], ltri, lvl, batch, seq)

        b_full = jnp.repeat(sgu_b[l].T, SGU_CH, axis=1)
        mix_c = _sgu(proj, sgu_ln_g[l][None, :], sgu_ln_b[l][None, :], sgu_w[l], b_full, lvl, batch, seq)

        x1 = _out_proj(xc, mix_a, mix_b, mix_c, w_out_b, tm // 2, d)
        nxt = [(w, l + 1) for w in dense] if l + 1 < depth else []
        xc, *cast = _ffn(x1, norm_ffn[l][None, :], w_gu_b, w_down_b, norm_final[None, :], tm, 512,
                         final_norm=(l == depth - 1), casts=nxt)
        if cast:
            w_out_b, w_gu_b, w_down_b = cast
    return xc.reshape(batch, seq, d)
```

```python
import functools

import numpy as np
import jax
import jax.numpy as jnp
from jax import lax
from jax.experimental import pallas as pl
from jax.experimental.pallas import tpu as pltpu

F32 = jnp.float32
BF16 = jnp.bfloat16

D_MODEL = 2048
HEADS = 4
DK = 96
DKP = 128
DV = 192
QK_W = HEADS * DKP
V_W = HEADS * DV
PAIR_K = 2 * DKP
PAIR_V = 2 * DV
SGU_W = 512
SGU_GROUPS = 4
SGU_CH = 128
GATE_RANK = 16
GATE_TAU = 16.0
CONV_WIDTH = 4
D_FF = 5632
EPS = 1e-6
CHUNK = 128
N_LEVELS = 7
LANE = 128

OFF_GLA_Q, OFF_GLA_K, OFF_ML_Q, OFF_ML_K = 0, 512, 1024, 1536
OFF_SGU_U, OFF_SGU_V = 2048, 2560
OFF_GLA_V, OFF_GLA_G, OFF_ML_V, OFF_ML_O = 3072, 3840, 4608, 5376
MAIN_W = 6144
OFF_GLA_A1, OFF_ML_IF = 6144, 6272
GATE_W = 2 * LANE
PROJ_W = MAIN_W + GATE_W
SGU_STEP = 4

VMEM_LIMIT = 58 * 1024 * 1024


def _column_moves():
    gk = HEADS * DK
    sizes = (gk, gk, V_W, V_W, GATE_RANK, gk, gk, V_W, V_W, HEADS, HEADS, SGU_W, SGU_W)
    starts = np.concatenate([[0], np.cumsum(sizes)]).tolist()
    (gq, gkk, gv, gg, ga1, mq, mk, mv, mo, mi, mf, su, sv) = starts[:-1]
    moves = []
    for dst, src in ((OFF_GLA_Q, gq), (OFF_GLA_K, gkk), (OFF_ML_Q, mq), (OFF_ML_K, mk)):
        moves += [(dst + h * DKP, src + h * DK, DK) for h in range(HEADS)]
    moves += [(OFF_SGU_U, su, SGU_W), (OFF_SGU_V, sv, SGU_W), (OFF_GLA_V, gv, V_W), (OFF_GLA_G, gg, V_W),
              (OFF_ML_V, mv, V_W), (OFF_ML_O, mo, V_W), (OFF_GLA_A1, ga1, GATE_RANK),
              (OFF_ML_IF, mi, 2 * HEADS)]
    assert mf == mi + HEADS and starts[-1] == sum(sizes)
    return moves


def _relayout_kernel(w_ref, o_ref):
    o_ref[...] = jnp.zeros_like(o_ref)
    for dst, src, n in _column_moves():
        if n % 16:
            pad = jnp.zeros((16 - n % 16, w_ref.shape[1]), F32)
            o_ref[dst:dst + n + pad.shape[0], :] = jnp.concatenate([w_ref[src:src + n, :], pad]).astype(BF16)
        else:
            o_ref[dst:dst + n, :] = w_ref[src:src + n, :].astype(BF16)


def _relayout_w_in(w_in, cols=256):
    w_t = jnp.swapaxes(w_in, 1, 2)
    depth, p, d = w_t.shape
    return pl.pallas_call(
        _relayout_kernel,
        grid=(depth, d // cols),
        in_specs=[pl.BlockSpec((None, p, cols), lambda l, i: (l, 0, i))],
        out_specs=pl.BlockSpec((None, PROJ_W, cols), lambda l, i: (l, 0, i)),
        out_shape=jax.ShapeDtypeStruct((depth, PROJ_W, d), BF16),
        compiler_params=pltpu.CompilerParams(
            dimension_semantics=("parallel", "parallel"), vmem_limit_bytes=VMEM_LIMIT),
        name="relayout",
    )(w_t)


def _pad_heads(a):
    lead = a.shape[:-1]
    a = a.reshape(lead + (HEADS, DK))
    a = jnp.pad(a, [(0, 0)] * len(lead) + [(0, 0), (0, DKP - DK)])
    return a.reshape(lead + (QK_W,))


def _pair_level_matrix():
    c = CHUNK
    t = np.arange(c)[:, None]
    s = np.arange(c)[None, :]
    x = np.bitwise_xor(t, s)
    lvl = np.floor(np.log2(np.maximum(x, 1))).astype(np.int32)
    lvl = np.where(s == t, -1, lvl)
    lvl = np.where(s > t, -2, lvl)
    return lvl.astype(np.int32)


def _dot(a, b):
    return jnp.dot(a, b, preferred_element_type=F32)


def _dot_nt(a, b):
    return lax.dot_general(a, b, (((1,), (1,)), ((), ())), preferred_element_type=F32)


def _split3(x):
    hi = x.astype(BF16)
    r = x - hi.astype(F32)
    mid = r.astype(BF16)
    lo = (r - mid.astype(F32)).astype(BF16)
    return hi, mid, lo


LOG2E = 1.4426950408889634
LN2 = 0.6931471805599453


def _log2_sigmoid(x):
    xl = x * LOG2E
    return jnp.minimum(xl, 0.0) - jnp.log2(1.0 + jnp.exp2(-jnp.abs(xl)))


def _log_sigmoid(x):
    return _log2_sigmoid(x) * LN2


def _pair_cols(c0, c1, rows):
    first = lax.broadcasted_iota(jnp.int32, (rows, LANE), 1) < DV - LANE
    b0 = jnp.broadcast_to(c0, (rows, LANE))
    b1 = jnp.broadcast_to(c1, (rows, LANE))
    return jnp.concatenate([b0, jnp.where(first, b0, b1), b1], axis=1)


def _head_rms_scale(o):
    rows = o.shape[0]
    o2 = o * o
    first = lax.broadcasted_iota(jnp.int32, (rows, LANE), 1) < DV - LANE
    mid = o2[:, LANE:2 * LANE]
    ss0 = jnp.sum(o2[:, 0:LANE] + jnp.where(first, mid, 0.0), axis=-1, keepdims=True)
    ss1 = jnp.sum(o2[:, 2 * LANE:3 * LANE] + jnp.where(first, 0.0, mid), axis=-1, keepdims=True)
    return _pair_cols(lax.rsqrt(ss0 / DV + EPS), lax.rsqrt(ss1 / DV + EPS), rows)


def _cast_plan(w, layer, steps):
    rows, cols = w.shape[1], w.shape[2]
    nblk = max(n for n in range(1, steps + 1) if rows % n == 0 and (rows // n) % 16 == 0)
    return w, layer, rows, cols, nblk


def _cast_specs(plans, step_of):
    in_specs, out_specs, out_shapes, args = [], [], [], []
    for w, layer, rows, cols, nblk in plans:
        blk = lambda *g, nblk=nblk: jnp.minimum(step_of(*g), nblk - 1)
        in_specs.append(pl.BlockSpec((None, rows // nblk, cols), lambda *g, blk=blk, layer=layer: (layer, blk(*g), 0)))
        out_specs.append(pl.BlockSpec((rows // nblk, cols), lambda *g, blk=blk: (blk(*g), 0)))
        out_shapes.append(jax.ShapeDtypeStruct((rows, cols), BF16))
        args.append(w)
    return in_specs, out_specs, out_shapes, args


def _run_casts(step, plans, src_refs, dst_refs):
    for (_, _, _, _, nblk), src, dst in zip(plans, src_refs, dst_refs):
        @pl.when(step < nblk)
        def _(src=src, dst=dst):
            dst[...] = src[...].astype(BF16)


def _in_proj_kernel(x_ref, g_ref, w_ref, wg_ref, *rest, plans):
    n = len(plans)
    o_ref, og_ref, h_ref = rest[n], rest[n + 1], rest[2 * n + 2]
    _run_casts(pl.program_id(0) * pl.num_programs(1) + pl.program_id(1), plans,
               rest[0:n], rest[n + 2:2 * n + 2])

    @pl.when(pl.program_id(1) == 0)
    def _():
        x = x_ref[...]
        ms = jnp.mean(x * x, axis=-1, keepdims=True)
        h_ref[...] = (x * lax.rsqrt(ms + EPS) * g_ref[...]).astype(BF16)
        og_ref[...] = _dot_nt(h_ref[...], wg_ref[...])

    o_ref[...] = _dot_nt(h_ref[...], w_ref[...]).astype(o_ref.dtype)


def _in_proj(x2, g, w, layer, tm, tn, casts=()):
    t, d = x2.shape
    ni, nj = t // tm, MAIN_W // tn
    plans = [_cast_plan(cw, cl, ni * nj) for cw, cl in casts]
    c_in, c_out, c_shapes, c_args = _cast_specs(plans, lambda i, j: i * nj + j)
    return pl.pallas_call(
        functools.partial(_in_proj_kernel, plans=plans),
        grid=(ni, nj),
        in_specs=[
            pl.BlockSpec((tm, d), lambda i, j: (i, 0)),
            pl.BlockSpec((1, d), lambda i, j: (0, 0)),
            pl.BlockSpec((None, tn, d), lambda i, j: (layer, j, 0)),
            pl.BlockSpec((None, GATE_W, d), lambda i, j: (layer, MAIN_W // GATE_W, 0)),
        ] + c_in,
        out_specs=[pl.BlockSpec((tm, tn), lambda i, j: (i, j)),
                   pl.BlockSpec((tm, GATE_W), lambda i, j: (i, 0))] + c_out,
        out_shape=[jax.ShapeDtypeStruct((t, MAIN_W), BF16),
                   jax.ShapeDtypeStruct((t, GATE_W), F32)] + c_shapes,
        scratch_shapes=[pltpu.VMEM((tm, d), BF16)],
        compiler_params=pltpu.CompilerParams(
            dimension_semantics=("arbitrary", "arbitrary"), vmem_limit_bytes=VMEM_LIMIT),
        name="in_proj",
    )(x2, g, w, w, *c_args)


def _gla_kernel(qk_ref, vg_ref, a1_ref, a2_ref, ab_ref, nrm_ref, ltri_ref, lvl_ref,
                o_ref, s_ref, cum_ref, la_ref):
    s_ref[...] = jnp.zeros_like(s_ref)
    la_ref[0:8, :] = jnp.zeros((8, QK_W), F32)
    la_ref[8 + CHUNK:16 + CHUNK, :] = jnp.zeros((8, QK_W), F32)

    def chunk(c, carry):
        rows = pl.ds(pl.multiple_of(c * CHUNK, CHUNK), CHUNK)
        _gla_chunk(rows, qk_ref, vg_ref, a1_ref, a2_ref, ab_ref, nrm_ref, ltri_ref, lvl_ref,
                   o_ref, s_ref, cum_ref, la_ref)
        return carry

    lax.fori_loop(0, qk_ref.shape[0] // CHUNK, chunk, 0)


def _gla_chunk(rows, qk_ref, vg_ref, a1_ref, a2_ref, ab_ref, nrm_ref, ltri_ref, lvl_ref,
               o_ref, s_ref, cum_ref, la_ref):
    q = qk_ref[rows, 0:QK_W].astype(F32) * (DK ** -0.5)
    k = qk_ref[rows, QK_W:2 * QK_W].astype(F32)
    z = _dot(a1_ref[rows, :].astype(BF16), a2_ref[...]) + ab_ref[...]
    log_a = _log2_sigmoid(z) * (1.0 / GATE_TAU)
    la_ref[8:8 + CHUNK, :] = log_a
    ltri = ltri_ref[...]
    hi, mid, lo = _split3(log_a)
    cum = _dot(ltri, hi) + _dot(ltri, mid) + _dot(ltri, lo)
    cum_ref[...] = cum

    lvl = lvl_ref[...]
    groups = CHUNK // 8
    rows8 = lambda g: slice(8 * g, 8 * g + 8)
    heads = [slice(h * DKP, (h + 1) * DKP) for h in range(HEADS)]
    row = lax.broadcasted_iota(jnp.int32, (CHUNK, QK_W), 0)

    def scores(xq, yk):
        return [_dot_nt(xq[:, hs], yk[:, hs]) for hs in heads]

    qb = q.astype(BF16)
    kb = k.astype(BF16)
    attn = [[jnp.where(lvl[rows8(g), :] == -1, p[rows8(g), :], 0.0) for g in range(groups)]
            for p in scores(qb, kb)]

    def merge(level, parts, q_groups):
        for h in range(HEADS):
            for i, g in enumerate(q_groups):
                attn[h][g] = jnp.where(lvl[rows8(g), :] == level, parts[h][rows8(i), :], attn[h][g])

    e = jnp.exp2(jnp.where((row & 1) == 1, log_a, 0.0))
    merge(0, scores((q * e).astype(BF16), kb), range(groups))
    nxt = la_ref[pl.ds(9, CHUNK), :]
    prv = la_ref[pl.ds(7, CHUNK), :]
    r4 = row & 3
    e = jnp.exp2(jnp.where(r4 == 0, nxt, jnp.where(r4 == 1, 0.0, jnp.where(r4 == 2, log_a, log_a + prv))))
    merge(1, scores((q * e).astype(BF16), (k * e).astype(BF16)), range(groups))
    sub8 = lax.broadcasted_iota(jnp.int32, (8, QK_W), 0)
    pieces = []
    for g in range(groups):
        d = cum[rows8(g), :] - cum_ref[8 * g + 3:8 * g + 4, :]
        pieces.append(jnp.where(sub8 < 4, -d, d))
    e = jnp.exp2(jnp.concatenate(pieces, axis=0))
    merge(2, scores((q * e).astype(BF16), (k * e).astype(BF16)), range(groups))
    for level in range(3, N_LEVELS):
        m = 1 << level
        xq, yk, q_groups = [], [], []
        for base in range(0, CHUNK, 2 * m):
            k_rows = slice(base, base + m)
            q_rows = slice(base + m, base + 2 * m)
            edge = cum_ref[base + m - 1:base + m, :]
            yk += [k[k_rows, :] * jnp.exp2(edge - cum[k_rows, :]), k[q_rows, :]]
            xq.append(q[q_rows, :] * jnp.exp2(cum[q_rows, :] - edge))
            q_groups += range((base + m) // 8, (base + 2 * m) // 8)
        merge(level, scores(jnp.concatenate(xq, axis=0).astype(BF16),
                            jnp.concatenate(yk, axis=0).astype(BF16)), q_groups)

    last = cum_ref[CHUNK - 1:CHUNK, :]
    q_dec = (q * jnp.exp2(cum)).astype(BF16)
    k_dec = k * jnp.exp2(last - cum)
    decay_all = jnp.exp2(last)

    lane = lax.broadcasted_iota(jnp.int32, (CHUNK, PAIR_V), 1)
    lo_mask = lane < DV

    for p in range(HEADS // 2):
        ks = slice(p * PAIR_K, (p + 1) * PAIR_K)
        vs = slice(p * PAIR_V, (p + 1) * PAIR_V)
        vb = vg_ref[rows, vs]
        zero = jnp.zeros_like(vb)
        v_lo = jnp.where(lo_mask, vb, zero)
        v_hi = jnp.where(lo_mask, zero, vb)
        v_blk = jnp.concatenate([v_lo, v_hi], axis=0)
        state = s_ref[p]
        lhs = jnp.concatenate([jnp.concatenate(attn[2 * p], axis=0).astype(BF16),
                               jnp.concatenate(attn[2 * p + 1], axis=0).astype(BF16),
                               q_dec[:, ks]], axis=1)
        rhs = jnp.concatenate([v_blk, state.astype(BF16)], axis=0)
        o = _dot(lhs, rhs)

        kd_t = k_dec[:, ks].T.astype(BF16)
        upd = jnp.concatenate([_dot(kd_t[0:DKP, :], v_lo), _dot(kd_t[DKP:PAIR_K, :], v_hi)], axis=0)
        dcols = []
        for h in (2 * p, 2 * p + 1):
            d_row = jnp.broadcast_to(decay_all[:, h * DKP:(h + 1) * DKP], (DKP, DKP))
            d_col = d_row.T
            dcols.append(jnp.concatenate([d_col] * (PAIR_V // LANE), axis=1))
        s_ref[p] = state * jnp.concatenate(dcols, axis=0) + upd

        gate = vg_ref[rows, V_W + p * PAIR_V:V_W + (p + 1) * PAIR_V].astype(F32)
        out = o * _head_rms_scale(o) * nrm_ref[:, vs] * (gate * jax.nn.sigmoid(gate))
        o_ref[rows, vs] = out.astype(o_ref.dtype)


def _gla(proj, gates, a2p, abp, nrm, ltri, lvl, batch, seq):
    tok = lambda w, cb: pl.BlockSpec((seq, w), lambda b: (b, cb))
    const = lambda shape: pl.BlockSpec(shape, lambda b: (0,) * len(shape))
    assert OFF_GLA_K == OFF_GLA_Q + QK_W and OFF_GLA_G == OFF_GLA_V + V_W
    return pl.pallas_call(
        _gla_kernel,
        grid=(batch,),
        in_specs=[
            tok(2 * QK_W, OFF_GLA_Q // (2 * QK_W)),
            tok(2 * V_W, OFF_GLA_V // (2 * V_W)),
            tok(LANE, (OFF_GLA_A1 - MAIN_W) // LANE),
            const((LANE, QK_W)),
            const((1, QK_W)),
            const((1, V_W)),
            const((CHUNK, CHUNK)),
            const((CHUNK, CHUNK)),
        ],
        out_specs=pl.BlockSpec((seq, V_W), lambda b: (b, 0)),
        out_shape=jax.ShapeDtypeStruct((batch * seq, V_W), BF16),
        scratch_shapes=[
            pltpu.VMEM((HEADS // 2, PAIR_K, PAIR_V), F32),
            pltpu.VMEM((CHUNK, QK_W), F32),
            pltpu.VMEM((CHUNK + 16, QK_W), F32),
        ],
        compiler_params=pltpu.CompilerParams(
            dimension_semantics=("parallel",), vmem_limit_bytes=VMEM_LIMIT),
        name="gla",
    )(proj, proj, gates, a2p, abp, nrm, ltri, lvl)


ST_W = PAIR_V + LANE
TAIL = 8


def _mlstm_kernel(qk_ref, vo_ref, if_ref, cw_ref, ifb_ref, nrm_ref, ltri_ref, lvl_ref,
                  o_ref, c_ref, m_ref, xe_ref):
    c_ref[...] = jnp.zeros_like(c_ref)
    m_ref[...] = jnp.zeros_like(m_ref)
    xe_ref[0:TAIL, :] = jnp.zeros((TAIL, 2 * QK_W), F32)

    def chunk(c, carry):
        rows = pl.ds(pl.multiple_of(c * CHUNK, CHUNK), CHUNK)
        _mlstm_chunk(rows, qk_ref, vo_ref, if_ref, cw_ref, ifb_ref, nrm_ref, ltri_ref, lvl_ref,
                     o_ref, c_ref, m_ref, xe_ref)
        return carry

    lax.fori_loop(0, qk_ref.shape[0] // CHUNK, chunk, 0, unroll=2)


def _mlstm_chunk(rows, qk_ref, vo_ref, if_ref, cw_ref, ifb_ref, nrm_ref, ltri_ref, lvl_ref,
                 o_ref, c_ref, m_ref, xe_ref):
    xe_ref[TAIL:TAIL + CHUNK, :] = qk_ref[rows, :].astype(F32)
    y = jnp.zeros((CHUNK, 2 * QK_W), F32)
    for j in range(CONV_WIDTH):
        y = y + cw_ref[j:j + 1, :] * xe_ref[pl.ds(TAIL - (CONV_WIDTH - 1) + j, CHUNK), :]
    xe_ref[0:TAIL, :] = xe_ref[CHUNK:CHUNK + TAIL, :]
    y = y * jax.nn.sigmoid(y)
    qm = y[:, 0:QK_W]
    km = y[:, QK_W:2 * QK_W] * (DK ** -0.5)
    qb = qm.astype(BF16)
    kb = km.astype(BF16)
    km_t = km.T

    slab = if_ref[rows, :] + ifb_ref[...]
    lane_g = lax.broadcasted_iota(jnp.int32, (CHUNK, LANE), 1)
    gates = jnp.where(lane_g < HEADS, slab, _log_sigmoid(slab))
    ltri = ltri_ref[...]
    g_hi, g_mid, g_lo = _split3(gates)
    cum_col = _dot(ltri, g_hi) + _dot(ltri, g_mid) + _dot(ltri, g_lo)
    gates_t = gates.T[0:2 * HEADS, :]
    t_hi, t_mid, t_lo = _split3(gates_t)
    cum_row = _dot_nt(t_hi, ltri) + _dot_nt(t_mid, ltri) + _dot_nt(t_lo, ltri)

    lvl = lvl_ref[...]
    causal = lvl >= -1
    lane = lax.broadcasted_iota(jnp.int32, (CHUNK, PAIR_V), 1)
    lo_mask = lane < DV
    lane_e = lax.broadcasted_iota(jnp.int32, (CHUNK, LANE), 1)
    one_col = [jnp.where(lane_e == i, 1.0, 0.0).astype(BF16) for i in range(2)]
    lane_r = lax.broadcasted_iota(jnp.int32, (PAIR_K, LANE), 1)
    row_r = lax.broadcasted_iota(jnp.int32, (PAIR_K, LANE), 0)
    ones_blk = jnp.where(lane_r == jnp.where(row_r < CHUNK, 0, 1), 1.0, 0.0).astype(BF16)

    for p in range(HEADS // 2):
        ks = slice(p * PAIR_K, (p + 1) * PAIR_K)
        vs = slice(p * PAIR_V, (p + 1) * PAIR_V)
        vb = vo_ref[rows, vs]
        zero = jnp.zeros_like(vb)
        sc_parts, qs_parts, floor_parts, kw_parts, dprev_parts = [], [], [], [], []
        for h in (2 * p, 2 * p + 1):
            hs = slice(h * DKP, (h + 1) * DKP)
            cc = jnp.broadcast_to(cum_col[:, HEADS + h:HEADS + h + 1], (CHUNK, CHUNK))
            cum_r = cum_row[HEADS + h:HEADS + h + 1, :]
            ib_r = gates_t[h:h + 1, :]
            m_prev = m_ref[h:h + 1, :]
            dmat = jnp.where(causal, cc - cum_r + ib_r, -jnp.inf)
            inter = cc + m_prev
            m_t = jnp.maximum(inter, jnp.max(dmat, axis=-1, keepdims=True))
            w = jnp.exp(dmat - m_t)
            sc_inter = jnp.exp(inter - m_t)
            sc_parts.append((_dot_nt(qb[:, hs], kb[:, hs]) * w).astype(BF16))
            qs_parts.append((qm[:, hs] * sc_inter).astype(BF16))
            floor_parts.append(jnp.exp(-m_t))
            total = cum_r[:, CHUNK - 1:CHUNK]
            g_row = total - cum_r + ib_r
            m_new = jnp.maximum(total + m_prev, jnp.max(g_row, axis=-1, keepdims=True))
            wj = jnp.exp(g_row - m_new)
            dprev_parts.append(jnp.exp(total + m_prev - m_new))
            kw_parts.append((km_t[hs, :] * wj).astype(BF16))
            m_ref[h:h + 1, :] = m_new

        state = c_ref[p]
        st_m = state.astype(BF16)
        v_lo = jnp.where(lo_mask, vb, zero)
        v_hi = jnp.where(lo_mask, zero, vb)
        v_blk = jnp.concatenate([v_lo, v_hi], axis=0)
        rhs = jnp.concatenate([jnp.concatenate([v_blk, ones_blk], axis=1), st_m], axis=0)
        lhs = jnp.concatenate(sc_parts + qs_parts, axis=1)
        res = _dot(lhs, rhs)
        num = res[:, 0:PAIR_V]
        den = _pair_cols(res[:, PAIR_V:PAIR_V + 1], res[:, PAIR_V + 1:PAIR_V + 2], CHUNK)
        floor = _pair_cols(floor_parts[0][:, 0:1], floor_parts[1][:, 0:1], CHUNK)
        hid = num / jnp.maximum(jnp.abs(den), floor)

        upd = jnp.concatenate(
            [_dot(kw_parts[0], jnp.concatenate([v_lo, one_col[0]], axis=1)),
             _dot(kw_parts[1], jnp.concatenate([v_hi, one_col[1]], axis=1))], axis=0)
        d_rows = jnp.concatenate(
            [jnp.broadcast_to(jnp.concatenate([d] * (ST_W // LANE), axis=1), (DKP, ST_W))
             for d in dprev_parts], axis=0)
        c_ref[p] = d_rows * state + upd

        og = vo_ref[rows, V_W + p * PAIR_V:V_W + (p + 1) * PAIR_V].astype(F32)
        out = hid * _head_rms_scale(hid) * nrm_ref[:, vs] * jax.nn.sigmoid(og)
        o_ref[rows, vs] = out.astype(o_ref.dtype)


def _mlstm(proj, gates, cw, ifb, nrm, ltri, lvl, batch, seq):
    tok = lambda w, cb: pl.BlockSpec((seq, w), lambda b: (b, cb))
    const = lambda shape: pl.BlockSpec(shape, lambda b: (0,) * len(shape))
    assert OFF_ML_K == OFF_ML_Q + QK_W and OFF_ML_O == OFF_ML_V + V_W
    return pl.pallas_call(
        _mlstm_kernel,
        grid=(batch,),
        in_specs=[
            tok(2 * QK_W, OFF_ML_Q // (2 * QK_W)),
            tok(2 * V_W, OFF_ML_V // (2 * V_W)),
            tok(LANE, (OFF_ML_IF - MAIN_W) // LANE),
            const((CONV_WIDTH, 2 * QK_W)),
            const((1, LANE)),
            const((1, V_W)),
            const((CHUNK, CHUNK)),
            const((CHUNK, CHUNK)),
        ],
        out_specs=pl.BlockSpec((seq, V_W), lambda b: (b, 0)),
        out_shape=jax.ShapeDtypeStruct((batch * seq, V_W), BF16),
        scratch_shapes=[
            pltpu.VMEM((HEADS // 2, PAIR_K, ST_W), F32),
            pltpu.VMEM((2 * HEADS, LANE), F32),
            pltpu.VMEM((TAIL + CHUNK, 2 * QK_W), F32),
        ],
        compiler_params=pltpu.CompilerParams(
            dimension_semantics=("parallel",), vmem_limit_bytes=VMEM_LIMIT),
        name="mlstm",
    )(proj, proj, gates, cw, ifb, nrm, ltri, lvl)


def _sgu_kernel(u_ref, v_ref, lng_ref, lnb_ref, w_ref, b_ref, lvl_ref, o_ref):
    causal = lvl_ref[...] >= -1
    w_causal = [jnp.where(causal, w_ref[g], 0.0).astype(BF16) for g in range(SGU_GROUPS)]
    for n in range(u_ref.shape[0] // CHUNK):
        rows = slice(n * CHUNK, (n + 1) * CHUNK)
        u = jax.nn.gelu(u_ref[rows, :].astype(F32))
        v = jax.nn.gelu(v_ref[rows, :].astype(F32))
        mu = jnp.mean(v, axis=-1, keepdims=True)
        var = jnp.mean(jnp.square(v - mu), axis=-1, keepdims=True)
        vn = ((v - mu) * lax.rsqrt(var + EPS) * lng_ref[...] + lnb_ref[...]).astype(BF16)
        for g in range(SGU_GROUPS):
            gs = slice(g * SGU_CH, (g + 1) * SGU_CH)
            mixed = _dot(w_causal[g], vn[:, gs]) + b_ref[:, gs]
            o_ref[rows, gs] = (u[:, gs] * mixed).astype(o_ref.dtype)


def _sgu(proj, lng, lnb, w, b_full, lvl, batch, seq):
    step = CHUNK * SGU_STEP
    while (batch * seq) % step:
        step //= 2
    nb = batch * seq // step
    tok = lambda w_, cb: pl.BlockSpec((step, w_), lambda i: (i, cb))
    const = lambda shape: pl.BlockSpec(shape, lambda i: (0,) * len(shape))
    return pl.pallas_call(
        _sgu_kernel,
        grid=(nb,),
        in_specs=[
            tok(SGU_W, OFF_SGU_U // SGU_W),
            tok(SGU_W, OFF_SGU_V // SGU_W),
            const((1, SGU_W)),
            const((1, SGU_W)),
            const((SGU_GROUPS, CHUNK, CHUNK)),
            const((CHUNK, SGU_W)),
            const((CHUNK, CHUNK)),
        ],
        out_specs=pl.BlockSpec((step, SGU_W), lambda i: (i, 0)),
        out_shape=jax.ShapeDtypeStruct((batch * seq, SGU_W), BF16),
        compiler_params=pltpu.CompilerParams(
            dimension_semantics=("parallel",), vmem_limit_bytes=VMEM_LIMIT),
        name="sgu",
    )(proj, proj, lng, lnb, w, b_full, lvl)


def _out_proj_kernel(x_ref, a_ref, b_ref, c_ref, wa_ref, wb_ref, wc_ref, o_ref):
    o_ref[...] = (x_ref[...] + _dot(a_ref[...], wa_ref[...]) + _dot(b_ref[...], wb_ref[...])
                  + _dot(c_ref[...], wc_ref[...]))


def _out_proj(x2, mix_a, mix_b, mix_c, w_out, tm, tn):
    t, d = x2.shape
    return pl.pallas_call(
        _out_proj_kernel,
        grid=(d // tn, t // tm),
        in_specs=[
            pl.BlockSpec((tm, tn), lambda j, i: (i, j)),
            pl.BlockSpec((tm, V_W), lambda j, i: (i, 0)),
            pl.BlockSpec((tm, V_W), lambda j, i: (i, 0)),
            pl.BlockSpec((tm, SGU_W), lambda j, i: (i, 0)),
            pl.BlockSpec((V_W, tn), lambda j, i: (0, j)),
            pl.BlockSpec((V_W, tn), lambda j, i: (1, j)),
            pl.BlockSpec((SGU_W, tn), lambda j, i: (2 * V_W // SGU_W, j)),
        ],
        out_specs=pl.BlockSpec((tm, tn), lambda j, i: (i, j)),
        out_shape=jax.ShapeDtypeStruct((t, d), F32),
        compiler_params=pltpu.CompilerParams(
            dimension_semantics=("parallel", "arbitrary"), vmem_limit_bytes=VMEM_LIMIT),
        name="out_proj",
    )(x2, mix_a, mix_b, mix_c, w_out, w_out, w_out)


def _ffn_kernel(x_ref, g_ref, wg_ref, wu_ref, wd_ref, gf_ref, *rest, final_norm, plans):
    n = len(plans)
    o_ref, h_ref = rest[n], rest[2 * n + 1]
    j = pl.program_id(1)
    _run_casts(pl.program_id(0) * pl.num_programs(1) + j, plans, rest[0:n], rest[n + 1:2 * n + 1])

    @pl.when(j == 0)
    def _():
        x = x_ref[...]
        ms = jnp.mean(x * x, axis=-1, keepdims=True)
        h_ref[...] = (x * lax.rsqrt(ms + EPS) * g_ref[...]).astype(BF16)
        o_ref[...] = x

    h = h_ref[...]
    gate = _dot(h, wg_ref[...])
    up = _dot(h, wu_ref[...])
    act = (gate * jax.nn.sigmoid(gate) * up).astype(BF16)
    o_ref[...] += _dot(act, wd_ref[...])

    if final_norm:
        @pl.when(j == pl.num_programs(1) - 1)
        def _():
            y = o_ref[...]
            ms = jnp.mean(y * y, axis=-1, keepdims=True)
            o_ref[...] = y * lax.rsqrt(ms + EPS) * gf_ref[...]


def _ffn(x2, g, w_gu, w_down, g_final, tm, tf, final_norm, casts=()):
    t, d = x2.shape
    nf = D_FF // tf
    plans = [_cast_plan(cw, cl, (t // tm) * nf) for cw, cl in casts]
    c_in, c_out, c_shapes, c_args = _cast_specs(plans, lambda i, j: i * nf + j)
    return pl.pallas_call(
        functools.partial(_ffn_kernel, final_norm=final_norm, plans=plans),
        grid=(t // tm, nf),
        in_specs=[
            pl.BlockSpec((tm, d), lambda i, j: (i, 0)),
            pl.BlockSpec((1, d), lambda i, j: (0, 0)),
            pl.BlockSpec((d, tf), lambda i, j: (0, j)),
            pl.BlockSpec((d, tf), lambda i, j: (0, nf + j)),
            pl.BlockSpec((tf, d), lambda i, j: (j, 0)),
            pl.BlockSpec((1, d), lambda i, j: (0, 0)),
        ] + c_in,
        out_specs=[pl.BlockSpec((tm, d), lambda i, j: (i, 0))] + c_out,
        out_shape=[jax.ShapeDtypeStruct((t, d), F32)] + c_shapes,
        scratch_shapes=[pltpu.VMEM((tm, d), BF16)],
        compiler_params=pltpu.CompilerParams(
            dimension_semantics=("arbitrary", "arbitrary"), vmem_limit_bytes=VMEM_LIMIT),
        name="ffn",
    )(x2, g, w_gu, w_gu, w_down, g_final, *c_args)


def _tiles(tokens):
    tm = 1024
    while tokens % tm:
        tm //= 2
    return tm


def kernel(x, norm_mix, w_in, gla_a2, gla_ab, gla_norm, ml_conv, ml_ib, ml_fb, ml_norm,
           sgu_ln_g, sgu_ln_b, sgu_w, sgu_b, w_out, norm_ffn, w_gu, w_down, norm_final):
    batch, seq, d = x.shape
    depth = w_in.shape[0]
    tokens = batch * seq
    tm = _tiles(tokens)

    lvl = jnp.asarray(_pair_level_matrix())
    ltri = jnp.asarray(np.tril(np.ones((CHUNK, CHUNK), np.float32)), BF16)

    w_in_b = _relayout_w_in(w_in)
    dense = (w_out, w_gu, w_down)

    xc = x.reshape(tokens, d)
    for l in range(depth):
        if l == 0:
            proj, gates, w_out_b, w_gu_b, w_down_b = _in_proj(
                xc, norm_mix[l][None, :], w_in_b, l, tm, 1536, casts=[(w, 0) for w in dense])
        else:
            proj, gates = _in_proj(xc, norm_mix[l][None, :], w_in_b, l, tm, 1536)

        a2p = jnp.pad(_pad_heads(gla_a2[l]), ((0, LANE - GATE_RANK), (0, 0))).astype(BF16)
        abp = _pad_heads(gla_ab[l])[None, :]
        mix_a = _gla(proj, gates, a2p, abp, gla_norm[l][None, :], ltri, lvl, batch, seq)

        cw = jnp.concatenate([_pad_heads(ml_conv[l][:, :HEADS * DK]),
                              _pad_heads(ml_conv[l][:, HEADS * DK:])], axis=1)
        ifb = jnp.pad(jnp.concatenate([ml_ib[l], ml_fb[l]]), (0, LANE - 2 * HEADS))[None, :]
        mix_b = _mlstm(proj, gates, cw, ifb, ml_norm[l][None, :], ltri, lvl, batch, seq)

        b_full = jnp.repeat(sgu_b[l].T, SGU_CH, axis=1)
        mix_c = _sgu(proj, sgu_ln_g[l][None, :], sgu_ln_b[l][None, :], sgu_w[l], b_full, lvl, batch, seq)

        x1 = _out_proj(xc, mix_a, mix_b, mix_c, w_out_b, tm // 2, d)
        nxt = [(w, l + 1) for w in dense] if l + 1 < depth else []
        xc, *cast = _ffn(x1, norm_ffn[l][None, :], w_gu_b, w_down_b, norm_final[None, :], tm, 512,
                         final_norm=(l == depth - 1), casts=nxt)
        if cast:
            w_out_b, w_gu_b, w_down_b = cast
    return xc.reshape(batch, seq, d)
```

```python
import functools

import numpy as np
import jax
import jax.numpy as jnp
from jax import lax
from jax.experimental import pallas as pl
from jax.experimental.pallas import tpu as pltpu

F32 = jnp.float32
BF16 = jnp.bfloat16

D_MODEL = 2048
HEADS = 4
DK = 96
DKP = 128
DV = 192
QK_W = HEADS * DKP
V_W = HEADS * DV
PAIR_K = 2 * DKP
PAIR_V = 2 * DV
SGU_W = 512
SGU_GROUPS = 4
SGU_CH = 128
GATE_RANK = 16
GATE_TAU = 16.0
CONV_WIDTH = 4
D_FF = 5632
EPS = 1e-6
CHUNK = 128
N_LEVELS = 7
LANE = 128

OFF_GLA_Q, OFF_GLA_K, OFF_ML_Q, OFF_ML_K = 0, 512, 1024, 1536
OFF_SGU_U, OFF_SGU_V = 2048, 2560
OFF_GLA_V, OFF_GLA_G, OFF_ML_V, OFF_ML_O = 3072, 3840, 4608, 5376
MAIN_W = 6144
OFF_GLA_A1, OFF_ML_IF = 6144, 6272
GATE_W = 2 * LANE
PROJ_W = MAIN_W + GATE_W
SGU_STEP = 4

VMEM_LIMIT = 58 * 1024 * 1024


def _column_moves():
    gk = HEADS * DK
    sizes = (gk, gk, V_W, V_W, GATE_RANK, gk, gk, V_W, V_W, HEADS, HEADS, SGU_W, SGU_W)
    starts = np.concatenate([[0], np.cumsum(sizes)]).tolist()
    (gq, gkk, gv, gg, ga1, mq, mk, mv, mo, mi, mf, su, sv) = starts[:-1]
    moves = []
    for dst, src in ((OFF_GLA_Q, gq), (OFF_GLA_K, gkk), (OFF_ML_Q, mq), (OFF_ML_K, mk)):
        moves += [(dst + h * DKP, src + h * DK, DK) for h in range(HEADS)]
    moves += [(OFF_SGU_U, su, SGU_W), (OFF_SGU_V, sv, SGU_W), (OFF_GLA_V, gv, V_W), (OFF_GLA_G, gg, V_W),
              (OFF_ML_V, mv, V_W), (OFF_ML_O, mo, V_W), (OFF_GLA_A1, ga1, GATE_RANK),
              (OFF_ML_IF, mi, 2 * HEADS)]
    assert mf == mi + HEADS and starts[-1] == sum(sizes)
    return moves


def _relayout_kernel(w_ref, o_ref):
    o_ref[...] = jnp.zeros_like(o_ref)
    for dst, src, n in _column_moves():
        if n % 16:
            pad = jnp.zeros((16 - n % 16, w_ref.shape[1]), F32)
            o_ref[dst:dst + n + pad.shape[0], :] = jnp.concatenate([w_ref[src:src + n, :], pad]).astype(BF16)
        else:
            o_ref[dst:dst + n, :] = w_ref[src:src + n, :].astype(BF16)


def _relayout_w_in(w_in, cols=256):
    w_t = jnp.swapaxes(w_in, 1, 2)
    depth, p, d = w_t.shape
    return pl.pallas_call(
        _relayout_kernel,
        grid=(depth, d // cols),
        in_specs=[pl.BlockSpec((None, p, cols), lambda l, i: (l, 0, i))],
        out_specs=pl.BlockSpec((None, PROJ_W, cols), lambda l, i: (l, 0, i)),
        out_shape=jax.ShapeDtypeStruct((depth, PROJ_W, d), BF16),
        compiler_params=pltpu.CompilerParams(
            dimension_semantics=("parallel", "parallel"), vmem_limit_bytes=VMEM_LIMIT),
        name="relayout",
    )(w_t)


def _pad_heads(a):
    lead = a.shape[:-1]
    a = a.reshape(lead + (HEADS, DK))
    a = jnp.pad(a, [(0, 0)] * len(lead) + [(0, 0), (0, DKP - DK)])
    return a.reshape(lead + (QK_W,))


def _pair_level_matrix():
    c = CHUNK
    t = np.arange(c)[:, None]
    s = np.arange(c)[None, :]
    x = np.bitwise_xor(t, s)
    lvl = np.floor(np.log2(np.maximum(x, 1))).astype(np.int32)
    lvl = np.where(s == t, -1, lvl)
    lvl = np.where(s > t, -2, lvl)
    return lvl.astype(np.int32)


def _dot(a, b):
    return jnp.dot(a, b, preferred_element_type=F32)


def _dot_nt(a, b):
    return lax.dot_general(a, b, (((1,), (1,)), ((), ())), preferred_element_type=F32)


def _split3(x):
    hi = x.astype(BF16)
    r = x - hi.astype(F32)
    mid = r.astype(BF16)
    lo = (r - mid.astype(F32)).astype(BF16)
    return hi, mid, lo


LOG2E = 1.4426950408889634
LN2 = 0.6931471805599453


def _log2_sigmoid(x):
    xl = x * LOG2E
    return jnp.minimum(xl, 0.0) - jnp.log2(1.0 + jnp.exp2(-jnp.abs(xl)))


def _log_sigmoid(x):
    return _log2_sigmoid(x) * LN2


def _pair_cols(c0, c1, rows):
    first = lax.broadcasted_iota(jnp.int32, (rows, LANE), 1) < DV - LANE
    b0 = jnp.broadcast_to(c0, (rows, LANE))
    b1 = jnp.broadcast_to(c1, (rows, LANE))
    return jnp.concatenate([b0, jnp.where(first, b0, b1), b1], axis=1)


def _head_rms_scale(o):
    rows = o.shape[0]
    o2 = o * o
    first = lax.broadcasted_iota(jnp.int32, (rows, LANE), 1) < DV - LANE
    mid = o2[:, LANE:2 * LANE]
    ss0 = jnp.sum(o2[:, 0:LANE] + jnp.where(first, mid, 0.0), axis=-1, keepdims=True)
    ss1 = jnp.sum(o2[:, 2 * LANE:3 * LANE] + jnp.where(first, 0.0, mid), axis=-1, keepdims=True)
    return _pair_cols(lax.rsqrt(ss0 / DV + EPS), lax.rsqrt(ss1 / DV + EPS), rows)


def _cast_plan(w, layer, steps):
    rows, cols = w.shape[1], w.shape[2]
    nblk = max(n for n in range(1, steps + 1) if rows % n == 0 and (rows // n) % 16 == 0)
    return w, layer, rows, cols, nblk


def _cast_specs(plans, step_of):
    in_specs, out_specs, out_shapes, args = [], [], [], []
    for w, layer, rows, cols, nblk in plans:
        blk = lambda *g, nblk=nblk: jnp.minimum(step_of(*g), nblk - 1)
        in_specs.append(pl.BlockSpec((None, rows // nblk, cols), lambda *g, blk=blk, layer=layer: (layer, blk(*g), 0)))
        out_specs.append(pl.BlockSpec((rows // nblk, cols), lambda *g, blk=blk: (blk(*g), 0)))
        out_shapes.append(jax.ShapeDtypeStruct((rows, cols), BF16))
        args.append(w)
    return in_specs, out_specs, out_shapes, args


def _run_casts(step, plans, src_refs, dst_refs):
    for (_, _, _, _, nblk), src, dst in zip(plans, src_refs, dst_refs):
        @pl.when(step < nblk)
        def _(src=src, dst=dst):
            dst[...] = src[...].astype(BF16)


def _in_proj_kernel(x_ref, g_ref, w_ref, wg_ref, *rest, plans):
    n = len(plans)
    o_ref, og_ref, h_ref = rest[n], rest[n + 1], rest[2 * n + 2]
    _run_casts(pl.program_id(0) * pl.num_programs(1) + pl.program_id(1), plans,
               rest[0:n], rest[n + 2:2 * n + 2])

    @pl.when(pl.program_id(1) == 0)
    def _():
        x = x_ref[...]
        ms = jnp.mean(x * x, axis=-1, keepdims=True)
        h_ref[...] = (x * lax.rsqrt(ms + EPS) * g_ref[...]).astype(BF16)
        og_ref[...] = _dot_nt(h_ref[...], wg_ref[...])

    o_ref[...] = _dot_nt(h_ref[...], w_ref[...]).astype(o_ref.dtype)


def _in_proj(x2, g, w, layer, tm, tn, casts=()):
    t, d = x2.shape
    ni, nj = t // tm, MAIN_W // tn
    plans = [_cast_plan(cw, cl, ni * nj) for cw, cl in casts]
    c_in, c_out, c_shapes, c_args = _cast_specs(plans, lambda i, j: i * nj + j)
    return pl.pallas_call(
        functools.partial(_in_proj_kernel, plans=plans),
        grid=(ni, nj),
        in_specs=[
            pl.BlockSpec((tm, d), lambda i, j: (i, 0)),
            pl.BlockSpec((1, d), lambda i, j: (0, 0)),
            pl.BlockSpec((None, tn, d), lambda i, j: (layer, j, 0)),
            pl.BlockSpec((None, GATE_W, d), lambda i, j: (layer, MAIN_W // GATE_W, 0)),
        ] + c_in,
        out_specs=[pl.BlockSpec((tm, tn), lambda i, j: (i, j)),
                   pl.BlockSpec((tm, GATE_W), lambda i, j: (i, 0))] + c_out,
        out_shape=[jax.ShapeDtypeStruct((t, MAIN_W), BF16),
                   jax.ShapeDtypeStruct((t, GATE_W), F32)] + c_shapes,
        scratch_shapes=[pltpu.VMEM((tm, d), BF16)],
        compiler_params=pltpu.CompilerParams(
            dimension_semantics=("arbitrary", "arbitrary"), vmem_limit_bytes=VMEM_LIMIT),
        name="in_proj",
    )(x2, g, w, w, *c_args)


GLA_NSEQ = 2
GLA_SPLIT = 2


def _gla_kernel(qk_ref, vg_ref, a1_ref, a2_ref, ab_ref, nrm_ref, ltri_ref, lvl_ref,
                o_ref, s_ref, cum_ref, la_ref):
    @pl.when(pl.program_id(1) == 0)
    def _():
        s_ref[...] = jnp.zeros_like(s_ref)
        la_ref[0:8, :] = jnp.zeros((8, la_ref.shape[1]), F32)
        la_ref[8 + CHUNK:16 + CHUNK, :] = jnp.zeros((8, la_ref.shape[1]), F32)

    def chunk(c, carry):
        rows = pl.ds(pl.multiple_of(c * CHUNK, CHUNK), CHUNK)
        _gla_chunk(rows, qk_ref, vg_ref, a1_ref, a2_ref, ab_ref, nrm_ref, ltri_ref, lvl_ref,
                   o_ref, s_ref, cum_ref, la_ref)
        return carry

    lax.fori_loop(0, qk_ref.shape[1] // CHUNK, chunk, 0)


def _gla_chunk(rows, qk_ref, vg_ref, a1_ref, a2_ref, ab_ref, nrm_ref, ltri_ref, lvl_ref,
               o_ref, s_ref, cum_ref, la_ref):
    nseq = qk_ref.shape[0]
    width = nseq * QK_W
    n_heads = nseq * HEADS
    side = lambda parts: jnp.concatenate(parts, axis=1)
    q = side([qk_ref[i, rows, 0:QK_W] for i in range(nseq)]).astype(F32) * (DK ** -0.5)
    k = side([qk_ref[i, rows, QK_W:2 * QK_W] for i in range(nseq)]).astype(F32)
    z = side([_dot(a1_ref[i, rows, :].astype(BF16), a2_ref[...]) + ab_ref[...] for i in range(nseq)])
    log_a = _log2_sigmoid(z) * (1.0 / GATE_TAU)
    la_ref[8:8 + CHUNK, :] = log_a
    ltri = ltri_ref[...]
    hi, mid, lo = _split3(log_a)
    cum = _dot(ltri, hi) + _dot(ltri, mid) + _dot(ltri, lo)
    cum_ref[...] = cum

    lvl = lvl_ref[...]
    groups = CHUNK // 8
    rows8 = lambda g: slice(8 * g, 8 * g + 8)
    heads = [slice(h * DKP, (h + 1) * DKP) for h in range(n_heads)]
    row = lax.broadcasted_iota(jnp.int32, (CHUNK, width), 0)

    def scores(xq, yk):
        return [_dot_nt(xq[:, hs], yk[:, hs]) for hs in heads]

    qb = q.astype(BF16)
    kb = k.astype(BF16)
    attn = [[jnp.where(lvl[rows8(g), :] == -1, p[rows8(g), :], 0.0) for g in range(groups)]
            for p in scores(qb, kb)]

    def merge(level, parts, q_groups):
        for h in range(n_heads):
            for i, g in enumerate(q_groups):
                attn[h][g] = jnp.where(lvl[rows8(g), :] == level, parts[h][rows8(i), :], attn[h][g])

    e = jnp.exp2(jnp.where((row & 1) == 1, log_a, 0.0))
    merge(0, scores((q * e).astype(BF16), kb), range(groups))
    nxt = la_ref[pl.ds(9, CHUNK), :]
    prv = la_ref[pl.ds(7, CHUNK), :]
    r4 = row & 3
    e = jnp.exp2(jnp.where(r4 == 0, nxt, jnp.where(r4 == 1, 0.0, jnp.where(r4 == 2, log_a, log_a + prv))))
    merge(1, scores((q * e).astype(BF16), (k * e).astype(BF16)), range(groups))
    sub8 = lax.broadcasted_iota(jnp.int32, (8, width), 0)
    pieces = []
    for g in range(groups):
        d = cum[rows8(g), :] - cum_ref[8 * g + 3:8 * g + 4, :]
        pieces.append(jnp.where(sub8 < 4, -d, d))
    e = jnp.exp2(jnp.concatenate(pieces, axis=0))
    merge(2, scores((q * e).astype(BF16), (k * e).astype(BF16)), range(groups))
    for level in range(3, N_LEVELS):
        m = 1 << level
        xq, yk, q_groups = [], [], []
        for base in range(0, CHUNK, 2 * m):
            k_rows = slice(base, base + m)
            q_rows = slice(base + m, base + 2 * m)
            edge = cum_ref[base + m - 1:base + m, :]
            yk += [k[k_rows, :] * jnp.exp2(edge - cum[k_rows, :]), k[q_rows, :]]
            xq.append(q[q_rows, :] * jnp.exp2(cum[q_rows, :] - edge))
            q_groups += range((base + m) // 8, (base + 2 * m) // 8)
        merge(level, scores(jnp.concatenate(xq, axis=0).astype(BF16),
                            jnp.concatenate(yk, axis=0).astype(BF16)), q_groups)

    last = cum_ref[CHUNK - 1:CHUNK, :]
    q_dec = (q * jnp.exp2(cum)).astype(BF16)
    k_dec = k * jnp.exp2(last - cum)
    decay_all = jnp.exp2(last)

    lane = lax.broadcasted_iota(jnp.int32, (CHUNK, PAIR_V), 1)
    lo_mask = lane < DV

    for pp in range(n_heads // 2):
        seq_i, p = divmod(pp, HEADS // 2)
        ks = slice(pp * PAIR_K, (pp + 1) * PAIR_K)
        vs = slice(p * PAIR_V, (p + 1) * PAIR_V)
        vb = vg_ref[seq_i, rows, vs]
        zero = jnp.zeros_like(vb)
        v_lo = jnp.where(lo_mask, vb, zero)
        v_hi = jnp.where(lo_mask, zero, vb)
        v_blk = jnp.concatenate([v_lo, v_hi], axis=0)
        state = s_ref[pp]
        lhs = jnp.concatenate([jnp.concatenate(attn[2 * pp], axis=0).astype(BF16),
                               jnp.concatenate(attn[2 * pp + 1], axis=0).astype(BF16),
                               q_dec[:, ks]], axis=1)
        rhs = jnp.concatenate([v_blk, state.astype(BF16)], axis=0)
        o = _dot(lhs, rhs)

        kd_t = k_dec[:, ks].T.astype(BF16)
        upd = jnp.concatenate([_dot(kd_t[0:DKP, :], v_lo), _dot(kd_t[DKP:PAIR_K, :], v_hi)], axis=0)
        dcols = []
        for h in (2 * pp, 2 * pp + 1):
            d_row = jnp.broadcast_to(decay_all[:, h * DKP:(h + 1) * DKP], (DKP, DKP))
            d_col = d_row.T
            dcols.append(jnp.concatenate([d_col] * (PAIR_V // LANE), axis=1))
        s_ref[pp] = state * jnp.concatenate(dcols, axis=0) + upd

        gate = vg_ref[seq_i, rows, V_W + p * PAIR_V:V_W + (p + 1) * PAIR_V].astype(F32)
        out = o * _head_rms_scale(o) * nrm_ref[:, vs] * (gate * jax.nn.sigmoid(gate))
        o_ref[seq_i, rows, vs] = out.astype(o_ref.dtype)


def _gla(proj, gates, a2p, abp, nrm, ltri, lvl, batch, seq):
    nseq = GLA_NSEQ if batch % GLA_NSEQ == 0 else 1
    split = GLA_SPLIT if seq % (GLA_SPLIT * CHUNK) == 0 else 1
    part = seq // split
    grouped = lambda a: a.reshape(batch // nseq, nseq, split, part, a.shape[-1])
    tok = lambda w, cb: pl.BlockSpec((None, nseq, None, part, w), lambda b, c: (b, 0, c, 0, cb))
    const = lambda shape: pl.BlockSpec(shape, lambda b, c: (0,) * len(shape))
    assert OFF_GLA_K == OFF_GLA_Q + QK_W and OFF_GLA_G == OFF_GLA_V + V_W
    out = pl.pallas_call(
        _gla_kernel,
        grid=(batch // nseq, split),
        in_specs=[
            tok(2 * QK_W, OFF_GLA_Q // (2 * QK_W)),
            tok(2 * V_W, OFF_GLA_V // (2 * V_W)),
            tok(LANE, (OFF_GLA_A1 - MAIN_W) // LANE),
            const((LANE, QK_W)),
            const((1, QK_W)),
            const((1, V_W)),
            const((CHUNK, CHUNK)),
            const((CHUNK, CHUNK)),
        ],
        out_specs=tok(V_W, 0),
        out_shape=jax.ShapeDtypeStruct((batch // nseq, nseq, split, part, V_W), BF16),
        scratch_shapes=[
            pltpu.VMEM((nseq * HEADS // 2, PAIR_K, PAIR_V), F32),
            pltpu.VMEM((CHUNK, nseq * QK_W), F32),
            pltpu.VMEM((CHUNK + 16, nseq * QK_W), F32),
        ],
        compiler_params=pltpu.CompilerParams(
            dimension_semantics=("parallel", "arbitrary"), vmem_limit_bytes=VMEM_LIMIT),
        name="gla",
    )(grouped(proj), grouped(proj), grouped(gates), a2p, abp, nrm, ltri, lvl)
    return out.reshape(batch * seq, V_W)


ST_W = PAIR_V + LANE
TAIL = 8


def _mlstm_kernel(qk_ref, vo_ref, if_ref, cw_ref, ifb_ref, nrm_ref, ltri_ref, lvl_ref,
                  o_ref, c_ref, m_ref, xe_ref):
    c_ref[...] = jnp.zeros_like(c_ref)
    m_ref[...] = jnp.zeros_like(m_ref)
    xe_ref[0:TAIL, :] = jnp.zeros((TAIL, 2 * QK_W), F32)

    def chunk(c, carry):
        rows = pl.ds(pl.multiple_of(c * CHUNK, CHUNK), CHUNK)
        _mlstm_chunk(rows, qk_ref, vo_ref, if_ref, cw_ref, ifb_ref, nrm_ref, ltri_ref, lvl_ref,
                     o_ref, c_ref, m_ref, xe_ref)
        return carry

    lax.fori_loop(0, qk_ref.shape[0] // CHUNK, chunk, 0, unroll=4)


def _mlstm_chunk(rows, qk_ref, vo_ref, if_ref, cw_ref, ifb_ref, nrm_ref, ltri_ref, lvl_ref,
                 o_ref, c_ref, m_ref, xe_ref):
    xe_ref[TAIL:TAIL + CHUNK, :] = qk_ref[rows, :].astype(F32)
    y = jnp.zeros((CHUNK, 2 * QK_W), F32)
    for j in range(CONV_WIDTH):
        y = y + cw_ref[j:j + 1, :] * xe_ref[pl.ds(TAIL - (CONV_WIDTH - 1) + j, CHUNK), :]
    xe_ref[0:TAIL, :] = xe_ref[CHUNK:CHUNK + TAIL, :]
    y = y * jax.nn.sigmoid(y)
    qm = y[:, 0:QK_W]
    km = y[:, QK_W:2 * QK_W] * (DK ** -0.5)
    qb = qm.astype(BF16)
    kb = km.astype(BF16)
    km_t = km.T

    slab = if_ref[rows, :] + ifb_ref[...]
    lane_g = lax.broadcasted_iota(jnp.int32, (CHUNK, LANE), 1)
    gates = jnp.where(lane_g < HEADS, slab, _log_sigmoid(slab))
    ltri = ltri_ref[...]
    g_hi, g_mid, g_lo = _split3(gates)
    cum_col = _dot(ltri, g_hi) + _dot(ltri, g_mid) + _dot(ltri, g_lo)
    gates_t = gates.T[0:2 * HEADS, :]
    t_hi, t_mid, t_lo = _split3(gates_t)
    cum_row = _dot_nt(t_hi, ltri) + _dot_nt(t_mid, ltri) + _dot_nt(t_lo, ltri)

    lvl = lvl_ref[...]
    causal = lvl >= -1
    lane = lax.broadcasted_iota(jnp.int32, (CHUNK, PAIR_V), 1)
    lo_mask = lane < DV
    lane_e = lax.broadcasted_iota(jnp.int32, (CHUNK, LANE), 1)
    one_col = [jnp.where(lane_e == i, 1.0, 0.0).astype(BF16) for i in range(2)]
    lane_r = lax.broadcasted_iota(jnp.int32, (PAIR_K, LANE), 1)
    row_r = lax.broadcasted_iota(jnp.int32, (PAIR_K, LANE), 0)
    ones_blk = jnp.where(lane_r == jnp.where(row_r < CHUNK, 0, 1), 1.0, 0.0).astype(BF16)

    for p in range(HEADS // 2):
        ks = slice(p * PAIR_K, (p + 1) * PAIR_K)
        vs = slice(p * PAIR_V, (p + 1) * PAIR_V)
        vb = vo_ref[rows, vs]
        zero = jnp.zeros_like(vb)
        sc_parts, qs_parts, floor_parts, kw_parts, dprev_parts = [], [], [], [], []
        for h in (2 * p, 2 * p + 1):
            hs = slice(h * DKP, (h + 1) * DKP)
            cc = jnp.broadcast_to(cum_col[:, HEADS + h:HEADS + h + 1], (CHUNK, CHUNK))
            cum_r = cum_row[HEADS + h:HEADS + h + 1, :]
            ib_r = gates_t[h:h + 1, :]
            m_prev = m_ref[h:h + 1, :]
            dmat = jnp.where(causal, cc - cum_r + ib_r, -jnp.inf)
            inter = cc + m_prev
            m_t = jnp.maximum(inter, jnp.max(dmat, axis=-1, keepdims=True))
            w = jnp.exp(dmat - m_t)
            sc_inter = jnp.exp(inter - m_t)
            sc_parts.append((_dot_nt(qb[:, hs], kb[:, hs]) * w).astype(BF16))
            qs_parts.append((qm[:, hs] * sc_inter).astype(BF16))
            floor_parts.append(jnp.exp(-m_t))
            total = cum_r[:, CHUNK - 1:CHUNK]
            g_row = total - cum_r + ib_r
            m_new = jnp.maximum(total + m_prev, jnp.max(g_row, axis=-1, keepdims=True))
            wj = jnp.exp(g_row - m_new)
            dprev_parts.append(jnp.exp(total + m_prev - m_new))
            kw_parts.append((km_t[hs, :] * wj).astype(BF16))
            m_ref[h:h + 1, :] = m_new

        state = c_ref[p]
        st_m = state.astype(BF16)
        v_lo = jnp.where(lo_mask, vb, zero)
        v_hi = jnp.where(lo_mask, zero, vb)
        v_blk = jnp.concatenate([v_lo, v_hi], axis=0)
        rhs = jnp.concatenate([jnp.concatenate([v_blk, ones_blk], axis=1), st_m], axis=0)
        lhs = jnp.concatenate(sc_parts + qs_parts, axis=1)
        res = _dot(lhs, rhs)
        num = res[:, 0:PAIR_V]
        den = _pair_cols(res[:, PAIR_V:PAIR_V + 1], res[:, PAIR_V + 1:PAIR_V + 2], CHUNK)
        floor = _pair_cols(floor_parts[0][:, 0:1], floor_parts[1][:, 0:1], CHUNK)
        hid = num / jnp.maximum(jnp.abs(den), floor)

        upd = jnp.concatenate(
            [_dot(kw_parts[0], jnp.concatenate([v_lo, one_col[0]], axis=1)),
             _dot(kw_parts[1], jnp.concatenate([v_hi, one_col[1]], axis=1))], axis=0)
        d_rows = jnp.concatenate(
            [jnp.broadcast_to(jnp.concatenate([d] * (ST_W // LANE), axis=1), (DKP, ST_W))
             for d in dprev_parts], axis=0)
        c_ref[p] = d_rows * state + upd

        og = vo_ref[rows, V_W + p * PAIR_V:V_W + (p + 1) * PAIR_V].astype(F32)
        out = hid * _head_rms_scale(hid) * nrm_ref[:, vs] * jax.nn.sigmoid(og)
        o_ref[rows, vs] = out.astype(o_ref.dtype)


def _mlstm(proj, gates, cw, ifb, nrm, ltri, lvl, batch, seq):
    tok = lambda w, cb: pl.BlockSpec((seq, w), lambda b: (b, cb))
    const = lambda shape: pl.BlockSpec(shape, lambda b: (0,) * len(shape))
    assert OFF_ML_K == OFF_ML_Q + QK_W and OFF_ML_O == OFF_ML_V + V_W
    return pl.pallas_call(
        _mlstm_kernel,
        grid=(batch,),
        in_specs=[
            tok(2 * QK_W, OFF_ML_Q // (2 * QK_W)),
            tok(2 * V_W, OFF_ML_V // (2 * V_W)),
            tok(LANE, (OFF_ML_IF - MAIN_W) // LANE),
            const((CONV_WIDTH, 2 * QK_W)),
            const((1, LANE)),
            const((1, V_W)),
            const((CHUNK, CHUNK)),
            const((CHUNK, CHUNK)),
        ],
        out_specs=pl.BlockSpec((seq, V_W), lambda b: (b, 0)),
        out_shape=jax.ShapeDtypeStruct((batch * seq, V_W), BF16),
        scratch_shapes=[
            pltpu.VMEM((HEADS // 2, PAIR_K, ST_W), F32),
            pltpu.VMEM((2 * HEADS, LANE), F32),
            pltpu.VMEM((TAIL + CHUNK, 2 * QK_W), F32),
        ],
        compiler_params=pltpu.CompilerParams(
            dimension_semantics=("parallel",), vmem_limit_bytes=VMEM_LIMIT),
        name="mlstm",
    )(proj, proj, gates, cw, ifb, nrm, ltri, lvl)


def _sgu_kernel(u_ref, v_ref, lng_ref, lnb_ref, w_ref, b_ref, lvl_ref, o_ref):
    causal = lvl_ref[...] >= -1
    w_causal = [jnp.where(causal, w_ref[g], 0.0).astype(BF16) for g in range(SGU_GROUPS)]
    for n in range(u_ref.shape[0] // CHUNK):
        rows = slice(n * CHUNK, (n + 1) * CHUNK)
        u = jax.nn.gelu(u_ref[rows, :].astype(F32))
        v = jax.nn.gelu(v_ref[rows, :].astype(F32))
        mu = jnp.mean(v, axis=-1, keepdims=True)
        var = jnp.mean(jnp.square(v - mu), axis=-1, keepdims=True)
        vn = ((v - mu) * lax.rsqrt(var + EPS) * lng_ref[...] + lnb_ref[...]).astype(BF16)
        for g in range(SGU_GROUPS):
            gs = slice(g * SGU_CH, (g + 1) * SGU_CH)
            mixed = _dot(w_causal[g], vn[:, gs]) + b_ref[:, gs]
            o_ref[rows, gs] = (u[:, gs] * mixed).astype(o_ref.dtype)


def _sgu(proj, lng, lnb, w, b_full, lvl, batch, seq):
    step = CHUNK * SGU_STEP
    while (batch * seq) % step:
        step //= 2
    nb = batch * seq // step
    tok = lambda w_, cb: pl.BlockSpec((step, w_), lambda i: (i, cb))
    const = lambda shape: pl.BlockSpec(shape, lambda i: (0,) * len(shape))
    return pl.pallas_call(
        _sgu_kernel,
        grid=(nb,),
        in_specs=[
            tok(SGU_W, OFF_SGU_U // SGU_W),
            tok(SGU_W, OFF_SGU_V // SGU_W),
            const((1, SGU_W)),
            const((1, SGU_W)),
            const((SGU_GROUPS, CHUNK, CHUNK)),
            const((CHUNK, SGU_W)),
            const((CHUNK, CHUNK)),
        ],
        out_specs=pl.BlockSpec((step, SGU_W), lambda i: (i, 0)),
        out_shape=jax.ShapeDtypeStruct((batch * seq, SGU_W), BF16),
        compiler_params=pltpu.CompilerParams(
            dimension_semantics=("parallel",), vmem_limit_bytes=VMEM_LIMIT),
        name="sgu",
    )(proj, proj, lng, lnb, w, b_full, lvl)


def _out_proj_kernel(x_ref, a_ref, b_ref, c_ref, wa_ref, wb_ref, wc_ref, o_ref):
    o_ref[...] = (x_ref[...] + _dot(a_ref[...], wa_ref[...]) + _dot(b_ref[...], wb_ref[...])
                  + _dot(c_ref[...], wc_ref[...]))


def _out_proj(x2, mix_a, mix_b, mix_c, w_out, tm, tn):
    t, d = x2.shape
    return pl.pallas_call(
        _out_proj_kernel,
        grid=(d // tn, t // tm),
        in_specs=[
            pl.BlockSpec((tm, tn), lambda j, i: (i, j)),
            pl.BlockSpec((tm, V_W), lambda j, i: (i, 0)),
            pl.BlockSpec((tm, V_W), lambda j, i: (i, 0)),
            pl.BlockSpec((tm, SGU_W), lambda j, i: (i, 0)),
            pl.BlockSpec((V_W, tn), lambda j, i: (0, j)),
            pl.BlockSpec((V_W, tn), lambda j, i: (1, j)),
            pl.BlockSpec((SGU_W, tn), lambda j, i: (2 * V_W // SGU_W, j)),
        ],
        out_specs=pl.BlockSpec((tm, tn), lambda j, i: (i, j)),
        out_shape=jax.ShapeDtypeStruct((t, d), F32),
        compiler_params=pltpu.CompilerParams(
            dimension_semantics=("parallel", "arbitrary"), vmem_limit_bytes=VMEM_LIMIT),
        name="out_proj",
    )(x2, mix_a, mix_b, mix_c, w_out, w_out, w_out)


def _ffn_kernel(x_ref, g_ref, wg_ref, wu_ref, wd_ref, gf_ref, *rest, final_norm, plans):
    n = len(plans)
    o_ref, h_ref = rest[n], rest[2 * n + 1]
    j = pl.program_id(1)
    _run_casts(pl.program_id(0) * pl.num_programs(1) + j, plans, rest[0:n], rest[n + 1:2 * n + 1])

    @pl.when(j == 0)
    def _():
        x = x_ref[...]
        ms = jnp.mean(x * x, axis=-1, keepdims=True)
        h_ref[...] = (x * lax.rsqrt(ms + EPS) * g_ref[...]).astype(BF16)
        o_ref[...] = x

    h = h_ref[...]
    gate = _dot(h, wg_ref[...])
    up = _dot(h, wu_ref[...])
    act = (gate * jax.nn.sigmoid(gate) * up).astype(BF16)
    o_ref[...] += _dot(act, wd_ref[...])

    if final_norm:
        @pl.when(j == pl.num_programs(1) - 1)
        def _():
            y = o_ref[...]
            ms = jnp.mean(y * y, axis=-1, keepdims=True)
            o_ref[...] = y * lax.rsqrt(ms + EPS) * gf_ref[...]


def _ffn(x2, g, w_gu, w_down, g_final, tm, tf, final_norm, casts=()):
    t, d = x2.shape
    nf = D_FF // tf
    plans = [_cast_plan(cw, cl, (t // tm) * nf) for cw, cl in casts]
    c_in, c_out, c_shapes, c_args = _cast_specs(plans, lambda i, j: i * nf + j)
    return pl.pallas_call(
        functools.partial(_ffn_kernel, final_norm=final_norm, plans=plans),
        grid=(t // tm, nf),
        in_specs=[
            pl.BlockSpec((tm, d), lambda i, j: (i, 0)),
            pl.BlockSpec((1, d), lambda i, j: (0, 0)),
            pl.BlockSpec((d, tf), lambda i, j: (0, j)),
            pl.BlockSpec((d, tf), lambda i, j: (0, nf + j)),
            pl.BlockSpec((tf, d), lambda i, j: (j, 0)),
            pl.BlockSpec((1, d), lambda i, j: (0, 0)),
        ] + c_in,
        out_specs=[pl.BlockSpec((tm, d), lambda i, j: (i, 0))] + c_out,
        out_shape=[jax.ShapeDtypeStruct((t, d), F32)] + c_shapes,
        scratch_shapes=[pltpu.VMEM((tm, d), BF16)],
        compiler_params=pltpu.CompilerParams(
            dimension_semantics=("arbitrary", "arbitrary"), vmem_limit_bytes=VMEM_LIMIT),
        name="ffn",
    )(x2, g, w_gu, w_gu, w_down, g_final, *c_args)


def _tiles(tokens):
    tm = 1024
    while tokens % tm:
        tm //= 2
    return tm


def kernel(x, norm_mix, w_in, gla_a2, gla_ab, gla_norm, ml_conv, ml_ib, ml_fb, ml_norm,
           sgu_ln_g, sgu_ln_b, sgu_w, sgu_b, w_out, norm_ffn, w_gu, w_down, norm_final):
    batch, seq, d = x.shape
    depth = w_in.shape[0]
    tokens = batch * seq
    tm = _tiles(tokens)

    lvl = jnp.asarray(_pair_level_matrix())
    ltri = jnp.asarray(np.tril(np.ones((CHUNK, CHUNK), np.float32)), BF16)

    w_in_b = _relayout_w_in(w_in)
    dense = (w_out, w_gu, w_down)

    xc = x.reshape(tokens, d)
    for l in range(depth):
        if l == 0:
            proj, gates, w_out_b, w_gu_b, w_down_b = _in_proj(
                xc, norm_mix[l][None, :], w_in_b, l, tm, 1536, casts=[(w, 0) for w in dense])
        else:
            proj, gates = _in_proj(xc, norm_mix[l][None, :], w_in_b, l, tm, 1536)

        a2p = jnp.pad(_pad_heads(gla_a2[l]), ((0, LANE - GATE_RANK), (0, 0))).astype(BF16)
        abp = _pad_heads(gla_ab[l])[None, :]
        mix_a = _gla(proj, gates, a2p, abp, gla_norm[l][None, :], ltri, lvl, batch, seq)

        cw = jnp.concatenate([_pad_heads(ml_conv[l][:, :HEADS * DK]),
                              _pad_heads(ml_conv[l][:, HEADS * DK:])], axis=1)
        ifb = jnp.pad(jnp.concatenate([ml_ib[l], ml_fb[l]]), (0, LANE - 2 * HEADS))[None, :]
        mix_b = _mlstm(proj, gates, cw, ifb, ml_norm[l][None, :], ltri, lvl, batch, seq)

        b_full = jnp.repeat(sgu_b[l].T, SGU_CH, axis=1)
        mix_c = _sgu(proj, sgu_ln_g[l][None, :], sgu_ln_b[l][None, :], sgu_w[l], b_full, lvl, batch, seq)

        x1 = _out_proj(xc, mix_a, mix_b, mix_c, w_out_b, tm // 2, d)
        nxt = [(w, l + 1) for w in dense] if l + 1 < depth else []
        xc, *cast = _ffn(x1, norm_ffn[l][None, :], w_gu_b, w_down_b, norm_final[None, :], tm, 512,
                         final_norm=(l == depth - 1), casts=nxt)
        if cast:
            w_out_b, w_gu_b, w_down_b = cast
    return xc.reshape(batch, seq, d)
```

```python
import functools
from typing import NamedTuple

import numpy as np
import jax
import jax.numpy as jnp
from jax import lax
from jax.experimental import pallas as pl
from jax.experimental.pallas import tpu as pltpu

F32 = jnp.float32
BF16 = jnp.bfloat16

HEADS = 4
DK = 96
DKP = 128
DV = 192
QK_W = HEADS * DKP
V_W = HEADS * DV
PAIR_K = 2 * DKP
PAIR_V = 2 * DV
SGU_W = 512
SGU_GROUPS = 4
SGU_CH = 128
GATE_RANK = 16
GATE_TAU = 16.0
CONV_WIDTH = 4
D_FF = 5632
EPS = 1e-6
CHUNK = 128
N_LEVELS = 7
LANE = 128

OFF_GLA_Q, OFF_GLA_K, OFF_ML_Q, OFF_ML_K = 0, 512, 1024, 1536
OFF_SGU_U, OFF_SGU_V = 2048, 2560
OFF_GLA_V, OFF_GLA_G, OFF_ML_V, OFF_ML_O = 3072, 3840, 4608, 5376
MAIN_W = 6144
OFF_GLA_A1, OFF_ML_IF = 6144, 6272
GATE_W = 2 * LANE
PROJ_W = MAIN_W + GATE_W
SGU_STEP = 4

VMEM_LIMIT = 58 * 1024 * 1024


def _column_moves():
    gk = HEADS * DK
    sizes = (gk, gk, V_W, V_W, GATE_RANK, gk, gk, V_W, V_W, HEADS, HEADS, SGU_W, SGU_W)
    starts = np.concatenate([[0], np.cumsum(sizes)]).tolist()
    (gq, gkk, gv, gg, ga1, mq, mk, mv, mo, mi, mf, su, sv) = starts[:-1]
    moves = []
    for dst, src in ((OFF_GLA_Q, gq), (OFF_GLA_K, gkk), (OFF_ML_Q, mq), (OFF_ML_K, mk)):
        moves += [(dst + h * DKP, src + h * DK, DK) for h in range(HEADS)]
    moves += [(OFF_SGU_U, su, SGU_W), (OFF_SGU_V, sv, SGU_W), (OFF_GLA_V, gv, V_W), (OFF_GLA_G, gg, V_W),
              (OFF_ML_V, mv, V_W), (OFF_ML_O, mo, V_W), (OFF_GLA_A1, ga1, GATE_RANK),
              (OFF_ML_IF, mi, 2 * HEADS)]
    assert mf == mi + HEADS and starts[-1] == sum(sizes)
    return moves


def _relayout_kernel(w_ref, o_ref):
    o_ref[...] = jnp.zeros_like(o_ref)
    for dst, src, n in _column_moves():
        if n % 16:
            pad = jnp.zeros((16 - n % 16, w_ref.shape[1]), F32)
            o_ref[dst:dst + n + pad.shape[0], :] = jnp.concatenate([w_ref[src:src + n, :], pad]).astype(BF16)
        else:
            o_ref[dst:dst + n, :] = w_ref[src:src + n, :].astype(BF16)


def _relayout_w_in(w_in, cols=256):
    w_t = jnp.swapaxes(w_in, 1, 2)
    depth, p, d = w_t.shape
    return pl.pallas_call(
        _relayout_kernel,
        grid=(depth, d // cols),
        in_specs=[pl.BlockSpec((None, p, cols), lambda l, i: (l, 0, i))],
        out_specs=pl.BlockSpec((None, PROJ_W, cols), lambda l, i: (l, 0, i)),
        out_shape=jax.ShapeDtypeStruct((depth, PROJ_W, d), BF16),
        compiler_params=pltpu.CompilerParams(
            dimension_semantics=("parallel", "parallel"), vmem_limit_bytes=VMEM_LIMIT),
        name="relayout",
    )(w_t)


def _pad_heads(a):
    lead = a.shape[:-1]
    a = a.reshape(lead + (HEADS, DK))
    a = jnp.pad(a, [(0, 0)] * len(lead) + [(0, 0), (0, DKP - DK)])
    return a.reshape(lead + (QK_W,))


def _pair_level_matrix():
    c = CHUNK
    t = np.arange(c)[:, None]
    s = np.arange(c)[None, :]
    x = np.bitwise_xor(t, s)
    lvl = np.floor(np.log2(np.maximum(x, 1))).astype(np.int32)
    lvl = np.where(s == t, -1, lvl)
    lvl = np.where(s > t, -2, lvl)
    return lvl.astype(np.int32)


def _dot(a, b):
    return jnp.dot(a, b, preferred_element_type=F32)


def _dot_nt(a, b):
    return lax.dot_general(a, b, (((1,), (1,)), ((), ())), preferred_element_type=F32)


def _split3(x):
    hi = x.astype(BF16)
    r = x - hi.astype(F32)
    mid = r.astype(BF16)
    lo = (r - mid.astype(F32)).astype(BF16)
    return hi, mid, lo


LOG2E = 1.4426950408889634
LN2 = 0.6931471805599453


def _log2_sigmoid(x):
    xl = x * LOG2E
    return jnp.minimum(xl, 0.0) - jnp.log2(1.0 + jnp.exp2(-jnp.abs(xl)))


def _log_sigmoid(x):
    return _log2_sigmoid(x) * LN2


def _pair_cols(c0, c1, rows):
    first = lax.broadcasted_iota(jnp.int32, (rows, LANE), 1) < DV - LANE
    b0 = jnp.broadcast_to(c0, (rows, LANE))
    b1 = jnp.broadcast_to(c1, (rows, LANE))
    return jnp.concatenate([b0, jnp.where(first, b0, b1), b1], axis=1)


def _head_rms_scale(o):
    rows = o.shape[0]
    o2 = o * o
    first = lax.broadcasted_iota(jnp.int32, (rows, LANE), 1) < DV - LANE
    mid = o2[:, LANE:2 * LANE]
    ss0 = jnp.sum(o2[:, 0:LANE] + jnp.where(first, mid, 0.0), axis=-1, keepdims=True)
    ss1 = jnp.sum(o2[:, 2 * LANE:3 * LANE] + jnp.where(first, 0.0, mid), axis=-1, keepdims=True)
    return _pair_cols(lax.rsqrt(ss0 / DV + EPS), lax.rsqrt(ss1 / DV + EPS), rows)


def _cast_plan(w, layer, steps):
    rows, cols = w.shape[1], w.shape[2]
    nblk = max(n for n in range(1, steps + 1) if rows % n == 0 and (rows // n) % 16 == 0)
    return w, layer, rows, cols, nblk


def _cast_specs(plans, step_of):
    in_specs, out_specs, out_shapes, args = [], [], [], []
    for w, layer, rows, cols, nblk in plans:
        blk = lambda *g, nblk=nblk: jnp.minimum(step_of(*g), nblk - 1)
        in_specs.append(pl.BlockSpec((None, rows // nblk, cols), lambda *g, blk=blk, layer=layer: (layer, blk(*g), 0)))
        out_specs.append(pl.BlockSpec((rows // nblk, cols), lambda *g, blk=blk: (blk(*g), 0)))
        out_shapes.append(jax.ShapeDtypeStruct((rows, cols), BF16))
        args.append(w)
    return in_specs, out_specs, out_shapes, args


def _run_casts(step, plans, src_refs, dst_refs):
    for (_, _, _, _, nblk), src, dst in zip(plans, src_refs, dst_refs):
        @pl.when(step < nblk)
        def _(src=src, dst=dst):
            dst[...] = src[...].astype(BF16)


def _in_proj_kernel(x_ref, g_ref, w_ref, wg_ref, *rest, plans):
    n = len(plans)
    o_ref, og_ref, h_ref = rest[n], rest[n + 1], rest[2 * n + 2]
    _run_casts(pl.program_id(0) * pl.num_programs(1) + pl.program_id(1), plans,
               rest[0:n], rest[n + 2:2 * n + 2])

    @pl.when(pl.program_id(1) == 0)
    def _():
        x = x_ref[...]
        ms = jnp.mean(x * x, axis=-1, keepdims=True)
        h_ref[...] = (x * lax.rsqrt(ms + EPS) * g_ref[...]).astype(BF16)
        og_ref[...] = _dot_nt(h_ref[...], wg_ref[...])

    o_ref[...] = _dot_nt(h_ref[...], w_ref[...]).astype(o_ref.dtype)


def _in_proj(x2, g, w, layer, tm, tn, casts=()):
    t, d = x2.shape
    ni, nj = t // tm, MAIN_W // tn
    plans = [_cast_plan(cw, cl, ni * nj) for cw, cl in casts]
    c_in, c_out, c_shapes, c_args = _cast_specs(plans, lambda i, j: i * nj + j)
    return pl.pallas_call(
        functools.partial(_in_proj_kernel, plans=plans),
        grid=(ni, nj),
        in_specs=[
            pl.BlockSpec((tm, d), lambda i, j: (i, 0)),
            pl.BlockSpec((1, d), lambda i, j: (0, 0)),
            pl.BlockSpec((None, tn, d), lambda i, j: (layer, j, 0)),
            pl.BlockSpec((None, GATE_W, d), lambda i, j: (layer, MAIN_W // GATE_W, 0)),
        ] + c_in,
        out_specs=[pl.BlockSpec((tm, tn), lambda i, j: (i, j)),
                   pl.BlockSpec((tm, GATE_W), lambda i, j: (i, 0))] + c_out,
        out_shape=[jax.ShapeDtypeStruct((t, MAIN_W), BF16),
                   jax.ShapeDtypeStruct((t, GATE_W), F32)] + c_shapes,
        scratch_shapes=[pltpu.VMEM((tm, d), BF16)],
        compiler_params=pltpu.CompilerParams(
            dimension_semantics=("arbitrary", "arbitrary"), vmem_limit_bytes=VMEM_LIMIT),
        name="in_proj",
    )(x2, g, w, w, *c_args)


GLA_NSEQ = 2
GLA_SPLIT = 2


def _gla_kernel(qk_ref, vg_ref, a1_ref, a2_ref, ab_ref, nrm_ref, ltri_ref, lvl_ref,
                o_ref, s_ref, cum_ref, la_ref):
    @pl.when(pl.program_id(1) == 0)
    def _():
        s_ref[...] = jnp.zeros_like(s_ref)
        la_ref[0:8, :] = jnp.zeros((8, la_ref.shape[1]), F32)
        la_ref[8 + CHUNK:16 + CHUNK, :] = jnp.zeros((8, la_ref.shape[1]), F32)

    def chunk(c, carry):
        rows = pl.ds(pl.multiple_of(c * CHUNK, CHUNK), CHUNK)
        _gla_chunk(rows, qk_ref, vg_ref, a1_ref, a2_ref, ab_ref, nrm_ref, ltri_ref, lvl_ref,
                   o_ref, s_ref, cum_ref, la_ref)
        return carry

    lax.fori_loop(0, qk_ref.shape[1] // CHUNK, chunk, 0)


def _gla_chunk(rows, qk_ref, vg_ref, a1_ref, a2_ref, ab_ref, nrm_ref, ltri_ref, lvl_ref,
               o_ref, s_ref, cum_ref, la_ref):
    nseq = qk_ref.shape[0]
    width = nseq * QK_W
    n_heads = nseq * HEADS
    side = lambda parts: jnp.concatenate(parts, axis=1)
    q = side([qk_ref[i, rows, 0:QK_W] for i in range(nseq)]).astype(F32) * (DK ** -0.5)
    k = side([qk_ref[i, rows, QK_W:2 * QK_W] for i in range(nseq)]).astype(F32)
    z = side([_dot(a1_ref[i, rows, :].astype(BF16), a2_ref[...]) + ab_ref[...] for i in range(nseq)])
    log_a = _log2_sigmoid(z) * (1.0 / GATE_TAU)
    la_ref[8:8 + CHUNK, :] = log_a
    ltri = ltri_ref[...]
    hi, mid, lo = _split3(log_a)
    cum = _dot(ltri, hi) + _dot(ltri, mid) + _dot(ltri, lo)
    cum_ref[...] = cum

    lvl = lvl_ref[...]
    groups = CHUNK // 8
    rows8 = lambda g: slice(8 * g, 8 * g + 8)
    heads = [slice(h * DKP, (h + 1) * DKP) for h in range(n_heads)]
    row = lax.broadcasted_iota(jnp.int32, (CHUNK, width), 0)

    def scores(xq, yk):
        return [_dot_nt(xq[:, hs], yk[:, hs]) for hs in heads]

    qb = q.astype(BF16)
    kb = k.astype(BF16)
    attn = [[jnp.where(lvl[rows8(g), :] == -1, p[rows8(g), :], 0.0) for g in range(groups)]
            for p in scores(qb, kb)]

    def merge(level, parts, q_groups):
        for h in range(n_heads):
            for i, g in enumerate(q_groups):
                attn[h][g] = jnp.where(lvl[rows8(g), :] == level, parts[h][rows8(i), :], attn[h][g])

    e = jnp.exp2(jnp.where((row & 1) == 1, log_a, 0.0))
    merge(0, scores((q * e).astype(BF16), kb), range(groups))
    nxt = la_ref[pl.ds(9, CHUNK), :]
    prv = la_ref[pl.ds(7, CHUNK), :]
    r4 = row & 3
    e = jnp.exp2(jnp.where(r4 == 0, nxt, jnp.where(r4 == 1, 0.0, jnp.where(r4 == 2, log_a, log_a + prv))))
    merge(1, scores((q * e).astype(BF16), (k * e).astype(BF16)), range(groups))
    sub8 = lax.broadcasted_iota(jnp.int32, (8, width), 0)
    pieces = []
    for g in range(groups):
        d = cum[rows8(g), :] - cum_ref[8 * g + 3:8 * g + 4, :]
        pieces.append(jnp.where(sub8 < 4, -d, d))
    e = jnp.exp2(jnp.concatenate(pieces, axis=0))
    merge(2, scores((q * e).astype(BF16), (k * e).astype(BF16)), range(groups))
    for level in range(3, N_LEVELS):
        m = 1 << level
        xq, yk, q_groups = [], [], []
        for base in range(0, CHUNK, 2 * m):
            k_rows = slice(base, base + m)
            q_rows = slice(base + m, base + 2 * m)
            edge = cum_ref[base + m - 1:base + m, :]
            yk += [k[k_rows, :] * jnp.exp2(edge - cum[k_rows, :]), k[q_rows, :]]
            xq.append(q[q_rows, :] * jnp.exp2(cum[q_rows, :] - edge))
            q_groups += range((base + m) // 8, (base + 2 * m) // 8)
        merge(level, scores(jnp.concatenate(xq, axis=0).astype(BF16),
                            jnp.concatenate(yk, axis=0).astype(BF16)), q_groups)

    last = cum_ref[CHUNK - 1:CHUNK, :]
    q_dec = (q * jnp.exp2(cum)).astype(BF16)
    k_dec = k * jnp.exp2(last - cum)
    decay_all = jnp.exp2(last)

    lane = lax.broadcasted_iota(jnp.int32, (CHUNK, PAIR_V), 1)
    lo_mask = lane < DV

    for pp in range(n_heads // 2):
        seq_i, p = divmod(pp, HEADS // 2)
        ks = slice(pp * PAIR_K, (pp + 1) * PAIR_K)
        vs = slice(p * PAIR_V, (p + 1) * PAIR_V)
        vb = vg_ref[seq_i, rows, vs]
        zero = jnp.zeros_like(vb)
        v_lo = jnp.where(lo_mask, vb, zero)
        v_hi = jnp.where(lo_mask, zero, vb)
        v_blk = jnp.concatenate([v_lo, v_hi], axis=0)
        state = s_ref[pp]
        lhs = jnp.concatenate([jnp.concatenate(attn[2 * pp], axis=0).astype(BF16),
                               jnp.concatenate(attn[2 * pp + 1], axis=0).astype(BF16),
                               q_dec[:, ks]], axis=1)
        rhs = jnp.concatenate([v_blk, state.astype(BF16)], axis=0)
        o = _dot(lhs, rhs)

        kd_t = k_dec[:, ks].T.astype(BF16)
        upd = jnp.concatenate([_dot(kd_t[0:DKP, :], v_lo), _dot(kd_t[DKP:PAIR_K, :], v_hi)], axis=0)
        dcols = []
        for h in (2 * pp, 2 * pp + 1):
            d_row = jnp.broadcast_to(decay_all[:, h * DKP:(h + 1) * DKP], (DKP, DKP))
            d_col = d_row.T
            dcols.append(jnp.concatenate([d_col] * (PAIR_V // LANE), axis=1))
        s_ref[pp] = state * jnp.concatenate(dcols, axis=0) + upd

        gate = vg_ref[seq_i, rows, V_W + p * PAIR_V:V_W + (p + 1) * PAIR_V].astype(F32)
        out = o * _head_rms_scale(o) * nrm_ref[:, vs] * (gate * jax.nn.sigmoid(gate))
        o_ref[seq_i, rows, vs] = out.astype(o_ref.dtype)


def _gla(proj, gates, a2p, abp, nrm, ltri, lvl, batch, seq):
    nseq = GLA_NSEQ if batch % GLA_NSEQ == 0 else 1
    split = GLA_SPLIT if seq % (GLA_SPLIT * CHUNK) == 0 else 1
    part = seq // split
    grouped = lambda a: a.reshape(batch // nseq, nseq, split, part, a.shape[-1])
    tok = lambda w, cb: pl.BlockSpec((None, nseq, None, part, w), lambda b, c: (b, 0, c, 0, cb))
    const = lambda shape: pl.BlockSpec(shape, lambda b, c: (0,) * len(shape))
    assert OFF_GLA_K == OFF_GLA_Q + QK_W and OFF_GLA_G == OFF_GLA_V + V_W
    out = pl.pallas_call(
        _gla_kernel,
        grid=(batch // nseq, split),
        in_specs=[
            tok(2 * QK_W, OFF_GLA_Q // (2 * QK_W)),
            tok(2 * V_W, OFF_GLA_V // (2 * V_W)),
            tok(LANE, (OFF_GLA_A1 - MAIN_W) // LANE),
            const((LANE, QK_W)),
            const((1, QK_W)),
            const((1, V_W)),
            const((CHUNK, CHUNK)),
            const((CHUNK, CHUNK)),
        ],
        out_specs=tok(V_W, 0),
        out_shape=jax.ShapeDtypeStruct((batch // nseq, nseq, split, part, V_W), BF16),
        scratch_shapes=[
            pltpu.VMEM((nseq * HEADS // 2, PAIR_K, PAIR_V), F32),
            pltpu.VMEM((CHUNK, nseq * QK_W), F32),
            pltpu.VMEM((CHUNK + 16, nseq * QK_W), F32),
        ],
        compiler_params=pltpu.CompilerParams(
            dimension_semantics=("parallel", "arbitrary"), vmem_limit_bytes=VMEM_LIMIT),
        name="gla",
    )(grouped(proj), grouped(proj), grouped(gates), a2p, abp, nrm, ltri, lvl)
    return out.reshape(batch * seq, V_W)


ST_W = PAIR_V + LANE
TAIL = 8


def _mlstm_kernel(qk_ref, vo_ref, if_ref, cw_ref, ifb_ref, nrm_ref, ltri_ref, lvl_ref,
                  o_ref, c_ref, m_ref, xe_ref):
    c_ref[...] = jnp.zeros_like(c_ref)
    m_ref[...] = jnp.zeros_like(m_ref)
    xe_ref[0:TAIL, :] = jnp.zeros((TAIL, 2 * QK_W), F32)

    def chunk(c, carry):
        rows = pl.ds(pl.multiple_of(c * CHUNK, CHUNK), CHUNK)
        _mlstm_chunk(rows, qk_ref, vo_ref, if_ref, cw_ref, ifb_ref, nrm_ref, ltri_ref, lvl_ref,
                     o_ref, c_ref, m_ref, xe_ref)
        return carry

    lax.fori_loop(0, qk_ref.shape[0] // CHUNK, chunk, 0, unroll=2)


def _mlstm_chunk(rows, qk_ref, vo_ref, if_ref, cw_ref, ifb_ref, nrm_ref, ltri_ref, lvl_ref,
                 o_ref, c_ref, m_ref, xe_ref):
    xe_ref[TAIL:TAIL + CHUNK, :] = qk_ref[rows, :].astype(F32)
    y = jnp.zeros((CHUNK, 2 * QK_W), F32)
    for j in range(CONV_WIDTH):
        y = y + cw_ref[j:j + 1, :] * xe_ref[pl.ds(TAIL - (CONV_WIDTH - 1) + j, CHUNK), :]
    xe_ref[0:TAIL, :] = xe_ref[CHUNK:CHUNK + TAIL, :]
    y = y * jax.nn.sigmoid(y)
    qm = y[:, 0:QK_W]
    km = y[:, QK_W:2 * QK_W] * (DK ** -0.5)
    qb = qm.astype(BF16)
    kb = km.astype(BF16)
    km_t = km.T

    slab = if_ref[rows, :] + ifb_ref[...]
    lane_g = lax.broadcasted_iota(jnp.int32, (CHUNK, LANE), 1)
    gates = jnp.where(lane_g < HEADS, slab, _log_sigmoid(slab))
    ltri = ltri_ref[...]
    g_hi, g_mid, g_lo = _split3(gates)
    cum_col = _dot(ltri, g_hi) + _dot(ltri, g_mid) + _dot(ltri, g_lo)
    gates_t = gates.T[0:2 * HEADS, :]
    t_hi, t_mid, t_lo = _split3(gates_t)
    cum_row = _dot_nt(t_hi, ltri) + _dot_nt(t_mid, ltri) + _dot_nt(t_lo, ltri)

    lvl = lvl_ref[...]
    causal = lvl >= -1
    lane = lax.broadcasted_iota(jnp.int32, (CHUNK, PAIR_V), 1)
    lo_mask = lane < DV
    lane_e = lax.broadcasted_iota(jnp.int32, (CHUNK, LANE), 1)
    one_col = [jnp.where(lane_e == i, 1.0, 0.0).astype(BF16) for i in range(2)]
    lane_r = lax.broadcasted_iota(jnp.int32, (PAIR_K, LANE), 1)
    row_r = lax.broadcasted_iota(jnp.int32, (PAIR_K, LANE), 0)
    ones_blk = jnp.where(lane_r == jnp.where(row_r < CHUNK, 0, 1), 1.0, 0.0).astype(BF16)

    for p in range(HEADS // 2):
        ks = slice(p * PAIR_K, (p + 1) * PAIR_K)
        vs = slice(p * PAIR_V, (p + 1) * PAIR_V)
        vb = vo_ref[rows, vs]
        zero = jnp.zeros_like(vb)
        sc_parts, qs_parts, floor_parts, kw_parts, dprev_parts = [], [], [], [], []
        for h in (2 * p, 2 * p + 1):
            hs = slice(h * DKP, (h + 1) * DKP)
            cc = jnp.broadcast_to(cum_col[:, HEADS + h:HEADS + h + 1], (CHUNK, CHUNK))
            cum_r = cum_row[HEADS + h:HEADS + h + 1, :]
            ib_r = gates_t[h:h + 1, :]
            m_prev = m_ref[h:h + 1, :]
            dmat = jnp.where(causal, cc - cum_r + ib_r, -jnp.inf)
            inter = cc + m_prev
            m_t = jnp.maximum(inter, jnp.max(dmat, axis=-1, keepdims=True))
            w = jnp.exp(dmat - m_t)
            sc_inter = jnp.exp(inter - m_t)
            sc_parts.append((_dot_nt(qb[:, hs], kb[:, hs]) * w).astype(BF16))
            qs_parts.append((qm[:, hs] * sc_inter).astype(BF16))
            floor_parts.append(jnp.exp(-m_t))
            total = cum_r[:, CHUNK - 1:CHUNK]
            g_row = total - cum_r + ib_r
            m_new = jnp.maximum(total + m_prev, jnp.max(g_row, axis=-1, keepdims=True))
            wj = jnp.exp(g_row - m_new)
            dprev_parts.append(jnp.exp(total + m_prev - m_new))
            kw_parts.append((km_t[hs, :] * wj).astype(BF16))
            m_ref[h:h + 1, :] = m_new

        state = c_ref[p]
        st_m = state.astype(BF16)
        v_lo = jnp.where(lo_mask, vb, zero)
        v_hi = jnp.where(lo_mask, zero, vb)
        v_blk = jnp.concatenate([v_lo, v_hi], axis=0)
        rhs = jnp.concatenate([jnp.concatenate([v_blk, ones_blk], axis=1), st_m], axis=0)
        lhs = jnp.concatenate(sc_parts + qs_parts, axis=1)
        res = _dot(lhs, rhs)
        num = res[:, 0:PAIR_V]
        den = _pair_cols(res[:, PAIR_V:PAIR_V + 1], res[:, PAIR_V + 1:PAIR_V + 2], CHUNK)
        floor = _pair_cols(floor_parts[0][:, 0:1], floor_parts[1][:, 0:1], CHUNK)
        hid = num / jnp.maximum(jnp.abs(den), floor)

        upd = jnp.concatenate(
            [_dot(kw_parts[0], jnp.concatenate([v_lo, one_col[0]], axis=1)),
             _dot(kw_parts[1], jnp.concatenate([v_hi, one_col[1]], axis=1))], axis=0)
        d_rows = jnp.concatenate(
            [jnp.broadcast_to(jnp.concatenate([d] * (ST_W // LANE), axis=1), (DKP, ST_W))
             for d in dprev_parts], axis=0)
        c_ref[p] = d_rows * state + upd

        og = vo_ref[rows, V_W + p * PAIR_V:V_W + (p + 1) * PAIR_V].astype(F32)
        out = hid * _head_rms_scale(hid) * nrm_ref[:, vs] * jax.nn.sigmoid(og)
        o_ref[rows, vs] = out.astype(o_ref.dtype)


def _mlstm(proj, gates, cw, ifb, nrm, ltri, lvl, batch, seq):
    tok = lambda w, cb: pl.BlockSpec((seq, w), lambda b: (b, cb))
    const = lambda shape: pl.BlockSpec(shape, lambda b: (0,) * len(shape))
    assert OFF_ML_K == OFF_ML_Q + QK_W and OFF_ML_O == OFF_ML_V + V_W
    return pl.pallas_call(
        _mlstm_kernel,
        grid=(batch,),
        in_specs=[
            tok(2 * QK_W, OFF_ML_Q // (2 * QK_W)),
            tok(2 * V_W, OFF_ML_V // (2 * V_W)),
            tok(LANE, (OFF_ML_IF - MAIN_W) // LANE),
            const((CONV_WIDTH, 2 * QK_W)),
            const((1, LANE)),
            const((1, V_W)),
            const((CHUNK, CHUNK)),
            const((CHUNK, CHUNK)),
        ],
        out_specs=pl.BlockSpec((seq, V_W), lambda b: (b, 0)),
        out_shape=jax.ShapeDtypeStruct((batch * seq, V_W), BF16),
        scratch_shapes=[
            pltpu.VMEM((HEADS // 2, PAIR_K, ST_W), F32),
            pltpu.VMEM((2 * HEADS, LANE), F32),
            pltpu.VMEM((TAIL + CHUNK, 2 * QK_W), F32),
        ],
        compiler_params=pltpu.CompilerParams(
            dimension_semantics=("parallel",), vmem_limit_bytes=VMEM_LIMIT),
        name="mlstm",
    )(proj, proj, gates, cw, ifb, nrm, ltri, lvl)


def _sgu_kernel(u_ref, v_ref, lng_ref, lnb_ref, w_ref, b_ref, lvl_ref, o_ref):
    causal = lvl_ref[...] >= -1
    w_causal = [jnp.where(causal, w_ref[g], 0.0).astype(BF16) for g in range(SGU_GROUPS)]
    for n in range(u_ref.shape[0] // CHUNK):
        rows = slice(n * CHUNK, (n + 1) * CHUNK)
        u = jax.nn.gelu(u_ref[rows, :].astype(F32))
        v = jax.nn.gelu(v_ref[rows, :].astype(F32))
        mu = jnp.mean(v, axis=-1, keepdims=True)
        var = jnp.mean(jnp.square(v - mu), axis=-1, keepdims=True)
        vn = ((v - mu) * lax.rsqrt(var + EPS) * lng_ref[...] + lnb_ref[...]).astype(BF16)
        for g in range(SGU_GROUPS):
            gs = slice(g * SGU_CH, (g + 1) * SGU_CH)
            mixed = _dot(w_causal[g], vn[:, gs]) + b_ref[:, gs]
            o_ref[rows, gs] = (u[:, gs] * mixed).astype(o_ref.dtype)


def _sgu(proj, lng, lnb, w, b_full, lvl, batch, seq):
    step = CHUNK * SGU_STEP
    while (batch * seq) % step:
        step //= 2
    nb = batch * seq // step
    tok = lambda w_, cb: pl.BlockSpec((step, w_), lambda i: (i, cb))
    const = lambda shape: pl.BlockSpec(shape, lambda i: (0,) * len(shape))
    return pl.pallas_call(
        _sgu_kernel,
        grid=(nb,),
        in_specs=[
            tok(SGU_W, OFF_SGU_U // SGU_W),
            tok(SGU_W, OFF_SGU_V // SGU_W),
            const((1, SGU_W)),
            const((1, SGU_W)),
            const((SGU_GROUPS, CHUNK, CHUNK)),
            const((CHUNK, SGU_W)),
            const((CHUNK, CHUNK)),
        ],
        out_specs=pl.BlockSpec((step, SGU_W), lambda i: (i, 0)),
        out_shape=jax.ShapeDtypeStruct((batch * seq, SGU_W), BF16),
        compiler_params=pltpu.CompilerParams(
            dimension_semantics=("parallel",), vmem_limit_bytes=VMEM_LIMIT),
        name="sgu",
    )(proj, proj, lng, lnb, w, b_full, lvl)


def _out_proj_kernel(x_ref, a_ref, b_ref, c_ref, wa_ref, wb_ref, wc_ref, o_ref):
    o_ref[...] = (x_ref[...] + _dot(a_ref[...], wa_ref[...]) + _dot(b_ref[...], wb_ref[...])
                  + _dot(c_ref[...], wc_ref[...]))


def _out_proj(x2, mix_a, mix_b, mix_c, w_out, tm, tn):
    t, d = x2.shape
    return pl.pallas_call(
        _out_proj_kernel,
        grid=(d // tn, t // tm),
        in_specs=[
            pl.BlockSpec((tm, tn), lambda j, i: (i, j)),
            pl.BlockSpec((tm, V_W), lambda j, i: (i, 0)),
            pl.BlockSpec((tm, V_W), lambda j, i: (i, 0)),
            pl.BlockSpec((tm, SGU_W), lambda j, i: (i, 0)),
            pl.BlockSpec((V_W, tn), lambda j, i: (0, j)),
            pl.BlockSpec((V_W, tn), lambda j, i: (1, j)),
            pl.BlockSpec((SGU_W, tn), lambda j, i: (2 * V_W // SGU_W, j)),
        ],
        out_specs=pl.BlockSpec((tm, tn), lambda j, i: (i, j)),
        out_shape=jax.ShapeDtypeStruct((t, d), F32),
        compiler_params=pltpu.CompilerParams(
            dimension_semantics=("parallel", "arbitrary"), vmem_limit_bytes=VMEM_LIMIT),
        name="out_proj",
    )(x2, mix_a, mix_b, mix_c, w_out, w_out, w_out)


def _ffn_kernel(x_ref, g_ref, wg_ref, wu_ref, wd_ref, gf_ref, *rest, final_norm, plans):
    n = len(plans)
    o_ref, h_ref = rest[n], rest[2 * n + 1]
    j = pl.program_id(1)
    _run_casts(pl.program_id(0) * pl.num_programs(1) + j, plans, rest[0:n], rest[n + 1:2 * n + 1])

    @pl.when(j == 0)
    def _():
        x = x_ref[...]
        ms = jnp.mean(x * x, axis=-1, keepdims=True)
        h_ref[...] = (x * lax.rsqrt(ms + EPS) * g_ref[...]).astype(BF16)
        o_ref[...] = x

    h = h_ref[...]
    gate = _dot(h, wg_ref[...])
    up = _dot(h, wu_ref[...])
    act = (gate * jax.nn.sigmoid(gate) * up).astype(BF16)
    o_ref[...] += _dot(act, wd_ref[...])

    if final_norm:
        @pl.when(j == pl.num_programs(1) - 1)
        def _():
            y = o_ref[...]
            ms = jnp.mean(y * y, axis=-1, keepdims=True)
            o_ref[...] = y * lax.rsqrt(ms + EPS) * gf_ref[...]


def _ffn(x2, g, w_gu, w_down, g_final, tm, tf, final_norm, casts=()):
    t, d = x2.shape
    nf = D_FF // tf
    plans = [_cast_plan(cw, cl, (t // tm) * nf) for cw, cl in casts]
    c_in, c_out, c_shapes, c_args = _cast_specs(plans, lambda i, j: i * nf + j)
    return pl.pallas_call(
        functools.partial(_ffn_kernel, final_norm=final_norm, plans=plans),
        grid=(t // tm, nf),
        in_specs=[
            pl.BlockSpec((tm, d), lambda i, j: (i, 0)),
            pl.BlockSpec((1, d), lambda i, j: (0, 0)),
            pl.BlockSpec((d, tf), lambda i, j: (0, j)),
            pl.BlockSpec((d, tf), lambda i, j: (0, nf + j)),
            pl.BlockSpec((tf, d), lambda i, j: (j, 0)),
            pl.BlockSpec((1, d), lambda i, j: (0, 0)),
        ] + c_in,
        out_specs=[pl.BlockSpec((tm, d), lambda i, j: (i, 0))] + c_out,
        out_shape=[jax.ShapeDtypeStruct((t, d), F32)] + c_shapes,
        scratch_shapes=[pltpu.VMEM((tm, d), BF16)],
        compiler_params=pltpu.CompilerParams(
            dimension_semantics=("arbitrary", "arbitrary"), vmem_limit_bytes=VMEM_LIMIT),
        name="ffn",
    )(x2, g, w_gu, w_gu, w_down, g_final, *c_args)


class _Tiles(NamedTuple):
    tm: int
    tn_in: int
    tm_out: int
    tf: int


def _tiles(tokens):
    tm = 1024
    while tokens % tm:
        tm //= 2
    return _Tiles(tm=tm, tn_in=1536, tm_out=max(tm // 2, 8), tf=512)


def kernel(x, norm_mix, w_in, gla_a2, gla_ab, gla_norm, ml_conv, ml_ib, ml_fb, ml_norm,
           sgu_ln_g, sgu_ln_b, sgu_w, sgu_b, w_out, norm_ffn, w_gu, w_down, norm_final):
    batch, seq, d = x.shape
    depth = w_in.shape[0]
    tokens = batch * seq
    tiles = _tiles(tokens)

    lvl = jnp.asarray(_pair_level_matrix())
    ltri = jnp.asarray(np.tril(np.ones((CHUNK, CHUNK), np.float32)), BF16)

    w_in_b = _relayout_w_in(w_in)
    dense = (w_out, w_gu, w_down)

    xc = x.reshape(tokens, d)
    for l in range(depth):
        if l == 0:
            proj, gates, w_out_b, w_gu_b, w_down_b = _in_proj(
                xc, norm_mix[l][None, :], w_in_b, l, tiles.tm, tiles.tn_in, casts=[(w, 0) for w in dense])
        else:
            proj, gates = _in_proj(xc, norm_mix[l][None, :], w_in_b, l, tiles.tm, tiles.tn_in)

        a2p = jnp.pad(_pad_heads(gla_a2[l]), ((0, LANE - GATE_RANK), (0, 0))).astype(BF16)
        abp = _pad_heads(gla_ab[l])[None, :]
        mix_a = _gla(proj, gates, a2p, abp, gla_norm[l][None, :], ltri, lvl, batch, seq)

        cw = jnp.concatenate([_pad_heads(ml_conv[l][:, :HEADS * DK]),
                              _pad_heads(ml_conv[l][:, HEADS * DK:])], axis=1)
        ifb = jnp.pad(jnp.concatenate([ml_ib[l], ml_fb[l]]), (0, LANE - 2 * HEADS))[None, :]
        mix_b = _mlstm(proj, gates, cw, ifb, ml_norm[l][None, :], ltri, lvl, batch, seq)

        b_full = jnp.repeat(sgu_b[l].T, SGU_CH, axis=1)
        mix_c = _sgu(proj, sgu_ln_g[l][None, :], sgu_ln_b[l][None, :], sgu_w[l], b_full, lvl, batch, seq)

        x1 = _out_proj(xc, mix_a, mix_b, mix_c, w_out_b, tiles.tm_out, d)
        nxt = [(w, l + 1) for w in dense] if l + 1 < depth else []
        xc, *cast = _ffn(x1, norm_ffn[l][None, :], w_gu_b, w_down_b, norm_final[None, :], tiles.tm,
                         tiles.tf, final_norm=(l == depth - 1), casts=nxt)
        if cast:
            w_out_b, w_gu_b, w_down_b = cast
    return xc.reshape(batch, seq, d)
```

```python
import functools
from typing import NamedTuple

import numpy as np
import jax
import jax.numpy as jnp
from jax import lax
from jax.experimental import pallas as pl
from jax.experimental.pallas import tpu as pltpu

F32 = jnp.float32
BF16 = jnp.bfloat16

HEADS = 4
DK = 96
DKP = 128
DV = 192
QK_W = HEADS * DKP
V_W = HEADS * DV
PAIR_K = 2 * DKP
PAIR_V = 2 * DV
SGU_W = 512
SGU_GROUPS = 4
SGU_CH = 128
GATE_RANK = 16
GATE_TAU = 16.0
CONV_WIDTH = 4
D_FF = 5632
EPS = 1e-6
CHUNK = 128
N_LEVELS = 7
LANE = 128

OFF_GLA_Q, OFF_GLA_K, OFF_ML_Q, OFF_ML_K = 0, 512, 1024, 1536
OFF_SGU_U, OFF_SGU_V = 2048, 2560
OFF_GLA_V, OFF_GLA_G, OFF_ML_V, OFF_ML_O = 3072, 3840, 4608, 5376
MAIN_W = 6144
OFF_GLA_A1, OFF_ML_IF = 6144, 6272
GATE_W = 2 * LANE
PROJ_W = MAIN_W + GATE_W
SGU_STEP = 4

VMEM_LIMIT = 58 * 1024 * 1024


def _column_moves():
    gk = HEADS * DK
    sizes = (gk, gk, V_W, V_W, GATE_RANK, gk, gk, V_W, V_W, HEADS, HEADS, SGU_W, SGU_W)
    starts = np.concatenate([[0], np.cumsum(sizes)]).tolist()
    (gq, gkk, gv, gg, ga1, mq, mk, mv, mo, mi, mf, su, sv) = starts[:-1]
    moves = []
    for dst, src in ((OFF_GLA_Q, gq), (OFF_GLA_K, gkk), (OFF_ML_Q, mq), (OFF_ML_K, mk)):
        moves += [(dst + h * DKP, src + h * DK, DK) for h in range(HEADS)]
    moves += [(OFF_SGU_U, su, SGU_W), (OFF_SGU_V, sv, SGU_W), (OFF_GLA_V, gv, V_W), (OFF_GLA_G, gg, V_W),
              (OFF_ML_V, mv, V_W), (OFF_ML_O, mo, V_W), (OFF_GLA_A1, ga1, GATE_RANK),
              (OFF_ML_IF, mi, 2 * HEADS)]
    assert mf == mi + HEADS and starts[-1] == sum(sizes)
    return moves


def _relayout_kernel(w_ref, o_ref):
    o_ref[...] = jnp.zeros_like(o_ref)
    for dst, src, n in _column_moves():
        if n % 16:
            pad = jnp.zeros((16 - n % 16, w_ref.shape[1]), F32)
            o_ref[dst:dst + n + pad.shape[0], :] = jnp.concatenate([w_ref[src:src + n, :], pad]).astype(BF16)
        else:
            o_ref[dst:dst + n, :] = w_ref[src:src + n, :].astype(BF16)


def _relayout_w_in(w_t, layer, cols=256):
    _, p, d = w_t.shape
    return pl.pallas_call(
        _relayout_kernel,
        grid=(d // cols,),
        in_specs=[pl.BlockSpec((None, p, cols), lambda i: (layer, 0, i))],
        out_specs=pl.BlockSpec((PROJ_W, cols), lambda i: (0, i)),
        out_shape=jax.ShapeDtypeStruct((PROJ_W, d), BF16),
        compiler_params=pltpu.CompilerParams(
            dimension_semantics=("parallel",), vmem_limit_bytes=VMEM_LIMIT),
        name="relayout",
    )(w_t)


def _pad_heads(a):
    lead = a.shape[:-1]
    a = a.reshape(lead + (HEADS, DK))
    a = jnp.pad(a, [(0, 0)] * len(lead) + [(0, 0), (0, DKP - DK)])
    return a.reshape(lead + (QK_W,))


def _pair_level_matrix():
    c = CHUNK
    t = np.arange(c)[:, None]
    s = np.arange(c)[None, :]
    x = np.bitwise_xor(t, s)
    lvl = np.floor(np.log2(np.maximum(x, 1))).astype(np.int32)
    lvl = np.where(s == t, -1, lvl)
    lvl = np.where(s > t, -2, lvl)
    return lvl.astype(np.int32)


def _dot(a, b):
    return jnp.dot(a, b, preferred_element_type=F32)


def _dot_nt(a, b):
    return lax.dot_general(a, b, (((1,), (1,)), ((), ())), preferred_element_type=F32)


def _split3(x):
    hi = x.astype(BF16)
    r = x - hi.astype(F32)
    mid = r.astype(BF16)
    lo = (r - mid.astype(F32)).astype(BF16)
    return hi, mid, lo


LOG2E = 1.4426950408889634
LN2 = 0.6931471805599453


def _log2_sigmoid(x):
    xl = x * LOG2E
    return jnp.minimum(xl, 0.0) - jnp.log2(1.0 + jnp.exp2(-jnp.abs(xl)))


def _log_sigmoid(x):
    return _log2_sigmoid(x) * LN2


def _pair_cols(c0, c1, rows):
    first = lax.broadcasted_iota(jnp.int32, (rows, LANE), 1) < DV - LANE
    b0 = jnp.broadcast_to(c0, (rows, LANE))
    b1 = jnp.broadcast_to(c1, (rows, LANE))
    return jnp.concatenate([b0, jnp.where(first, b0, b1), b1], axis=1)


def _head_rms_scale(o):
    rows = o.shape[0]
    o2 = o * o
    first = lax.broadcasted_iota(jnp.int32, (rows, LANE), 1) < DV - LANE
    mid = o2[:, LANE:2 * LANE]
    ss0 = jnp.sum(o2[:, 0:LANE] + jnp.where(first, mid, 0.0), axis=-1, keepdims=True)
    ss1 = jnp.sum(o2[:, 2 * LANE:3 * LANE] + jnp.where(first, 0.0, mid), axis=-1, keepdims=True)
    return _pair_cols(lax.rsqrt(ss0 / DV + EPS), lax.rsqrt(ss1 / DV + EPS), rows)


def _cast_plan(w, layer, steps):
    rows, cols = w.shape[1], w.shape[2]
    nblk = max(n for n in range(1, steps + 1) if rows % n == 0 and (rows // n) % 16 == 0)
    return w, layer, rows, cols, nblk


def _cast_specs(plans, step_of):
    in_specs, out_specs, out_shapes, args = [], [], [], []
    for w, layer, rows, cols, nblk in plans:
        blk = lambda *g, nblk=nblk: jnp.minimum(step_of(*g), nblk - 1)
        in_specs.append(pl.BlockSpec((None, rows // nblk, cols), lambda *g, blk=blk, layer=layer: (layer, blk(*g), 0)))
        out_specs.append(pl.BlockSpec((rows // nblk, cols), lambda *g, blk=blk: (blk(*g), 0)))
        out_shapes.append(jax.ShapeDtypeStruct((rows, cols), BF16))
        args.append(w)
    return in_specs, out_specs, out_shapes, args


def _run_casts(step, plans, src_refs, dst_refs):
    for (_, _, _, _, nblk), src, dst in zip(plans, src_refs, dst_refs):
        @pl.when(step < nblk)
        def _(src=src, dst=dst):
            dst[...] = src[...].astype(BF16)


def _in_proj_kernel(x_ref, g_ref, w_ref, wg_ref, *rest, plans):
    n = len(plans)
    o_ref, og_ref, h_ref = rest[n], rest[n + 1], rest[2 * n + 2]
    _run_casts(pl.program_id(0) * pl.num_programs(1) + pl.program_id(1), plans,
               rest[0:n], rest[n + 2:2 * n + 2])

    @pl.when(pl.program_id(1) == 0)
    def _():
        x = x_ref[...]
        ms = jnp.mean(x * x, axis=-1, keepdims=True)
        h_ref[...] = (x * lax.rsqrt(ms + EPS) * g_ref[...]).astype(BF16)
        og_ref[...] = _dot_nt(h_ref[...], wg_ref[...])

    o_ref[...] = _dot_nt(h_ref[...], w_ref[...]).astype(o_ref.dtype)


def _in_proj(x2, g, w, tm, tn, casts=()):
    t, d = x2.shape
    ni, nj = t // tm, MAIN_W // tn
    plans = [_cast_plan(cw, cl, ni * nj) for cw, cl in casts]
    c_in, c_out, c_shapes, c_args = _cast_specs(plans, lambda i, j: i * nj + j)
    return pl.pallas_call(
        functools.partial(_in_proj_kernel, plans=plans),
        grid=(ni, nj),
        in_specs=[
            pl.BlockSpec((tm, d), lambda i, j: (i, 0)),
            pl.BlockSpec((1, d), lambda i, j: (0, 0)),
            pl.BlockSpec((tn, d), lambda i, j: (j, 0)),
            pl.BlockSpec((GATE_W, d), lambda i, j: (MAIN_W // GATE_W, 0)),
        ] + c_in,
        out_specs=[pl.BlockSpec((tm, tn), lambda i, j: (i, j)),
                   pl.BlockSpec((tm, GATE_W), lambda i, j: (i, 0))] + c_out,
        out_shape=[jax.ShapeDtypeStruct((t, MAIN_W), BF16),
                   jax.ShapeDtypeStruct((t, GATE_W), F32)] + c_shapes,
        scratch_shapes=[pltpu.VMEM((tm, d), BF16)],
        compiler_params=pltpu.CompilerParams(
            dimension_semantics=("arbitrary", "arbitrary"), vmem_limit_bytes=VMEM_LIMIT),
        name="in_proj",
    )(x2, g, w, w, *c_args)


GLA_NSEQ = 2
GLA_SPLIT = 2


def _gla_kernel(qk_ref, vg_ref, a1_ref, a2_ref, ab_ref, nrm_ref, ltri_ref, lvl_ref,
                o_ref, s_ref, cum_ref, la_ref):
    @pl.when(pl.program_id(1) == 0)
    def _():
        s_ref[...] = jnp.zeros_like(s_ref)
        la_ref[0:8, :] = jnp.zeros((8, la_ref.shape[1]), F32)
        la_ref[8 + CHUNK:16 + CHUNK, :] = jnp.zeros((8, la_ref.shape[1]), F32)

    def chunk(c, carry):
        rows = pl.ds(pl.multiple_of(c * CHUNK, CHUNK), CHUNK)
        _gla_chunk(rows, qk_ref, vg_ref, a1_ref, a2_ref, ab_ref, nrm_ref, ltri_ref, lvl_ref,
                   o_ref, s_ref, cum_ref, la_ref)
        return carry

    lax.fori_loop(0, qk_ref.shape[1] // CHUNK, chunk, 0)


def _gla_chunk(rows, qk_ref, vg_ref, a1_ref, a2_ref, ab_ref, nrm_ref, ltri_ref, lvl_ref,
               o_ref, s_ref, cum_ref, la_ref):
    nseq = qk_ref.shape[0]
    width = nseq * QK_W
    n_heads = nseq * HEADS
    side = lambda parts: jnp.concatenate(parts, axis=1)
    q = side([qk_ref[i, rows, 0:QK_W] for i in range(nseq)]).astype(F32) * (DK ** -0.5)
    k = side([qk_ref[i, rows, QK_W:2 * QK_W] for i in range(nseq)]).astype(F32)
    z = side([_dot(a1_ref[i, rows, :].astype(BF16), a2_ref[...]) + ab_ref[...] for i in range(nseq)])
    log_a = _log2_sigmoid(z) * (1.0 / GATE_TAU)
    la_ref[8:8 + CHUNK, :] = log_a
    ltri = ltri_ref[...]
    hi, mid, lo = _split3(log_a)
    cum = _dot(ltri, hi) + _dot(ltri, mid) + _dot(ltri, lo)
    cum_ref[...] = cum

    lvl = lvl_ref[...]
    groups = CHUNK // 8
    rows8 = lambda g: slice(8 * g, 8 * g + 8)
    heads = [slice(h * DKP, (h + 1) * DKP) for h in range(n_heads)]
    row = lax.broadcasted_iota(jnp.int32, (CHUNK, width), 0)

    def scores(xq, yk):
        return [_dot_nt(xq[:, hs], yk[:, hs]) for hs in heads]

    qb = q.astype(BF16)
    kb = k.astype(BF16)
    attn = [[jnp.where(lvl[rows8(g), :] == -1, p[rows8(g), :], 0.0) for g in range(groups)]
            for p in scores(qb, kb)]

    def merge(level, parts, q_groups):
        for h in range(n_heads):
            for i, g in enumerate(q_groups):
                attn[h][g] = jnp.where(lvl[rows8(g), :] == level, parts[h][rows8(i), :], attn[h][g])

    e = jnp.exp2(jnp.where((row & 1) == 1, log_a, 0.0))
    merge(0, scores((q * e).astype(BF16), kb), range(groups))
    nxt = la_ref[pl.ds(9, CHUNK), :]
    prv = la_ref[pl.ds(7, CHUNK), :]
    r4 = row & 3
    e = jnp.exp2(jnp.where(r4 == 0, nxt, jnp.where(r4 == 1, 0.0, jnp.where(r4 == 2, log_a, log_a + prv))))
    merge(1, scores((q * e).astype(BF16), (k * e).astype(BF16)), range(groups))
    sub8 = lax.broadcasted_iota(jnp.int32, (8, width), 0)
    pieces = []
    for g in range(groups):
        d = cum[rows8(g), :] - cum_ref[8 * g + 3:8 * g + 4, :]
        pieces.append(jnp.where(sub8 < 4, -d, d))
    e = jnp.exp2(jnp.concatenate(pieces, axis=0))
    merge(2, scores((q * e).astype(BF16), (k * e).astype(BF16)), range(groups))
    for level in range(3, N_LEVELS):
        m = 1 << level
        xq, yk, q_groups = [], [], []
        for base in range(0, CHUNK, 2 * m):
            k_rows = slice(base, base + m)
            q_rows = slice(base + m, base + 2 * m)
            edge = cum_ref[base + m - 1:base + m, :]
            yk += [k[k_rows, :] * jnp.exp2(edge - cum[k_rows, :]), k[q_rows, :]]
            xq.append(q[q_rows, :] * jnp.exp2(cum[q_rows, :] - edge))
            q_groups += range((base + m) // 8, (base + 2 * m) // 8)
        merge(level, scores(jnp.concatenate(xq, axis=0).astype(BF16),
                            jnp.concatenate(yk, axis=0).astype(BF16)), q_groups)

    last = cum_ref[CHUNK - 1:CHUNK, :]
    q_dec = (q * jnp.exp2(cum)).astype(BF16)
    k_dec = k * jnp.exp2(last - cum)
    decay_all = jnp.exp2(last)

    lane = lax.broadcasted_iota(jnp.int32, (CHUNK, PAIR_V), 1)
    lo_mask = lane < DV

    for pp in range(n_heads // 2):
        seq_i, p = divmod(pp, HEADS // 2)
        ks = slice(pp * PAIR_K, (pp + 1) * PAIR_K)
        vs = slice(p * PAIR_V, (p + 1) * PAIR_V)
        vb = vg_ref[seq_i, rows, vs]
        zero = jnp.zeros_like(vb)
        v_lo = jnp.where(lo_mask, vb, zero)
        v_hi = jnp.where(lo_mask, zero, vb)
        v_blk = jnp.concatenate([v_lo, v_hi], axis=0)
        state = s_ref[pp]
        lhs = jnp.concatenate([jnp.concatenate(attn[2 * pp], axis=0).astype(BF16),
                               jnp.concatenate(attn[2 * pp + 1], axis=0).astype(BF16),
                               q_dec[:, ks]], axis=1)
        rhs = jnp.concatenate([v_blk, state.astype(BF16)], axis=0)
        o = _dot(lhs, rhs)

        kd_t = k_dec[:, ks].T.astype(BF16)
        upd = jnp.concatenate([_dot(kd_t[0:DKP, :], v_lo), _dot(kd_t[DKP:PAIR_K, :], v_hi)], axis=0)
        dcols = []
        for h in (2 * pp, 2 * pp + 1):
            d_row = jnp.broadcast_to(decay_all[:, h * DKP:(h + 1) * DKP], (DKP, DKP))
            d_col = d_row.T
            dcols.append(jnp.concatenate([d_col] * (PAIR_V // LANE), axis=1))
        s_ref[pp] = state * jnp.concatenate(dcols, axis=0) + upd

        gate = vg_ref[seq_i, rows, V_W + p * PAIR_V:V_W + (p + 1) * PAIR_V].astype(F32)
        out = o * _head_rms_scale(o) * nrm_ref[:, vs] * (gate * jax.nn.sigmoid(gate))
        o_ref[seq_i, rows, vs] = out.astype(o_ref.dtype)


def _gla(proj, gates, a2p, abp, nrm, ltri, lvl, batch, seq):
    nseq = GLA_NSEQ if batch % GLA_NSEQ == 0 else 1
    split = GLA_SPLIT if seq % (GLA_SPLIT * CHUNK) == 0 else 1
    part = seq // split
    grouped = lambda a: a.reshape(batch // nseq, nseq, split, part, a.shape[-1])
    tok = lambda w, cb: pl.BlockSpec((None, nseq, None, part, w), lambda b, c: (b, 0, c, 0, cb))
    const = lambda shape: pl.BlockSpec(shape, lambda b, c: (0,) * len(shape))
    assert OFF_GLA_K == OFF_GLA_Q + QK_W and OFF_GLA_G == OFF_GLA_V + V_W
    out = pl.pallas_call(
        _gla_kernel,
        grid=(batch // nseq, split),
        in_specs=[
            tok(2 * QK_W, OFF_GLA_Q // (2 * QK_W)),
            tok(2 * V_W, OFF_GLA_V // (2 * V_W)),
            tok(LANE, (OFF_GLA_A1 - MAIN_W) // LANE),
            const((LANE, QK_W)),
            const((1, QK_W)),
            const((1, V_W)),
            const((CHUNK, CHUNK)),
            const((CHUNK, CHUNK)),
        ],
        out_specs=tok(V_W, 0),
        out_shape=jax.ShapeDtypeStruct((batch // nseq, nseq, split, part, V_W), BF16),
        scratch_shapes=[
            pltpu.VMEM((nseq * HEADS // 2, PAIR_K, PAIR_V), F32),
            pltpu.VMEM((CHUNK, nseq * QK_W), F32),
            pltpu.VMEM((CHUNK + 16, nseq * QK_W), F32),
        ],
        compiler_params=pltpu.CompilerParams(
            dimension_semantics=("parallel", "arbitrary"), vmem_limit_bytes=VMEM_LIMIT),
        name="gla",
    )(grouped(proj), grouped(proj), grouped(gates), a2p, abp, nrm, ltri, lvl)
    return out.reshape(batch * seq, V_W)


ST_W = PAIR_V + LANE
TAIL = 8


def _mlstm_kernel(qk_ref, vo_ref, if_ref, cw_ref, ifb_ref, nrm_ref, ltri_ref, lvl_ref,
                  o_ref, c_ref, m_ref, xe_ref):
    c_ref[...] = jnp.zeros_like(c_ref)
    m_ref[...] = jnp.zeros_like(m_ref)
    xe_ref[0:TAIL, :] = jnp.zeros((TAIL, 2 * QK_W), F32)

    def chunk(c, carry):
        rows = pl.ds(pl.multiple_of(c * CHUNK, CHUNK), CHUNK)
        _mlstm_chunk(rows, qk_ref, vo_ref, if_ref, cw_ref, ifb_ref, nrm_ref, ltri_ref, lvl_ref,
                     o_ref, c_ref, m_ref, xe_ref)
        return carry

    lax.fori_loop(0, qk_ref.shape[0] // CHUNK, chunk, 0, unroll=2)


def _mlstm_chunk(rows, qk_ref, vo_ref, if_ref, cw_ref, ifb_ref, nrm_ref, ltri_ref, lvl_ref,
                 o_ref, c_ref, m_ref, xe_ref):
    xe_ref[TAIL:TAIL + CHUNK, :] = qk_ref[rows, :].astype(F32)
    y = jnp.zeros((CHUNK, 2 * QK_W), F32)
    for j in range(CONV_WIDTH):
        y = y + cw_ref[j:j + 1, :] * xe_ref[pl.ds(TAIL - (CONV_WIDTH - 1) + j, CHUNK), :]
    xe_ref[0:TAIL, :] = xe_ref[CHUNK:CHUNK + TAIL, :]
    y = y * jax.nn.sigmoid(y)
    qm = y[:, 0:QK_W]
    km = y[:, QK_W:2 * QK_W] * (DK ** -0.5)
    qb = qm.astype(BF16)
    kb = km.astype(BF16)
    km_t = km.T

    slab = if_ref[rows, :] + ifb_ref[...]
    lane_g = lax.broadcasted_iota(jnp.int32, (CHUNK, LANE), 1)
    gates = jnp.where(lane_g < HEADS, slab, _log_sigmoid(slab))
    ltri = ltri_ref[...]
    g_hi, g_mid, g_lo = _split3(gates)
    cum_col = _dot(ltri, g_hi) + _dot(ltri, g_mid) + _dot(ltri, g_lo)
    gates_t = gates.T[0:2 * HEADS, :]
    t_hi, t_mid, t_lo = _split3(gates_t)
    cum_row = _dot_nt(t_hi, ltri) + _dot_nt(t_mid, ltri) + _dot_nt(t_lo, ltri)

    lvl = lvl_ref[...]
    causal = lvl >= -1
    lane = lax.broadcasted_iota(jnp.int32, (CHUNK, PAIR_V), 1)
    lo_mask = lane < DV
    lane_e = lax.broadcasted_iota(jnp.int32, (CHUNK, LANE), 1)
    one_col = [jnp.where(lane_e == i, 1.0, 0.0).astype(BF16) for i in range(2)]
    lane_r = lax.broadcasted_iota(jnp.int32, (PAIR_K, LANE), 1)
    row_r = lax.broadcasted_iota(jnp.int32, (PAIR_K, LANE), 0)
    ones_blk = jnp.where(lane_r == jnp.where(row_r < CHUNK, 0, 1), 1.0, 0.0).astype(BF16)

    for p in range(HEADS // 2):
        ks = slice(p * PAIR_K, (p + 1) * PAIR_K)
        vs = slice(p * PAIR_V, (p + 1) * PAIR_V)
        vb = vo_ref[rows, vs]
        zero = jnp.zeros_like(vb)
        sc_parts, qs_parts, floor_parts, kw_parts, dprev_parts = [], [], [], [], []
        for h in (2 * p, 2 * p + 1):
            hs = slice(h * DKP, (h + 1) * DKP)
            cc = jnp.broadcast_to(cum_col[:, HEADS + h:HEADS + h + 1], (CHUNK, CHUNK))
            cum_r = cum_row[HEADS + h:HEADS + h + 1, :]
            ib_r = gates_t[h:h + 1, :]
            m_prev = m_ref[h:h + 1, :]
            dmat = jnp.where(causal, cc - cum_r + ib_r, -jnp.inf)
            inter = cc + m_prev
            m_t = jnp.maximum(inter, jnp.max(dmat, axis=-1, keepdims=True))
            w = jnp.exp(dmat - m_t)
            sc_inter = jnp.exp(inter - m_t)
            sc_parts.append((_dot_nt(qb[:, hs], kb[:, hs]) * w).astype(BF16))
            qs_parts.append((qm[:, hs] * sc_inter).astype(BF16))
            floor_parts.append(jnp.exp(-m_t))
            total = cum_r[:, CHUNK - 1:CHUNK]
            g_row = total - cum_r + ib_r
            m_new = jnp.maximum(total + m_prev, jnp.max(g_row, axis=-1, keepdims=True))
            wj = jnp.exp(g_row - m_new)
            dprev_parts.append(jnp.exp(total + m_prev - m_new))
            kw_parts.append((km_t[hs, :] * wj).astype(BF16))
            m_ref[h:h + 1, :] = m_new

        state = c_ref[p]
        st_m = state.astype(BF16)
        v_lo = jnp.where(lo_mask, vb, zero)
        v_hi = jnp.where(lo_mask, zero, vb)
        v_blk = jnp.concatenate([v_lo, v_hi], axis=0)
        rhs = jnp.concatenate([jnp.concatenate([v_blk, ones_blk], axis=1), st_m], axis=0)
        lhs = jnp.concatenate(sc_parts + qs_parts, axis=1)
        res = _dot(lhs, rhs)
        num = res[:, 0:PAIR_V]
        den = _pair_cols(res[:, PAIR_V:PAIR_V + 1], res[:, PAIR_V + 1:PAIR_V + 2], CHUNK)
        floor = _pair_cols(floor_parts[0][:, 0:1], floor_parts[1][:, 0:1], CHUNK)
        hid = num / jnp.maximum(jnp.abs(den), floor)

        upd = jnp.concatenate(
            [_dot(kw_parts[0], jnp.concatenate([v_lo, one_col[0]], axis=1)),
             _dot(kw_parts[1], jnp.concatenate([v_hi, one_col[1]], axis=1))], axis=0)
        d_rows = jnp.concatenate(
            [jnp.broadcast_to(jnp.concatenate([d] * (ST_W // LANE), axis=1), (DKP, ST_W))
             for d in dprev_parts], axis=0)
        c_ref[p] = d_rows * state + upd

        og = vo_ref[rows, V_W + p * PAIR_V:V_W + (p + 1) * PAIR_V].astype(F32)
        out = hid * _head_rms_scale(hid) * nrm_ref[:, vs] * jax.nn.sigmoid(og)
        o_ref[rows, vs] = out.astype(o_ref.dtype)


def _mlstm(proj, gates, cw, ifb, nrm, ltri, lvl, batch, seq):
    tok = lambda w, cb: pl.BlockSpec((seq, w), lambda b: (b, cb))
    const = lambda shape: pl.BlockSpec(shape, lambda b: (0,) * len(shape))
    assert OFF_ML_K == OFF_ML_Q + QK_W and OFF_ML_O == OFF_ML_V + V_W
    return pl.pallas_call(
        _mlstm_kernel,
        grid=(batch,),
        in_specs=[
            tok(2 * QK_W, OFF_ML_Q // (2 * QK_W)),
            tok(2 * V_W, OFF_ML_V // (2 * V_W)),
            tok(LANE, (OFF_ML_IF - MAIN_W) // LANE),
            const((CONV_WIDTH, 2 * QK_W)),
            const((1, LANE)),
            const((1, V_W)),
            const((CHUNK, CHUNK)),
            const((CHUNK, CHUNK)),
        ],
        out_specs=pl.BlockSpec((seq, V_W), lambda b: (b, 0)),
        out_shape=jax.ShapeDtypeStruct((batch * seq, V_W), BF16),
        scratch_shapes=[
            pltpu.VMEM((HEADS // 2, PAIR_K, ST_W), F32),
            pltpu.VMEM((2 * HEADS, LANE), F32),
            pltpu.VMEM((TAIL + CHUNK, 2 * QK_W), F32),
        ],
        compiler_params=pltpu.CompilerParams(
            dimension_semantics=("parallel",), vmem_limit_bytes=VMEM_LIMIT),
        name="mlstm",
    )(proj, proj, gates, cw, ifb, nrm, ltri, lvl)


def _sgu_kernel(u_ref, v_ref, lng_ref, lnb_ref, w_ref, b_ref, lvl_ref, o_ref):
    causal = lvl_ref[...] >= -1
    w_causal = [jnp.where(causal, w_ref[g], 0.0).astype(BF16) for g in range(SGU_GROUPS)]
    for n in range(u_ref.shape[0] // CHUNK):
        rows = slice(n * CHUNK, (n + 1) * CHUNK)
        u = jax.nn.gelu(u_ref[rows, :].astype(F32))
        v = jax.nn.gelu(v_ref[rows, :].astype(F32))
        mu = jnp.mean(v, axis=-1, keepdims=True)
        var = jnp.mean(jnp.square(v - mu), axis=-1, keepdims=True)
        vn = ((v - mu) * lax.rsqrt(var + EPS) * lng_ref[...] + lnb_ref[...]).astype(BF16)
        for g in range(SGU_GROUPS):
            gs = slice(g * SGU_CH, (g + 1) * SGU_CH)
            mixed = _dot(w_causal[g], vn[:, gs]) + b_ref[:, gs]
            o_ref[rows, gs] = (u[:, gs] * mixed).astype(o_ref.dtype)


def _sgu(proj, lng, lnb, w, b_full, lvl, batch, seq):
    step = CHUNK * SGU_STEP
    while (batch * seq) % step:
        step //= 2
    nb = batch * seq // step
    tok = lambda w_, cb: pl.BlockSpec((step, w_), lambda i: (i, cb))
    const = lambda shape: pl.BlockSpec(shape, lambda i: (0,) * len(shape))
    return pl.pallas_call(
        _sgu_kernel,
        grid=(nb,),
        in_specs=[
            tok(SGU_W, OFF_SGU_U // SGU_W),
            tok(SGU_W, OFF_SGU_V // SGU_W),
            const((1, SGU_W)),
            const((1, SGU_W)),
            const((SGU_GROUPS, CHUNK, CHUNK)),
            const((CHUNK, SGU_W)),
            const((CHUNK, CHUNK)),
        ],
        out_specs=pl.BlockSpec((step, SGU_W), lambda i: (i, 0)),
        out_shape=jax.ShapeDtypeStruct((batch * seq, SGU_W), BF16),
        compiler_params=pltpu.CompilerParams(
            dimension_semantics=("parallel",), vmem_limit_bytes=VMEM_LIMIT),
        name="sgu",
    )(proj, proj, lng, lnb, w, b_full, lvl)


def _out_proj_kernel(x_ref, a_ref, b_ref, c_ref, wa_ref, wb_ref, wc_ref, *rest, ride_steps):
    if ride_steps:
        wt_ref, o_ref, ow_ref = rest

        @pl.when(pl.program_id(0) * pl.num_programs(1) + pl.program_id(1) < ride_steps)
        def _():
            _relayout_kernel(wt_ref, ow_ref)
    else:
        o_ref, = rest
    o_ref[...] = (x_ref[...] + _dot(a_ref[...], wa_ref[...]) + _dot(b_ref[...], wb_ref[...])
                  + _dot(c_ref[...], wc_ref[...]))


def _out_proj(x2, mix_a, mix_b, mix_c, w_out, tm, tn, relayout=None):
    t, d = x2.shape
    ni = t // tm
    ride_in, ride_out, ride_shape, ride_args, ride_steps = [], [], [], [], 0
    if relayout is not None and relayout[0].shape[2] // LANE > (d // tn) * ni:
        x1, = _out_proj(x2, mix_a, mix_b, mix_c, w_out, tm, tn)
        return x1, _relayout_w_in(*relayout)
    if relayout is not None:
        w_t, layer = relayout
        ride_steps = w_t.shape[2] // LANE
        at = lambda j, i: jnp.minimum(j * ni + i, ride_steps - 1)
        ride_in = [pl.BlockSpec((None, w_t.shape[1], LANE), lambda j, i: (layer, 0, at(j, i)))]
        ride_out = [pl.BlockSpec((PROJ_W, LANE), lambda j, i: (0, at(j, i)))]
        ride_shape = [jax.ShapeDtypeStruct((PROJ_W, w_t.shape[2]), BF16)]
        ride_args = [w_t]
    return pl.pallas_call(
        functools.partial(_out_proj_kernel, ride_steps=ride_steps),
        grid=(d // tn, t // tm),
        in_specs=[
            pl.BlockSpec((tm, tn), lambda j, i: (i, j)),
            pl.BlockSpec((tm, V_W), lambda j, i: (i, 0)),
            pl.BlockSpec((tm, V_W), lambda j, i: (i, 0)),
            pl.BlockSpec((tm, SGU_W), lambda j, i: (i, 0)),
            pl.BlockSpec((V_W, tn), lambda j, i: (0, j)),
            pl.BlockSpec((V_W, tn), lambda j, i: (1, j)),
            pl.BlockSpec((SGU_W, tn), lambda j, i: (2 * V_W // SGU_W, j)),
        ] + ride_in,
        out_specs=[pl.BlockSpec((tm, tn), lambda j, i: (i, j))] + ride_out,
        out_shape=[jax.ShapeDtypeStruct((t, d), F32)] + ride_shape,
        compiler_params=pltpu.CompilerParams(
            dimension_semantics=("arbitrary", "arbitrary"), vmem_limit_bytes=VMEM_LIMIT),
        name="out_proj",
    )(x2, mix_a, mix_b, mix_c, w_out, w_out, w_out, *ride_args)


def _ffn_kernel(x_ref, g_ref, wg_ref, wu_ref, wd_ref, gf_ref, *rest, final_norm, plans):
    n = len(plans)
    o_ref, h_ref = rest[n], rest[2 * n + 1]
    j = pl.program_id(1)
    _run_casts(pl.program_id(0) * pl.num_programs(1) + j, plans, rest[0:n], rest[n + 1:2 * n + 1])

    @pl.when(j == 0)
    def _():
        x = x_ref[...]
        ms = jnp.mean(x * x, axis=-1, keepdims=True)
        h_ref[...] = (x * lax.rsqrt(ms + EPS) * g_ref[...]).astype(BF16)
        o_ref[...] = x

    h = h_ref[...]
    gate = _dot(h, wg_ref[...])
    up = _dot(h, wu_ref[...])
    act = (gate * jax.nn.sigmoid(gate) * up).astype(BF16)
    o_ref[...] += _dot(act, wd_ref[...])

    if final_norm:
        @pl.when(j == pl.num_programs(1) - 1)
        def _():
            y = o_ref[...]
            ms = jnp.mean(y * y, axis=-1, keepdims=True)
            o_ref[...] = y * lax.rsqrt(ms + EPS) * gf_ref[...]


def _ffn(x2, g, w_gu, w_down, g_final, tm, tf, final_norm, casts=()):
    t, d = x2.shape
    nf = D_FF // tf
    plans = [_cast_plan(cw, cl, (t // tm) * nf) for cw, cl in casts]
    c_in, c_out, c_shapes, c_args = _cast_specs(plans, lambda i, j: i * nf + j)
    return pl.pallas_call(
        functools.partial(_ffn_kernel, final_norm=final_norm, plans=plans),
        grid=(t // tm, nf),
        in_specs=[
            pl.BlockSpec((tm, d), lambda i, j: (i, 0)),
            pl.BlockSpec((1, d), lambda i, j: (0, 0)),
            pl.BlockSpec((d, tf), lambda i, j: (0, j)),
            pl.BlockSpec((d, tf), lambda i, j: (0, nf + j)),
            pl.BlockSpec((tf, d), lambda i, j: (j, 0)),
            pl.BlockSpec((1, d), lambda i, j: (0, 0)),
        ] + c_in,
        out_specs=[pl.BlockSpec((tm, d), lambda i, j: (i, 0))] + c_out,
        out_shape=[jax.ShapeDtypeStruct((t, d), F32)] + c_shapes,
        scratch_shapes=[pltpu.VMEM((tm, d), BF16)],
        compiler_params=pltpu.CompilerParams(
            dimension_semantics=("arbitrary", "arbitrary"), vmem_limit_bytes=VMEM_LIMIT),
        name="ffn",
    )(x2, g, w_gu, w_gu, w_down, g_final, *c_args)


class _Tiles(NamedTuple):
    tm: int
    tn_in: int
    tm_out: int
    tf: int


def _tiles(tokens):
    tm = 1024
    while tokens % tm:
        tm //= 2
    return _Tiles(tm=tm, tn_in=1536, tm_out=max(tm // 2, 8), tf=512)


def kernel(x, norm_mix, w_in, gla_a2, gla_ab, gla_norm, ml_conv, ml_ib, ml_fb, ml_norm,
           sgu_ln_g, sgu_ln_b, sgu_w, sgu_b, w_out, norm_ffn, w_gu, w_down, norm_final):
    batch, seq, d = x.shape
    depth = w_in.shape[0]
    tokens = batch * seq
    tiles = _tiles(tokens)

    lvl = jnp.asarray(_pair_level_matrix())
    ltri = jnp.asarray(np.tril(np.ones((CHUNK, CHUNK), np.float32)), BF16)

    w_in_t = jnp.swapaxes(w_in, 1, 2)
    w_in_b = _relayout_w_in(w_in_t, 0)
    dense = (w_out, w_gu, w_down)

    xc = x.reshape(tokens, d)
    for l in range(depth):
        if l == 0:
            proj, gates, w_out_b, w_gu_b, w_down_b = _in_proj(
                xc, norm_mix[l][None, :], w_in_b, tiles.tm, tiles.tn_in, casts=[(w, 0) for w in dense])
        else:
            proj, gates = _in_proj(xc, norm_mix[l][None, :], w_in_b, tiles.tm, tiles.tn_in)

        a2p = jnp.pad(_pad_heads(gla_a2[l]), ((0, LANE - GATE_RANK), (0, 0))).astype(BF16)
        abp = _pad_heads(gla_ab[l])[None, :]
        mix_a = _gla(proj, gates, a2p, abp, gla_norm[l][None, :], ltri, lvl, batch, seq)

        cw = jnp.concatenate([_pad_heads(ml_conv[l][:, :HEADS * DK]),
                              _pad_heads(ml_conv[l][:, HEADS * DK:])], axis=1)
        ifb = jnp.pad(jnp.concatenate([ml_ib[l], ml_fb[l]]), (0, LANE - 2 * HEADS))[None, :]
        mix_b = _mlstm(proj, gates, cw, ifb, ml_norm[l][None, :], ltri, lvl, batch, seq)

        b_full = jnp.repeat(sgu_b[l].T, SGU_CH, axis=1)
        mix_c = _sgu(proj, sgu_ln_g[l][None, :], sgu_ln_b[l][None, :], sgu_w[l], b_full, lvl, batch, seq)

        if l + 1 < depth:
            x1, w_in_b = _out_proj(xc, mix_a, mix_b, mix_c, w_out_b, tiles.tm_out, d, relayout=(w_in_t, l + 1))
        else:
            x1, = _out_proj(xc, mix_a, mix_b, mix_c, w_out_b, tiles.tm_out, d)
        nxt = [(w, l + 1) for w in dense] if l + 1 < depth else []
        xc, *cast = _ffn(x1, norm_ffn[l][None, :], w_gu_b, w_down_b, norm_final[None, :], tiles.tm,
                         tiles.tf, final_norm=(l == depth - 1), casts=nxt)
        if cast:
            w_out_b, w_gu_b, w_down_b = cast
    return xc.reshape(batch, seq, d)
```

```python
import functools
from typing import NamedTuple

import numpy as np
import jax
import jax.numpy as jnp
from jax import lax
from jax.experimental import pallas as pl
from jax.experimental.pallas import tpu as pltpu

F32 = jnp.float32
BF16 = jnp.bfloat16

HEADS = 4
DK = 96
DKP = 128
DV = 192
QK_W = HEADS * DKP
V_W = HEADS * DV
PAIR_K = 2 * DKP
PAIR_V = 2 * DV
SGU_W = 512
SGU_GROUPS = 4
SGU_CH = 128
GATE_RANK = 16
GATE_TAU = 16.0
CONV_WIDTH = 4
D_FF = 5632
EPS = 1e-6
CHUNK = 128
N_LEVELS = 7
LANE = 128

OFF_GLA_Q, OFF_GLA_K, OFF_ML_Q, OFF_ML_K = 0, 512, 1024, 1536
OFF_SGU_U, OFF_SGU_V = 2048, 2560
OFF_GLA_V, OFF_GLA_G, OFF_ML_V, OFF_ML_O = 3072, 3840, 4608, 5376
MAIN_W = 6144
OFF_GLA_A1, OFF_ML_IF = 6144, 6272
GATE_W = 2 * LANE
PROJ_W = MAIN_W + GATE_W
SGU_STEP = 4

VMEM_LIMIT = 58 * 1024 * 1024


def _column_moves():
    gk = HEADS * DK
    sizes = (gk, gk, V_W, V_W, GATE_RANK, gk, gk, V_W, V_W, HEADS, HEADS, SGU_W, SGU_W)
    starts = np.concatenate([[0], np.cumsum(sizes)]).tolist()
    (gq, gkk, gv, gg, ga1, mq, mk, mv, mo, mi, mf, su, sv) = starts[:-1]
    moves = []
    for dst, src in ((OFF_GLA_Q, gq), (OFF_GLA_K, gkk), (OFF_ML_Q, mq), (OFF_ML_K, mk)):
        moves += [(dst + h * DKP, src + h * DK, DK) for h in range(HEADS)]
    moves += [(OFF_SGU_U, su, SGU_W), (OFF_SGU_V, sv, SGU_W), (OFF_GLA_V, gv, V_W), (OFF_GLA_G, gg, V_W),
              (OFF_ML_V, mv, V_W), (OFF_ML_O, mo, V_W), (OFF_GLA_A1, ga1, GATE_RANK),
              (OFF_ML_IF, mi, 2 * HEADS)]
    assert mf == mi + HEADS and starts[-1] == sum(sizes)
    return moves


def _relayout_kernel(w_ref, o_ref):
    o_ref[...] = jnp.zeros_like(o_ref)
    for dst, src, n in _column_moves():
        if n % 16:
            pad = jnp.zeros((16 - n % 16, w_ref.shape[1]), F32)
            o_ref[dst:dst + n + pad.shape[0], :] = jnp.concatenate([w_ref[src:src + n, :], pad]).astype(BF16)
        else:
            o_ref[dst:dst + n, :] = w_ref[src:src + n, :].astype(BF16)


def _relayout_w_in(w_t, layer, cols=256):
    _, p, d = w_t.shape
    return pl.pallas_call(
        _relayout_kernel,
        grid=(d // cols,),
        in_specs=[pl.BlockSpec((None, p, cols), lambda i: (layer, 0, i))],
        out_specs=pl.BlockSpec((PROJ_W, cols), lambda i: (0, i)),
        out_shape=jax.ShapeDtypeStruct((PROJ_W, d), BF16),
        compiler_params=pltpu.CompilerParams(
            dimension_semantics=("parallel",), vmem_limit_bytes=VMEM_LIMIT),
        name="relayout",
    )(w_t)


def _pad_heads(a):
    lead = a.shape[:-1]
    a = a.reshape(lead + (HEADS, DK))
    a = jnp.pad(a, [(0, 0)] * len(lead) + [(0, 0), (0, DKP - DK)])
    return a.reshape(lead + (QK_W,))


def _pair_level_matrix():
    c = CHUNK
    t = np.arange(c)[:, None]
    s = np.arange(c)[None, :]
    x = np.bitwise_xor(t, s)
    lvl = np.floor(np.log2(np.maximum(x, 1))).astype(np.int32)
    lvl = np.where(s == t, -1, lvl)
    lvl = np.where(s > t, -2, lvl)
    return lvl.astype(np.int32)


def _dot(a, b):
    return jnp.dot(a, b, preferred_element_type=F32)


def _dot_nt(a, b):
    return lax.dot_general(a, b, (((1,), (1,)), ((), ())), preferred_element_type=F32)


def _split3(x):
    hi = x.astype(BF16)
    r = x - hi.astype(F32)
    mid = r.astype(BF16)
    lo = (r - mid.astype(F32)).astype(BF16)
    return hi, mid, lo


LOG2E = 1.4426950408889634
LN2 = 0.6931471805599453


def _log2_sigmoid(x):
    xl = x * LOG2E
    return jnp.minimum(xl, 0.0) - jnp.log2(1.0 + jnp.exp2(-jnp.abs(xl)))


def _log_sigmoid(x):
    return _log2_sigmoid(x) * LN2


def _pair_cols(c0, c1, rows):
    first = lax.broadcasted_iota(jnp.int32, (rows, LANE), 1) < DV - LANE
    b0 = jnp.broadcast_to(c0, (rows, LANE))
    b1 = jnp.broadcast_to(c1, (rows, LANE))
    return jnp.concatenate([b0, jnp.where(first, b0, b1), b1], axis=1)


def _head_rms_scale(o):
    rows = o.shape[0]
    o2 = o * o
    first = lax.broadcasted_iota(jnp.int32, (rows, LANE), 1) < DV - LANE
    mid = o2[:, LANE:2 * LANE]
    ss0 = jnp.sum(o2[:, 0:LANE] + jnp.where(first, mid, 0.0), axis=-1, keepdims=True)
    ss1 = jnp.sum(o2[:, 2 * LANE:3 * LANE] + jnp.where(first, 0.0, mid), axis=-1, keepdims=True)
    return _pair_cols(lax.rsqrt(ss0 / DV + EPS), lax.rsqrt(ss1 / DV + EPS), rows)


def _cast_plan(w, layer, steps):
    rows, cols = w.shape[1], w.shape[2]
    nblk = max(n for n in range(1, steps + 1) if rows % n == 0 and (rows // n) % 16 == 0)
    return w, layer, rows, cols, nblk


def _cast_specs(plans, step_of):
    in_specs, out_specs, out_shapes, args = [], [], [], []
    for w, layer, rows, cols, nblk in plans:
        blk = lambda *g, nblk=nblk: jnp.minimum(step_of(*g), nblk - 1)
        in_specs.append(pl.BlockSpec((None, rows // nblk, cols), lambda *g, blk=blk, layer=layer: (layer, blk(*g), 0)))
        out_specs.append(pl.BlockSpec((rows // nblk, cols), lambda *g, blk=blk: (blk(*g), 0)))
        out_shapes.append(jax.ShapeDtypeStruct((rows, cols), BF16))
        args.append(w)
    return in_specs, out_specs, out_shapes, args


def _run_casts(step, plans, src_refs, dst_refs):
    for (_, _, _, _, nblk), src, dst in zip(plans, src_refs, dst_refs):
        @pl.when(step < nblk)
        def _(src=src, dst=dst):
            dst[...] = src[...].astype(BF16)


def _in_proj_kernel(x_ref, g_ref, w_ref, wg_ref, *rest, plans):
    n = len(plans)
    o_ref, og_ref, h_ref = rest[n], rest[n + 1], rest[2 * n + 2]
    _run_casts(pl.program_id(0) * pl.num_programs(1) + pl.program_id(1), plans,
               rest[0:n], rest[n + 2:2 * n + 2])

    @pl.when(pl.program_id(1) == 0)
    def _():
        x = x_ref[...]
        ms = jnp.mean(x * x, axis=-1, keepdims=True)
        h_ref[...] = (x * lax.rsqrt(ms + EPS) * g_ref[...]).astype(BF16)
        og_ref[...] = _dot_nt(h_ref[...], wg_ref[...])

    o_ref[...] = _dot_nt(h_ref[...], w_ref[...]).astype(o_ref.dtype)


def _in_proj(x2, g, w, tm, tn, casts=()):
    t, d = x2.shape
    ni, nj = t // tm, MAIN_W // tn
    plans = [_cast_plan(cw, cl, ni * nj) for cw, cl in casts]
    c_in, c_out, c_shapes, c_args = _cast_specs(plans, lambda i, j: i * nj + j)
    return pl.pallas_call(
        functools.partial(_in_proj_kernel, plans=plans),
        grid=(ni, nj),
        in_specs=[
            pl.BlockSpec((tm, d), lambda i, j: (i, 0)),
            pl.BlockSpec((1, d), lambda i, j: (0, 0)),
            pl.BlockSpec((tn, d), lambda i, j: (j, 0)),
            pl.BlockSpec((GATE_W, d), lambda i, j: (MAIN_W // GATE_W, 0)),
        ] + c_in,
        out_specs=[pl.BlockSpec((tm, tn), lambda i, j: (i, j)),
                   pl.BlockSpec((tm, GATE_W), lambda i, j: (i, 0))] + c_out,
        out_shape=[jax.ShapeDtypeStruct((t, MAIN_W), BF16),
                   jax.ShapeDtypeStruct((t, GATE_W), F32)] + c_shapes,
        scratch_shapes=[pltpu.VMEM((tm, d), BF16)],
        compiler_params=pltpu.CompilerParams(
            dimension_semantics=("arbitrary", "arbitrary"), vmem_limit_bytes=VMEM_LIMIT),
        name="in_proj",
    )(x2, g, w, w, *c_args)


GLA_NSEQ = 2
GLA_SPLIT = 2


def _gla_kernel(qk_ref, vg_ref, a1_ref, a2_ref, ab_ref, nrm_ref, ltri_ref, lvl_ref,
                o_ref, s_ref, cum_ref, la_ref):
    @pl.when(pl.program_id(1) == 0)
    def _():
        s_ref[...] = jnp.zeros_like(s_ref)
        la_ref[0:8, :] = jnp.zeros((8, la_ref.shape[1]), F32)
        la_ref[8 + CHUNK:16 + CHUNK, :] = jnp.zeros((8, la_ref.shape[1]), F32)

    def chunk(c, carry):
        rows = pl.ds(pl.multiple_of(c * CHUNK, CHUNK), CHUNK)
        _gla_chunk(rows, qk_ref, vg_ref, a1_ref, a2_ref, ab_ref, nrm_ref, ltri_ref, lvl_ref,
                   o_ref, s_ref, cum_ref, la_ref)
        return carry

    lax.fori_loop(0, qk_ref.shape[1] // CHUNK, chunk, 0)


def _gla_chunk(rows, qk_ref, vg_ref, a1_ref, a2_ref, ab_ref, nrm_ref, ltri_ref, lvl_ref,
               o_ref, s_ref, cum_ref, la_ref):
    nseq = qk_ref.shape[0]
    width = nseq * QK_W
    n_heads = nseq * HEADS
    side = lambda parts: jnp.concatenate(parts, axis=1)
    q = side([qk_ref[i, rows, 0:QK_W] for i in range(nseq)]).astype(F32) * (DK ** -0.5)
    k = side([qk_ref[i, rows, QK_W:2 * QK_W] for i in range(nseq)]).astype(F32)
    z = side([_dot(a1_ref[i, rows, :].astype(BF16), a2_ref[...]) + ab_ref[...] for i in range(nseq)])
    log_a = _log2_sigmoid(z) * (1.0 / GATE_TAU)
    la_ref[8:8 + CHUNK, :] = log_a
    ltri = ltri_ref[...]
    hi, mid, lo = _split3(log_a)
    cum = _dot(ltri, hi) + _dot(ltri, mid) + _dot(ltri, lo)
    cum_ref[...] = cum

    lvl = lvl_ref[...]
    groups = CHUNK // 8
    rows8 = lambda g: slice(8 * g, 8 * g + 8)
    heads = [slice(h * DKP, (h + 1) * DKP) for h in range(n_heads)]
    row = lax.broadcasted_iota(jnp.int32, (CHUNK, width), 0)

    def scores(xq, yk):
        return [_dot_nt(xq[:, hs], yk[:, hs]) for hs in heads]

    qb = q.astype(BF16)
    kb = k.astype(BF16)
    attn = [[jnp.where(lvl[rows8(g), :] == -1, p[rows8(g), :], 0.0) for g in range(groups)]
            for p in scores(qb, kb)]

    def merge(level, parts, q_groups):
        for h in range(n_heads):
            for i, g in enumerate(q_groups):
                attn[h][g] = jnp.where(lvl[rows8(g), :] == level, parts[h][rows8(i), :], attn[h][g])

    e = jnp.exp2(jnp.where((row & 1) == 1, log_a, 0.0))
    merge(0, scores((q * e).astype(BF16), kb), range(groups))
    nxt = la_ref[pl.ds(9, CHUNK), :]
    prv = la_ref[pl.ds(7, CHUNK), :]
    r4 = row & 3
    e = jnp.exp2(jnp.where(r4 == 0, nxt, jnp.where(r4 == 1, 0.0, jnp.where(r4 == 2, log_a, log_a + prv))))
    merge(1, scores((q * e).astype(BF16), (k * e).astype(BF16)), range(groups))
    sub8 = lax.broadcasted_iota(jnp.int32, (8, width), 0)
    pieces = []
    for g in range(groups):
        d = cum[rows8(g), :] - cum_ref[8 * g + 3:8 * g + 4, :]
        pieces.append(jnp.where(sub8 < 4, -d, d))
    e = jnp.exp2(jnp.concatenate(pieces, axis=0))
    merge(2, scores((q * e).astype(BF16), (k * e).astype(BF16)), range(groups))
    for level in range(3, N_LEVELS):
        m = 1 << level
        xq, yk, q_groups = [], [], []
        for base in range(0, CHUNK, 2 * m):
            k_rows = slice(base, base + m)
            q_rows = slice(base + m, base + 2 * m)
            edge = cum_ref[base + m - 1:base + m, :]
            yk += [k[k_rows, :] * jnp.exp2(edge - cum[k_rows, :]), k[q_rows, :]]
            xq.append(q[q_rows, :] * jnp.exp2(cum[q_rows, :] - edge))
            q_groups += range((base + m) // 8, (base + 2 * m) // 8)
        merge(level, scores(jnp.concatenate(xq, axis=0).astype(BF16),
                            jnp.concatenate(yk, axis=0).astype(BF16)), q_groups)

    last = cum_ref[CHUNK - 1:CHUNK, :]
    q_dec = (q * jnp.exp2(cum)).astype(BF16)
    k_dec = k * jnp.exp2(last - cum)
    decay_all = jnp.exp2(last)

    lane = lax.broadcasted_iota(jnp.int32, (CHUNK, PAIR_V), 1)
    lo_mask = lane < DV

    for pp in range(n_heads // 2):
        seq_i, p = divmod(pp, HEADS // 2)
        ks = slice(pp * PAIR_K, (pp + 1) * PAIR_K)
        vs = slice(p * PAIR_V, (p + 1) * PAIR_V)
        vb = vg_ref[seq_i, rows, vs]
        zero = jnp.zeros_like(vb)
        v_lo = jnp.where(lo_mask, vb, zero)
        v_hi = jnp.where(lo_mask, zero, vb)
        v_blk = jnp.concatenate([v_lo, v_hi], axis=0)
        state = s_ref[pp]
        lhs = jnp.concatenate([jnp.concatenate(attn[2 * pp], axis=0).astype(BF16),
                               jnp.concatenate(attn[2 * pp + 1], axis=0).astype(BF16),
                               q_dec[:, ks]], axis=1)
        rhs = jnp.concatenate([v_blk, state.astype(BF16)], axis=0)
        o = _dot(lhs, rhs)

        kd_t = k_dec[:, ks].T.astype(BF16)
        upd = jnp.concatenate([_dot(kd_t[0:DKP, :], v_lo), _dot(kd_t[DKP:PAIR_K, :], v_hi)], axis=0)
        dcols = []
        for h in (2 * pp, 2 * pp + 1):
            d_row = jnp.broadcast_to(decay_all[:, h * DKP:(h + 1) * DKP], (DKP, DKP))
            d_col = d_row.T
            dcols.append(jnp.concatenate([d_col] * (PAIR_V // LANE), axis=1))
        s_ref[pp] = state * jnp.concatenate(dcols, axis=0) + upd

        gate = vg_ref[seq_i, rows, V_W + p * PAIR_V:V_W + (p + 1) * PAIR_V].astype(F32)
        out = o * _head_rms_scale(o) * nrm_ref[:, vs] * (gate * jax.nn.sigmoid(gate))
        o_ref[seq_i, rows, vs] = out.astype(o_ref.dtype)


def _gla(proj, gates, a2p, abp, nrm, ltri, lvl, batch, seq):
    nseq = GLA_NSEQ if batch % GLA_NSEQ == 0 else 1
    split = GLA_SPLIT if seq % (GLA_SPLIT * CHUNK) == 0 else 1
    part = seq // split
    grouped = lambda a: a.reshape(batch // nseq, nseq, split, part, a.shape[-1])
    tok = lambda w, cb: pl.BlockSpec((None, nseq, None, part, w), lambda b, c: (b, 0, c, 0, cb))
    const = lambda shape: pl.BlockSpec(shape, lambda b, c: (0,) * len(shape))
    assert OFF_GLA_K == OFF_GLA_Q + QK_W and OFF_GLA_G == OFF_GLA_V + V_W
    out = pl.pallas_call(
        _gla_kernel,
        grid=(batch // nseq, split),
        in_specs=[
            tok(2 * QK_W, OFF_GLA_Q // (2 * QK_W)),
            tok(2 * V_W, OFF_GLA_V // (2 * V_W)),
            tok(LANE, (OFF_GLA_A1 - MAIN_W) // LANE),
            const((LANE, QK_W)),
            const((1, QK_W)),
            const((1, V_W)),
            const((CHUNK, CHUNK)),
            const((CHUNK, CHUNK)),
        ],
        out_specs=tok(V_W, 0),
        out_shape=jax.ShapeDtypeStruct((batch // nseq, nseq, split, part, V_W), BF16),
        scratch_shapes=[
            pltpu.VMEM((nseq * HEADS // 2, PAIR_K, PAIR_V), F32),
            pltpu.VMEM((CHUNK, nseq * QK_W), F32),
            pltpu.VMEM((CHUNK + 16, nseq * QK_W), F32),
        ],
        compiler_params=pltpu.CompilerParams(
            dimension_semantics=("parallel", "arbitrary"), vmem_limit_bytes=VMEM_LIMIT),
        name="gla",
    )(grouped(proj), grouped(proj), grouped(gates), a2p, abp, nrm, ltri, lvl)
    return out.reshape(batch * seq, V_W)


ST_W = PAIR_V + LANE
TAIL = 8


def _mlstm_kernel(qk_ref, vo_ref, if_ref, cw_ref, ifb_ref, nrm_ref, ltri_ref, lvl_ref,
                  o_ref, c_ref, m_ref, xe_ref):
    c_ref[...] = jnp.zeros_like(c_ref)
    m_ref[...] = jnp.zeros_like(m_ref)
    xe_ref[0:TAIL, :] = jnp.zeros((TAIL, 2 * QK_W), F32)

    def chunk(c, carry):
        rows = pl.ds(pl.multiple_of(c * CHUNK, CHUNK), CHUNK)
        _mlstm_chunk(rows, qk_ref, vo_ref, if_ref, cw_ref, ifb_ref, nrm_ref, ltri_ref, lvl_ref,
                     o_ref, c_ref, m_ref, xe_ref)
        return carry

    lax.fori_loop(0, qk_ref.shape[0] // CHUNK, chunk, 0, unroll=2)


def _mlstm_chunk(rows, qk_ref, vo_ref, if_ref, cw_ref, ifb_ref, nrm_ref, ltri_ref, lvl_ref,
                 o_ref, c_ref, m_ref, xe_ref):
    xe_ref[TAIL:TAIL + CHUNK, :] = qk_ref[rows, :].astype(F32)
    y = jnp.zeros((CHUNK, 2 * QK_W), F32)
    for j in range(CONV_WIDTH):
        y = y + cw_ref[j:j + 1, :] * xe_ref[pl.ds(TAIL - (CONV_WIDTH - 1) + j, CHUNK), :]
    xe_ref[0:TAIL, :] = xe_ref[CHUNK:CHUNK + TAIL, :]
    y = y * jax.nn.sigmoid(y)
    qm = y[:, 0:QK_W]
    km = y[:, QK_W:2 * QK_W] * (DK ** -0.5)
    qb = qm.astype(BF16)
    kb = km.astype(BF16)
    km_t = km.T

    slab = if_ref[rows, :] + ifb_ref[...]
    lane_g = lax.broadcasted_iota(jnp.int32, (CHUNK, LANE), 1)
    gates = jnp.where(lane_g < HEADS, slab, _log_sigmoid(slab))
    ltri = ltri_ref[...]
    g_hi, g_mid, g_lo = _split3(gates)
    cum_col = _dot(ltri, g_hi) + _dot(ltri, g_mid) + _dot(ltri, g_lo)
    gates_t = gates.T[0:2 * HEADS, :]
    t_hi, t_mid, t_lo = _split3(gates_t)
    cum_row = _dot_nt(t_hi, ltri) + _dot_nt(t_mid, ltri) + _dot_nt(t_lo, ltri)

    lvl = lvl_ref[...]
    causal = lvl >= -1
    lane = lax.broadcasted_iota(jnp.int32, (CHUNK, PAIR_V), 1)
    lo_mask = lane < DV
    lane_e = lax.broadcasted_iota(jnp.int32, (CHUNK, LANE), 1)
    one_col = [jnp.where(lane_e == i, 1.0, 0.0).astype(BF16) for i in range(2)]
    lane_r = lax.broadcasted_iota(jnp.int32, (PAIR_K, LANE), 1)
    row_r = lax.broadcasted_iota(jnp.int32, (PAIR_K, LANE), 0)
    ones_blk = jnp.where(lane_r == jnp.where(row_r < CHUNK, 0, 1), 1.0, 0.0).astype(BF16)

    for p in range(HEADS // 2):
        ks = slice(p * PAIR_K, (p + 1) * PAIR_K)
        vs = slice(p * PAIR_V, (p + 1) * PAIR_V)
        vb = vo_ref[rows, vs]
        zero = jnp.zeros_like(vb)
        sc_parts, qs_parts, floor_parts, kw_parts, dprev_parts = [], [], [], [], []
        for h in (2 * p, 2 * p + 1):
            hs = slice(h * DKP, (h + 1) * DKP)
            cc = jnp.broadcast_to(cum_col[:, HEADS + h:HEADS + h + 1], (CHUNK, CHUNK))
            cum_r = cum_row[HEADS + h:HEADS + h + 1, :]
            ib_r = gates_t[h:h + 1, :]
            m_prev = m_ref[h:h + 1, :]
            dmat = jnp.where(causal, cc - cum_r + ib_r, -jnp.inf)
            inter = cc + m_prev
            m_t = jnp.maximum(inter, jnp.max(dmat, axis=-1, keepdims=True))
            w = jnp.exp(dmat - m_t)
            sc_inter = jnp.exp(inter - m_t)
            sc_parts.append((_dot_nt(qb[:, hs], kb[:, hs]) * w).astype(BF16))
            qs_parts.append((qm[:, hs] * sc_inter).astype(BF16))
            floor_parts.append(jnp.exp(-m_t))
            total = cum_r[:, CHUNK - 1:CHUNK]
            g_row = total - cum_r + ib_r
            m_new = jnp.maximum(total + m_prev, jnp.max(g_row, axis=-1, keepdims=True))
            wj = jnp.exp(g_row - m_new)
            dprev_parts.append(jnp.exp(total + m_prev - m_new))
            kw_parts.append((km_t[hs, :] * wj).astype(BF16))
            m_ref[h:h + 1, :] = m_new

        state = c_ref[p]
        st_m = state.astype(BF16)
        v_lo = jnp.where(lo_mask, vb, zero)
        v_hi = jnp.where(lo_mask, zero, vb)
        v_blk = jnp.concatenate([v_lo, v_hi], axis=0)
        rhs = jnp.concatenate([jnp.concatenate([v_blk, ones_blk], axis=1), st_m], axis=0)
        lhs = jnp.concatenate(sc_parts + qs_parts, axis=1)
        res = _dot(lhs, rhs)
        num = res[:, 0:PAIR_V]
        den = _pair_cols(res[:, PAIR_V:PAIR_V + 1], res[:, PAIR_V + 1:PAIR_V + 2], CHUNK)
        floor = _pair_cols(floor_parts[0][:, 0:1], floor_parts[1][:, 0:1], CHUNK)
        hid = num / jnp.maximum(jnp.abs(den), floor)

        upd = jnp.concatenate(
            [_dot(kw_parts[0], jnp.concatenate([v_lo, one_col[0]], axis=1)),
             _dot(kw_parts[1], jnp.concatenate([v_hi, one_col[1]], axis=1))], axis=0)
        d_rows = jnp.concatenate(
            [jnp.broadcast_to(jnp.concatenate([d] * (ST_W // LANE), axis=1), (DKP, ST_W))
             for d in dprev_parts], axis=0)
        c_ref[p] = d_rows * state + upd

        og = vo_ref[rows, V_W + p * PAIR_V:V_W + (p + 1) * PAIR_V].astype(F32)
        out = hid * _head_rms_scale(hid) * nrm_ref[:, vs] * jax.nn.sigmoid(og)
        o_ref[rows, vs] = out.astype(o_ref.dtype)


def _mlstm(proj, gates, cw, ifb, nrm, ltri, lvl, batch, seq):
    tok = lambda w, cb: pl.BlockSpec((seq, w), lambda b: (b, cb))
    const = lambda shape: pl.BlockSpec(shape, lambda b: (0,) * len(shape))
    assert OFF_ML_K == OFF_ML_Q + QK_W and OFF_ML_O == OFF_ML_V + V_W
    return pl.pallas_call(
        _mlstm_kernel,
        grid=(batch,),
        in_specs=[
            tok(2 * QK_W, OFF_ML_Q // (2 * QK_W)),
            tok(2 * V_W, OFF_ML_V // (2 * V_W)),
            tok(LANE, (OFF_ML_IF - MAIN_W) // LANE),
            const((CONV_WIDTH, 2 * QK_W)),
            const((1, LANE)),
            const((1, V_W)),
            const((CHUNK, CHUNK)),
            const((CHUNK, CHUNK)),
        ],
        out_specs=pl.BlockSpec((seq, V_W), lambda b: (b, 0)),
        out_shape=jax.ShapeDtypeStruct((batch * seq, V_W), BF16),
        scratch_shapes=[
            pltpu.VMEM((HEADS // 2, PAIR_K, ST_W), F32),
            pltpu.VMEM((2 * HEADS, LANE), F32),
            pltpu.VMEM((TAIL + CHUNK, 2 * QK_W), F32),
        ],
        compiler_params=pltpu.CompilerParams(
            dimension_semantics=("parallel",), vmem_limit_bytes=VMEM_LIMIT),
        name="mlstm",
    )(proj, proj, gates, cw, ifb, nrm, ltri, lvl)


def _sgu_kernel(u_ref, v_ref, lng_ref, lnb_ref, w_ref, b_ref, lvl_ref, o_ref):
    causal = lvl_ref[...] >= -1
    w_causal = [jnp.where(causal, w_ref[g], 0.0).astype(BF16) for g in range(SGU_GROUPS)]
    for n in range(u_ref.shape[0] // CHUNK):
        rows = slice(n * CHUNK, (n + 1) * CHUNK)
        u = jax.nn.gelu(u_ref[rows, :].astype(F32))
        v = jax.nn.gelu(v_ref[rows, :].astype(F32))
        mu = jnp.mean(v, axis=-1, keepdims=True)
        var = jnp.mean(jnp.square(v - mu), axis=-1, keepdims=True)
        vn = ((v - mu) * lax.rsqrt(var + EPS) * lng_ref[...] + lnb_ref[...]).astype(BF16)
        for g in range(SGU_GROUPS):
            gs = slice(g * SGU_CH, (g + 1) * SGU_CH)
            mixed = _dot(w_causal[g], vn[:, gs]) + b_ref[:, gs]
            o_ref[rows, gs] = (u[:, gs] * mixed).astype(o_ref.dtype)


def _sgu(proj, lng, lnb, w, b_full, lvl, batch, seq):
    step = CHUNK * SGU_STEP
    while (batch * seq) % step:
        step //= 2
    nb = batch * seq // step
    tok = lambda w_, cb: pl.BlockSpec((step, w_), lambda i: (i, cb))
    const = lambda shape: pl.BlockSpec(shape, lambda i: (0,) * len(shape))
    return pl.pallas_call(
        _sgu_kernel,
        grid=(nb,),
        in_specs=[
            tok(SGU_W, OFF_SGU_U // SGU_W),
            tok(SGU_W, OFF_SGU_V // SGU_W),
            const((1, SGU_W)),
            const((1, SGU_W)),
            const((SGU_GROUPS, CHUNK, CHUNK)),
            const((CHUNK, SGU_W)),
            const((CHUNK, CHUNK)),
        ],
        out_specs=pl.BlockSpec((step, SGU_W), lambda i: (i, 0)),
        out_shape=jax.ShapeDtypeStruct((batch * seq, SGU_W), BF16),
        compiler_params=pltpu.CompilerParams(
            dimension_semantics=("parallel",), vmem_limit_bytes=VMEM_LIMIT),
        name="sgu",
    )(proj, proj, lng, lnb, w, b_full, lvl)


def _out_proj_kernel(x_ref, a_ref, b_ref, c_ref, wa_ref, wb_ref, wc_ref, *rest, ride_steps):
    if ride_steps:
        wt_ref, o_ref, ow_ref = rest

        @pl.when(pl.program_id(0) * pl.num_programs(1) + pl.program_id(1) < ride_steps)
        def _():
            _relayout_kernel(wt_ref, ow_ref)
    else:
        o_ref, = rest
    o_ref[...] = (x_ref[...] + _dot(a_ref[...], wa_ref[...]) + _dot(b_ref[...], wb_ref[...])
                  + _dot(c_ref[...], wc_ref[...]))


def _out_proj(x2, mix_a, mix_b, mix_c, w_out, tm, tn, relayout=None):
    t, d = x2.shape
    ni = t // tm
    ride_in, ride_out, ride_shape, ride_args, ride_steps = [], [], [], [], 0
    if relayout is not None and relayout[0].shape[2] // LANE > (d // tn) * ni:
        x1, = _out_proj(x2, mix_a, mix_b, mix_c, w_out, tm, tn)
        return x1, _relayout_w_in(*relayout)
    if relayout is not None:
        w_t, layer = relayout
        ride_steps = w_t.shape[2] // LANE
        at = lambda j, i: jnp.minimum(j * ni + i, ride_steps - 1)
        ride_in = [pl.BlockSpec((None, w_t.shape[1], LANE), lambda j, i: (layer, 0, at(j, i)))]
        ride_out = [pl.BlockSpec((PROJ_W, LANE), lambda j, i: (0, at(j, i)))]
        ride_shape = [jax.ShapeDtypeStruct((PROJ_W, w_t.shape[2]), BF16)]
        ride_args = [w_t]
    once = dict(pipeline_mode=pl.Buffered(1)) if tn == d else {}
    return pl.pallas_call(
        functools.partial(_out_proj_kernel, ride_steps=ride_steps),
        grid=(d // tn, t // tm),
        in_specs=[
            pl.BlockSpec((tm, tn), lambda j, i: (i, j)),
            pl.BlockSpec((tm, V_W), lambda j, i: (i, 0)),
            pl.BlockSpec((tm, V_W), lambda j, i: (i, 0)),
            pl.BlockSpec((tm, SGU_W), lambda j, i: (i, 0)),
            pl.BlockSpec((V_W, tn), lambda j, i: (0, j), **once),
            pl.BlockSpec((V_W, tn), lambda j, i: (1, j), **once),
            pl.BlockSpec((SGU_W, tn), lambda j, i: (2 * V_W // SGU_W, j), **once),
        ] + ride_in,
        out_specs=[pl.BlockSpec((tm, tn), lambda j, i: (i, j))] + ride_out,
        out_shape=[jax.ShapeDtypeStruct((t, d), F32)] + ride_shape,
        compiler_params=pltpu.CompilerParams(
            dimension_semantics=("arbitrary", "arbitrary"), vmem_limit_bytes=VMEM_LIMIT),
        name="out_proj",
    )(x2, mix_a, mix_b, mix_c, w_out, w_out, w_out, *ride_args)


def _ffn_kernel(x_ref, g_ref, wg_ref, wu_ref, wd_ref, gf_ref, *rest, final_norm, plans):
    n = len(plans)
    o_ref, h_ref = rest[n], rest[2 * n + 1]
    j = pl.program_id(1)
    _run_casts(pl.program_id(0) * pl.num_programs(1) + j, plans, rest[0:n], rest[n + 1:2 * n + 1])

    @pl.when(j == 0)
    def _():
        x = x_ref[...]
        ms = jnp.mean(x * x, axis=-1, keepdims=True)
        h_ref[...] = (x * lax.rsqrt(ms + EPS) * g_ref[...]).astype(BF16)
        o_ref[...] = x

    h = h_ref[...]
    gate = _dot(h, wg_ref[...])
    up = _dot(h, wu_ref[...])
    act = (gate * jax.nn.sigmoid(gate) * up).astype(BF16)
    o_ref[...] += _dot(act, wd_ref[...])

    if final_norm:
        @pl.when(j == pl.num_programs(1) - 1)
        def _():
            y = o_ref[...]
            ms = jnp.mean(y * y, axis=-1, keepdims=True)
            o_ref[...] = y * lax.rsqrt(ms + EPS) * gf_ref[...]


def _ffn(x2, g, w_gu, w_down, g_final, tm, tf, final_norm, casts=()):
    t, d = x2.shape
    nf = D_FF // tf
    plans = [_cast_plan(cw, cl, (t // tm) * nf) for cw, cl in casts]
    c_in, c_out, c_shapes, c_args = _cast_specs(plans, lambda i, j: i * nf + j)
    return pl.pallas_call(
        functools.partial(_ffn_kernel, final_norm=final_norm, plans=plans),
        grid=(t // tm, nf),
        in_specs=[
            pl.BlockSpec((tm, d), lambda i, j: (i, 0)),
            pl.BlockSpec((1, d), lambda i, j: (0, 0)),
            pl.BlockSpec((d, tf), lambda i, j: (0, j)),
            pl.BlockSpec((d, tf), lambda i, j: (0, nf + j)),
            pl.BlockSpec((tf, d), lambda i, j: (j, 0)),
            pl.BlockSpec((1, d), lambda i, j: (0, 0)),
        ] + c_in,
        out_specs=[pl.BlockSpec((tm, d), lambda i, j: (i, 0))] + c_out,
        out_shape=[jax.ShapeDtypeStruct((t, d), F32)] + c_shapes,
        scratch_shapes=[pltpu.VMEM((tm, d), BF16)],
        compiler_params=pltpu.CompilerParams(
            dimension_semantics=("arbitrary", "arbitrary"), vmem_limit_bytes=VMEM_LIMIT),
        name="ffn",
    )(x2, g, w_gu, w_gu, w_down, g_final, *c_args)


class _Tiles(NamedTuple):
    tm: int
    tn_in: int
    tm_out: int
    tf: int


def _tiles(tokens):
    tm = 1024
    while tokens % tm:
        tm //= 2
    return _Tiles(tm=tm, tn_in=1536, tm_out=max(tm // 2, 8), tf=512)


def kernel(x, norm_mix, w_in, gla_a2, gla_ab, gla_norm, ml_conv, ml_ib, ml_fb, ml_norm,
           sgu_ln_g, sgu_ln_b, sgu_w, sgu_b, w_out, norm_ffn, w_gu, w_down, norm_final):
    batch, seq, d = x.shape
    depth = w_in.shape[0]
    tokens = batch * seq
    tiles = _tiles(tokens)

    lvl = jnp.asarray(_pair_level_matrix())
    ltri = jnp.asarray(np.tril(np.ones((CHUNK, CHUNK), np.float32)), BF16)

    w_in_t = jnp.swapaxes(w_in, 1, 2)
    w_in_b = _relayout_w_in(w_in_t, 0)
    dense = (w_out, w_gu, w_down)

    xc = x.reshape(tokens, d)
    for l in range(depth):
        if l == 0:
            proj, gates, w_out_b, w_gu_b, w_down_b = _in_proj(
                xc, norm_mix[l][None, :], w_in_b, tiles.tm, tiles.tn_in, casts=[(w, 0) for w in dense])
        else:
            proj, gates = _in_proj(xc, norm_mix[l][None, :], w_in_b, tiles.tm, tiles.tn_in)

        a2p = jnp.pad(_pad_heads(gla_a2[l]), ((0, LANE - GATE_RANK), (0, 0))).astype(BF16)
        abp = _pad_heads(gla_ab[l])[None, :]
        mix_a = _gla(proj, gates, a2p, abp, gla_norm[l][None, :], ltri, lvl, batch, seq)

        cw = jnp.concatenate([_pad_heads(ml_conv[l][:, :HEADS * DK]),
                              _pad_heads(ml_conv[l][:, HEADS * DK:])], axis=1)
        ifb = jnp.pad(jnp.concatenate([ml_ib[l], ml_fb[l]]), (0, LANE - 2 * HEADS))[None, :]
        mix_b = _mlstm(proj, gates, cw, ifb, ml_norm[l][None, :], ltri, lvl, batch, seq)

        b_full = jnp.repeat(sgu_b[l].T, SGU_CH, axis=1)
        mix_c = _sgu(proj, sgu_ln_g[l][None, :], sgu_ln_b[l][None, :], sgu_w[l], b_full, lvl, batch, seq)

        if l + 1 < depth:
            x1, w_in_b = _out_proj(xc, mix_a, mix_b, mix_c, w_out_b, tiles.tm_out, d, relayout=(w_in_t, l + 1))
        else:
            x1, = _out_proj(xc, mix_a, mix_b, mix_c, w_out_b, tiles.tm, d)
        nxt = [(w, l + 1) for w in dense] if l + 1 < depth else []
        xc, *cast = _ffn(x1, norm_ffn[l][None, :], w_gu_b, w_down_b, norm_final[None, :], tiles.tm,
                         tiles.tf, final_norm=(l == depth - 1), casts=nxt)
        if cast:
            w_out_b, w_gu_b, w_down_b = cast
    return xc.reshape(batch, seq, d)
```

```python
import functools
from typing import NamedTuple

import numpy as np
import jax
import jax.numpy as jnp
from jax import lax
from jax.experimental import pallas as pl
from jax.experimental.pallas import tpu as pltpu

F32 = jnp.float32
BF16 = jnp.bfloat16

HEADS = 4
DK = 96
DKP = 128
DV = 192
QK_W = HEADS * DKP
V_W = HEADS * DV
PAIR_K = 2 * DKP
PAIR_V = 2 * DV
SGU_W = 512
SGU_GROUPS = 4
SGU_CH = 128
GATE_RANK = 16
GATE_TAU = 16.0
CONV_WIDTH = 4
D_FF = 5632
EPS = 1e-6
CHUNK = 128
N_LEVELS = 7
LANE = 128

OFF_GLA_Q, OFF_GLA_K, OFF_ML_Q, OFF_ML_K = 0, 512, 1024, 1536
OFF_SGU_U, OFF_SGU_V = 2048, 2560
OFF_GLA_V, OFF_GLA_G, OFF_ML_V, OFF_ML_O = 3072, 3840, 4608, 5376
MAIN_W = 6144
OFF_GLA_A1, OFF_ML_IF = 6144, 6272
GATE_W = 2 * LANE
PROJ_W = MAIN_W + GATE_W
SGU_STEP = 4

VMEM_LIMIT = 58 * 1024 * 1024


def _column_moves():
    gk = HEADS * DK
    sizes = (gk, gk, V_W, V_W, GATE_RANK, gk, gk, V_W, V_W, HEADS, HEADS, SGU_W, SGU_W)
    starts = np.concatenate([[0], np.cumsum(sizes)]).tolist()
    (gq, gkk, gv, gg, ga1, mq, mk, mv, mo, mi, mf, su, sv) = starts[:-1]
    moves = []
    for dst, src in ((OFF_GLA_Q, gq), (OFF_GLA_K, gkk), (OFF_ML_Q, mq), (OFF_ML_K, mk)):
        moves += [(dst + h * DKP, src + h * DK, DK) for h in range(HEADS)]
    moves += [(OFF_SGU_U, su, SGU_W), (OFF_SGU_V, sv, SGU_W), (OFF_GLA_V, gv, V_W), (OFF_GLA_G, gg, V_W),
              (OFF_ML_V, mv, V_W), (OFF_ML_O, mo, V_W), (OFF_GLA_A1, ga1, GATE_RANK),
              (OFF_ML_IF, mi, 2 * HEADS)]
    assert mf == mi + HEADS and starts[-1] == sum(sizes)
    return moves


def _relayout_kernel(w_ref, o_ref):
    o_ref[...] = jnp.zeros_like(o_ref)
    for dst, src, n in _column_moves():
        if n % 16:
            pad = jnp.zeros((16 - n % 16, w_ref.shape[1]), F32)
            o_ref[dst:dst + n + pad.shape[0], :] = jnp.concatenate([w_ref[src:src + n, :], pad]).astype(BF16)
        else:
            o_ref[dst:dst + n, :] = w_ref[src:src + n, :].astype(BF16)


def _relayout_w_in(w_t, layer, cols=256):
    _, p, d = w_t.shape
    return pl.pallas_call(
        _relayout_kernel,
        grid=(d // cols,),
        in_specs=[pl.BlockSpec((None, p, cols), lambda i: (layer, 0, i))],
        out_specs=pl.BlockSpec((PROJ_W, cols), lambda i: (0, i)),
        out_shape=jax.ShapeDtypeStruct((PROJ_W, d), BF16),
        compiler_params=pltpu.CompilerParams(
            dimension_semantics=("parallel",), vmem_limit_bytes=VMEM_LIMIT),
        name="relayout",
    )(w_t)


def _pad_heads(a):
    lead = a.shape[:-1]
    a = a.reshape(lead + (HEADS, DK))
    a = jnp.pad(a, [(0, 0)] * len(lead) + [(0, 0), (0, DKP - DK)])
    return a.reshape(lead + (QK_W,))


def _pair_level_matrix():
    c = CHUNK
    t = np.arange(c)[:, None]
    s = np.arange(c)[None, :]
    x = np.bitwise_xor(t, s)
    lvl = np.floor(np.log2(np.maximum(x, 1))).astype(np.int32)
    lvl = np.where(s == t, -1, lvl)
    lvl = np.where(s > t, -2, lvl)
    return lvl.astype(np.int32)


def _dot(a, b):
    return jnp.dot(a, b, preferred_element_type=F32)


def _dot_nt(a, b):
    return lax.dot_general(a, b, (((1,), (1,)), ((), ())), preferred_element_type=F32)


def _split3(x):
    hi = x.astype(BF16)
    r = x - hi.astype(F32)
    mid = r.astype(BF16)
    lo = (r - mid.astype(F32)).astype(BF16)
    return hi, mid, lo


LOG2E = 1.4426950408889634
LN2 = 0.6931471805599453


def _log2_sigmoid(x):
    xl = x * LOG2E
    return jnp.minimum(xl, 0.0) - jnp.log2(1.0 + jnp.exp2(-jnp.abs(xl)))


def _log_sigmoid(x):
    return _log2_sigmoid(x) * LN2


def _pair_cols(c0, c1, rows):
    first = lax.broadcasted_iota(jnp.int32, (rows, LANE), 1) < DV - LANE
    b0 = jnp.broadcast_to(c0, (rows, LANE))
    b1 = jnp.broadcast_to(c1, (rows, LANE))
    return jnp.concatenate([b0, jnp.where(first, b0, b1), b1], axis=1)


def _head_rms_scale(o):
    rows = o.shape[0]
    o2 = o * o
    first = lax.broadcasted_iota(jnp.int32, (rows, LANE), 1) < DV - LANE
    mid = o2[:, LANE:2 * LANE]
    ss0 = jnp.sum(o2[:, 0:LANE] + jnp.where(first, mid, 0.0), axis=-1, keepdims=True)
    ss1 = jnp.sum(o2[:, 2 * LANE:3 * LANE] + jnp.where(first, 0.0, mid), axis=-1, keepdims=True)
    return _pair_cols(lax.rsqrt(ss0 / DV + EPS), lax.rsqrt(ss1 / DV + EPS), rows)


def _cast_plan(w, layer, steps, tile=0):
    rows, cols = w.shape[1], w.shape[2]
    nblk = max(n for n in range(1, steps + 1) if rows % n == 0 and (rows // n) % 16 == 0)
    return w, layer, rows, cols, nblk, tile


def _cast_specs(plans, step_of):
    in_specs, out_specs, out_shapes, args = [], [], [], []
    for w, layer, rows, cols, nblk, _ in plans:
        blk = lambda *g, nblk=nblk: jnp.minimum(step_of(*g), nblk - 1)
        in_specs.append(pl.BlockSpec((None, rows // nblk, cols), lambda *g, blk=blk, layer=layer: (layer, blk(*g), 0)))
        out_specs.append(pl.BlockSpec((rows // nblk, cols), lambda *g, blk=blk: (blk(*g), 0)))
        out_shapes.append(jax.ShapeDtypeStruct((rows, cols), BF16))
        args.append(w)
    return in_specs, out_specs, out_shapes, args


def _run_casts(step, plans, src_refs, dst_refs):
    for (_, _, _, cols, nblk, tile), src, dst in zip(plans, src_refs, dst_refs):
        @pl.when(step < nblk)
        def _(src=src, dst=dst, cols=cols, tile=tile):
            if not tile:
                dst[...] = src[...].astype(BF16)
            else:
                half = cols // 2
                for j in range(half // tile):
                    dst[:, 2 * j * tile:(2 * j + 1) * tile] = src[:, j * tile:(j + 1) * tile].astype(BF16)
                    dst[:, (2 * j + 1) * tile:(2 * j + 2) * tile] = (
                        src[:, half + j * tile:half + (j + 1) * tile].astype(BF16))


def _in_proj_kernel(x_ref, g_ref, w_ref, wg_ref, *rest, plans):
    n = len(plans)
    o_ref, og_ref, h_ref = rest[n], rest[n + 1], rest[2 * n + 2]
    _run_casts(pl.program_id(0) * pl.num_programs(1) + pl.program_id(1), plans,
               rest[0:n], rest[n + 2:2 * n + 2])

    @pl.when(pl.program_id(1) == 0)
    def _():
        x = x_ref[...]
        ms = jnp.mean(x * x, axis=-1, keepdims=True)
        h_ref[...] = (x * lax.rsqrt(ms + EPS) * g_ref[...]).astype(BF16)
        og_ref[...] = _dot_nt(h_ref[...], wg_ref[...])

    o_ref[...] = _dot_nt(h_ref[...], w_ref[...]).astype(o_ref.dtype)


def _in_proj(x2, g, w, tm, tn, casts=()):
    t, d = x2.shape
    ni, nj = t // tm, MAIN_W // tn
    plans = [_cast_plan(cw, cl, ni * nj, *ct) for cw, cl, *ct in casts]
    c_in, c_out, c_shapes, c_args = _cast_specs(plans, lambda i, j: i * nj + j)
    return pl.pallas_call(
        functools.partial(_in_proj_kernel, plans=plans),
        grid=(ni, nj),
        in_specs=[
            pl.BlockSpec((tm, d), lambda i, j: (i, 0)),
            pl.BlockSpec((1, d), lambda i, j: (0, 0)),
            pl.BlockSpec((tn, d), lambda i, j: (j, 0)),
            pl.BlockSpec((GATE_W, d), lambda i, j: (MAIN_W // GATE_W, 0)),
        ] + c_in,
        out_specs=[pl.BlockSpec((tm, tn), lambda i, j: (i, j)),
                   pl.BlockSpec((tm, GATE_W), lambda i, j: (i, 0))] + c_out,
        out_shape=[jax.ShapeDtypeStruct((t, MAIN_W), BF16),
                   jax.ShapeDtypeStruct((t, GATE_W), F32)] + c_shapes,
        scratch_shapes=[pltpu.VMEM((tm, d), BF16)],
        compiler_params=pltpu.CompilerParams(
            dimension_semantics=("arbitrary", "arbitrary"), vmem_limit_bytes=VMEM_LIMIT),
        name="in_proj",
    )(x2, g, w, w, *c_args)


GLA_NSEQ = 2
GLA_SPLIT = 2


def _gla_kernel(qk_ref, vg_ref, a1_ref, a2_ref, ab_ref, nrm_ref, ltri_ref, lvl_ref,
                o_ref, s_ref, cum_ref, la_ref):
    @pl.when(pl.program_id(1) == 0)
    def _():
        s_ref[...] = jnp.zeros_like(s_ref)
        la_ref[0:8, :] = jnp.zeros((8, la_ref.shape[1]), F32)
        la_ref[8 + CHUNK:16 + CHUNK, :] = jnp.zeros((8, la_ref.shape[1]), F32)

    def chunk(c, carry):
        rows = pl.ds(pl.multiple_of(c * CHUNK, CHUNK), CHUNK)
        _gla_chunk(rows, qk_ref, vg_ref, a1_ref, a2_ref, ab_ref, nrm_ref, ltri_ref, lvl_ref,
                   o_ref, s_ref, cum_ref, la_ref)
        return carry

    lax.fori_loop(0, qk_ref.shape[1] // CHUNK, chunk, 0)


def _gla_chunk(rows, qk_ref, vg_ref, a1_ref, a2_ref, ab_ref, nrm_ref, ltri_ref, lvl_ref,
               o_ref, s_ref, cum_ref, la_ref):
    nseq = qk_ref.shape[0]
    width = nseq * QK_W
    n_heads = nseq * HEADS
    side = lambda parts: jnp.concatenate(parts, axis=1)
    q = side([qk_ref[i, rows, 0:QK_W] for i in range(nseq)]).astype(F32) * (DK ** -0.5)
    k = side([qk_ref[i, rows, QK_W:2 * QK_W] for i in range(nseq)]).astype(F32)
    z = side([_dot(a1_ref[i, rows, :].astype(BF16), a2_ref[...]) + ab_ref[...] for i in range(nseq)])
    log_a = _log2_sigmoid(z) * (1.0 / GATE_TAU)
    la_ref[8:8 + CHUNK, :] = log_a
    ltri = ltri_ref[...]
    hi, mid, lo = _split3(log_a)
    cum = _dot(ltri, hi) + _dot(ltri, mid) + _dot(ltri, lo)
    cum_ref[...] = cum

    lvl = lvl_ref[...]
    groups = CHUNK // 8
    rows8 = lambda g: slice(8 * g, 8 * g + 8)
    heads = [slice(h * DKP, (h + 1) * DKP) for h in range(n_heads)]
    row = lax.broadcasted_iota(jnp.int32, (CHUNK, width), 0)

    def scores(xq, yk):
        return [_dot_nt(xq[:, hs], yk[:, hs]) for hs in heads]

    qb = q.astype(BF16)
    kb = k.astype(BF16)
    attn = [[jnp.where(lvl[rows8(g), :] == -1, p[rows8(g), :], 0.0) for g in range(groups)]
            for p in scores(qb, kb)]

    def merge(level, parts, q_groups):
        for h in range(n_heads):
            for i, g in enumerate(q_groups):
                attn[h][g] = jnp.where(lvl[rows8(g), :] == level, parts[h][rows8(i), :], attn[h][g])

    e = jnp.exp2(jnp.where((row & 1) == 1, log_a, 0.0))
    merge(0, scores((q * e).astype(BF16), kb), range(groups))
    nxt = la_ref[pl.ds(9, CHUNK), :]
    prv = la_ref[pl.ds(7, CHUNK), :]
    r4 = row & 3
    e = jnp.exp2(jnp.where(r4 == 0, nxt, jnp.where(r4 == 1, 0.0, jnp.where(r4 == 2, log_a, log_a + prv))))
    merge(1, scores((q * e).astype(BF16), (k * e).astype(BF16)), range(groups))
    sub8 = lax.broadcasted_iota(jnp.int32, (8, width), 0)
    pieces = []
    for g in range(groups):
        d = cum[rows8(g), :] - cum_ref[8 * g + 3:8 * g + 4, :]
        pieces.append(jnp.where(sub8 < 4, -d, d))
    e = jnp.exp2(jnp.concatenate(pieces, axis=0))
    merge(2, scores((q * e).astype(BF16), (k * e).astype(BF16)), range(groups))
    for level in range(3, N_LEVELS):
        m = 1 << level
        xq, yk, q_groups = [], [], []
        for base in range(0, CHUNK, 2 * m):
            k_rows = slice(base, base + m)
            q_rows = slice(base + m, base + 2 * m)
            edge = cum_ref[base + m - 1:base + m, :]
            yk += [k[k_rows, :] * jnp.exp2(edge - cum[k_rows, :]), k[q_rows, :]]
            xq.append(q[q_rows, :] * jnp.exp2(cum[q_rows, :] - edge))
            q_groups += range((base + m) // 8, (base + 2 * m) // 8)
        merge(level, scores(jnp.concatenate(xq, axis=0).astype(BF16),
                            jnp.concatenate(yk, axis=0).astype(BF16)), q_groups)

    last = cum_ref[CHUNK - 1:CHUNK, :]
    q_dec = (q * jnp.exp2(cum)).astype(BF16)
    k_dec = k * jnp.exp2(last - cum)
    decay_all = jnp.exp2(last)

    lane = lax.broadcasted_iota(jnp.int32, (CHUNK, PAIR_V), 1)
    lo_mask = lane < DV

    for pp in range(n_heads // 2):
        seq_i, p = divmod(pp, HEADS // 2)
        ks = slice(pp * PAIR_K, (pp + 1) * PAIR_K)
        vs = slice(p * PAIR_V, (p + 1) * PAIR_V)
        vb = vg_ref[seq_i, rows, vs]
        zero = jnp.zeros_like(vb)
        v_lo = jnp.where(lo_mask, vb, zero)
        v_hi = jnp.where(lo_mask, zero, vb)
        v_blk = jnp.concatenate([v_lo, v_hi], axis=0)
        state = s_ref[pp]
        lhs = jnp.concatenate([jnp.concatenate(attn[2 * pp], axis=0).astype(BF16),
                               jnp.concatenate(attn[2 * pp + 1], axis=0).astype(BF16),
                               q_dec[:, ks]], axis=1)
        rhs = jnp.concatenate([v_blk, state.astype(BF16)], axis=0)
        o = _dot(lhs, rhs)

        kd_t = k_dec[:, ks].T.astype(BF16)
        upd = jnp.concatenate([_dot(kd_t[0:DKP, :], v_lo), _dot(kd_t[DKP:PAIR_K, :], v_hi)], axis=0)
        dcols = []
        for h in (2 * pp, 2 * pp + 1):
            d_row = jnp.broadcast_to(decay_all[:, h * DKP:(h + 1) * DKP], (DKP, DKP))
            d_col = d_row.T
            dcols.append(jnp.concatenate([d_col] * (PAIR_V // LANE), axis=1))
        s_ref[pp] = state * jnp.concatenate(dcols, axis=0) + upd

        gate = vg_ref[seq_i, rows, V_W + p * PAIR_V:V_W + (p + 1) * PAIR_V].astype(F32)
        out = o * _head_rms_scale(o) * nrm_ref[:, vs] * (gate * jax.nn.sigmoid(gate))
        o_ref[seq_i, rows, vs] = out.astype(o_ref.dtype)


def _gla(proj, gates, a2p, abp, nrm, ltri, lvl, batch, seq):
    nseq = GLA_NSEQ if batch % GLA_NSEQ == 0 else 1
    split = GLA_SPLIT if seq % (GLA_SPLIT * CHUNK) == 0 else 1
    part = seq // split
    grouped = lambda a: a.reshape(batch // nseq, nseq, split, part, a.shape[-1])
    tok = lambda w, cb: pl.BlockSpec((None, nseq, None, part, w), lambda b, c: (b, 0, c, 0, cb))
    const = lambda shape: pl.BlockSpec(shape, lambda b, c: (0,) * len(shape))
    assert OFF_GLA_K == OFF_GLA_Q + QK_W and OFF_GLA_G == OFF_GLA_V + V_W
    out = pl.pallas_call(
        _gla_kernel,
        grid=(batch // nseq, split),
        in_specs=[
            tok(2 * QK_W, OFF_GLA_Q // (2 * QK_W)),
            tok(2 * V_W, OFF_GLA_V // (2 * V_W)),
            tok(LANE, (OFF_GLA_A1 - MAIN_W) // LANE),
            const((LANE, QK_W)),
            const((1, QK_W)),
            const((1, V_W)),
            const((CHUNK, CHUNK)),
            const((CHUNK, CHUNK)),
        ],
        out_specs=tok(V_W, 0),
        out_shape=jax.ShapeDtypeStruct((batch // nseq, nseq, split, part, V_W), BF16),
        scratch_shapes=[
            pltpu.VMEM((nseq * HEADS // 2, PAIR_K, PAIR_V), F32),
            pltpu.VMEM((CHUNK, nseq * QK_W), F32),
            pltpu.VMEM((CHUNK + 16, nseq * QK_W), F32),
        ],
        compiler_params=pltpu.CompilerParams(
            dimension_semantics=("parallel", "arbitrary"), vmem_limit_bytes=VMEM_LIMIT),
        name="gla",
    )(grouped(proj), grouped(proj), grouped(gates), a2p, abp, nrm, ltri, lvl)
    return out.reshape(batch * seq, V_W)


ST_W = PAIR_V + LANE
TAIL = 8


def _mlstm_kernel(qk_ref, vo_ref, if_ref, cw_ref, ifb_ref, nrm_ref, ltri_ref, lvl_ref,
                  o_ref, c_ref, m_ref, xe_ref):
    c_ref[...] = jnp.zeros_like(c_ref)
    m_ref[...] = jnp.zeros_like(m_ref)
    xe_ref[0:TAIL, :] = jnp.zeros((TAIL, 2 * QK_W), F32)

    def chunk(c, carry):
        rows = pl.ds(pl.multiple_of(c * CHUNK, CHUNK), CHUNK)
        _mlstm_chunk(rows, qk_ref, vo_ref, if_ref, cw_ref, ifb_ref, nrm_ref, ltri_ref, lvl_ref,
                     o_ref, c_ref, m_ref, xe_ref)
        return carry

    lax.fori_loop(0, qk_ref.shape[0] // CHUNK, chunk, 0, unroll=2)


def _mlstm_chunk(rows, qk_ref, vo_ref, if_ref, cw_ref, ifb_ref, nrm_ref, ltri_ref, lvl_ref,
                 o_ref, c_ref, m_ref, xe_ref):
    xe_ref[TAIL:TAIL + CHUNK, :] = qk_ref[rows, :].astype(F32)
    y = jnp.zeros((CHUNK, 2 * QK_W), F32)
    for j in range(CONV_WIDTH):
        y = y + cw_ref[j:j + 1, :] * xe_ref[pl.ds(TAIL - (CONV_WIDTH - 1) + j, CHUNK), :]
    xe_ref[0:TAIL, :] = xe_ref[CHUNK:CHUNK + TAIL, :]
    y = y * jax.nn.sigmoid(y)
    qm = y[:, 0:QK_W]
    km = y[:, QK_W:2 * QK_W] * (DK ** -0.5)
    qb = qm.astype(BF16)
    kb = km.astype(BF16)
    km_t = km.T

    slab = if_ref[rows, :] + ifb_ref[...]
    lane_g = lax.broadcasted_iota(jnp.int32, (CHUNK, LANE), 1)
    gates = jnp.where(lane_g < HEADS, slab, _log_sigmoid(slab))
    ltri = ltri_ref[...]
    g_hi, g_mid, g_lo = _split3(gates)
    cum_col = _dot(ltri, g_hi) + _dot(ltri, g_mid) + _dot(ltri, g_lo)
    gates_t = gates.T[0:2 * HEADS, :]
    t_hi, t_mid, t_lo = _split3(gates_t)
    cum_row = _dot_nt(t_hi, ltri) + _dot_nt(t_mid, ltri) + _dot_nt(t_lo, ltri)

    lvl = lvl_ref[...]
    causal = lvl >= -1
    lane = lax.broadcasted_iota(jnp.int32, (CHUNK, PAIR_V), 1)
    lo_mask = lane < DV
    lane_e = lax.broadcasted_iota(jnp.int32, (CHUNK, LANE), 1)
    one_col = [jnp.where(lane_e == i, 1.0, 0.0).astype(BF16) for i in range(2)]
    lane_r = lax.broadcasted_iota(jnp.int32, (PAIR_K, LANE), 1)
    row_r = lax.broadcasted_iota(jnp.int32, (PAIR_K, LANE), 0)
    ones_blk = jnp.where(lane_r == jnp.where(row_r < CHUNK, 0, 1), 1.0, 0.0).astype(BF16)

    for p in range(HEADS // 2):
        ks = slice(p * PAIR_K, (p + 1) * PAIR_K)
        vs = slice(p * PAIR_V, (p + 1) * PAIR_V)
        vb = vo_ref[rows, vs]
        zero = jnp.zeros_like(vb)
        sc_parts, qs_parts, floor_parts, kw_parts, dprev_parts = [], [], [], [], []
        for h in (2 * p, 2 * p + 1):
            hs = slice(h * DKP, (h + 1) * DKP)
            cc = jnp.broadcast_to(cum_col[:, HEADS + h:HEADS + h + 1], (CHUNK, CHUNK))
            cum_r = cum_row[HEADS + h:HEADS + h + 1, :]
            ib_r = gates_t[h:h + 1, :]
            m_prev = m_ref[h:h + 1, :]
            dmat = jnp.where(causal, cc - cum_r + ib_r, -jnp.inf)
            inter = cc + m_prev
            m_t = jnp.maximum(inter, jnp.max(dmat, axis=-1, keepdims=True))
            w = jnp.exp(dmat - m_t)
            sc_inter = jnp.exp(inter - m_t)
            sc_parts.append((_dot_nt(qb[:, hs], kb[:, hs]) * w).astype(BF16))
            qs_parts.append((qm[:, hs] * sc_inter).astype(BF16))
            floor_parts.append(jnp.exp(-m_t))
            total = cum_r[:, CHUNK - 1:CHUNK]
            g_row = total - cum_r + ib_r
            m_new = jnp.maximum(total + m_prev, jnp.max(g_row, axis=-1, keepdims=True))
            wj = jnp.exp(g_row - m_new)
            dprev_parts.append(jnp.exp(total + m_prev - m_new))
            kw_parts.append((km_t[hs, :] * wj).astype(BF16))
            m_ref[h:h + 1, :] = m_new

        state = c_ref[p]
        st_m = state.astype(BF16)
        v_lo = jnp.where(lo_mask, vb, zero)
        v_hi = jnp.where(lo_mask, zero, vb)
        v_blk = jnp.concatenate([v_lo, v_hi], axis=0)
        rhs = jnp.concatenate([jnp.concatenate([v_blk, ones_blk], axis=1), st_m], axis=0)
        lhs = jnp.concatenate(sc_parts + qs_parts, axis=1)
        res = _dot(lhs, rhs)
        num = res[:, 0:PAIR_V]
        den = _pair_cols(res[:, PAIR_V:PAIR_V + 1], res[:, PAIR_V + 1:PAIR_V + 2], CHUNK)
        floor = _pair_cols(floor_parts[0][:, 0:1], floor_parts[1][:, 0:1], CHUNK)
        hid = num / jnp.maximum(jnp.abs(den), floor)

        upd = jnp.concatenate(
            [_dot(kw_parts[0], jnp.concatenate([v_lo, one_col[0]], axis=1)),
             _dot(kw_parts[1], jnp.concatenate([v_hi, one_col[1]], axis=1))], axis=0)
        d_rows = jnp.concatenate(
            [jnp.broadcast_to(jnp.concatenate([d] * (ST_W // LANE), axis=1), (DKP, ST_W))
             for d in dprev_parts], axis=0)
        c_ref[p] = d_rows * state + upd

        og = vo_ref[rows, V_W + p * PAIR_V:V_W + (p + 1) * PAIR_V].astype(F32)
        out = hid * _head_rms_scale(hid) * nrm_ref[:, vs] * jax.nn.sigmoid(og)
        o_ref[rows, vs] = out.astype(o_ref.dtype)


def _mlstm(proj, gates, cw, ifb, nrm, ltri, lvl, batch, seq):
    tok = lambda w, cb: pl.BlockSpec((seq, w), lambda b: (b, cb))
    const = lambda shape: pl.BlockSpec(shape, lambda b: (0,) * len(shape))
    assert OFF_ML_K == OFF_ML_Q + QK_W and OFF_ML_O == OFF_ML_V + V_W
    return pl.pallas_call(
        _mlstm_kernel,
        grid=(batch,),
        in_specs=[
            tok(2 * QK_W, OFF_ML_Q // (2 * QK_W)),
            tok(2 * V_W, OFF_ML_V // (2 * V_W)),
            tok(LANE, (OFF_ML_IF - MAIN_W) // LANE),
            const((CONV_WIDTH, 2 * QK_W)),
            const((1, LANE)),
            const((1, V_W)),
            const((CHUNK, CHUNK)),
            const((CHUNK, CHUNK)),
        ],
        out_specs=pl.BlockSpec((seq, V_W), lambda b: (b, 0)),
        out_shape=jax.ShapeDtypeStruct((batch * seq, V_W), BF16),
        scratch_shapes=[
            pltpu.VMEM((HEADS // 2, PAIR_K, ST_W), F32),
            pltpu.VMEM((2 * HEADS, LANE), F32),
            pltpu.VMEM((TAIL + CHUNK, 2 * QK_W), F32),
        ],
        compiler_params=pltpu.CompilerParams(
            dimension_semantics=("parallel",), vmem_limit_bytes=VMEM_LIMIT),
        name="mlstm",
    )(proj, proj, gates, cw, ifb, nrm, ltri, lvl)


def _sgu_kernel(u_ref, v_ref, lng_ref, lnb_ref, w_ref, b_ref, lvl_ref, o_ref):
    causal = lvl_ref[...] >= -1
    w_causal = [jnp.where(causal, w_ref[g], 0.0).astype(BF16) for g in range(SGU_GROUPS)]
    for n in range(u_ref.shape[0] // CHUNK):
        rows = slice(n * CHUNK, (n + 1) * CHUNK)
        u = jax.nn.gelu(u_ref[rows, :].astype(F32))
        v = jax.nn.gelu(v_ref[rows, :].astype(F32))
        mu = jnp.mean(v, axis=-1, keepdims=True)
        var = jnp.mean(jnp.square(v - mu), axis=-1, keepdims=True)
        vn = ((v - mu) * lax.rsqrt(var + EPS) * lng_ref[...] + lnb_ref[...]).astype(BF16)
        for g in range(SGU_GROUPS):
            gs = slice(g * SGU_CH, (g + 1) * SGU_CH)
            mixed = _dot(w_causal[g], vn[:, gs]) + b_ref[:, gs]
            o_ref[rows, gs] = (u[:, gs] * mixed).astype(o_ref.dtype)


def _sgu(proj, lng, lnb, w, b_full, lvl, batch, seq):
    step = CHUNK * SGU_STEP
    while (batch * seq) % step:
        step //= 2
    nb = batch * seq // step
    tok = lambda w_, cb: pl.BlockSpec((step, w_), lambda i: (i, cb))
    const = lambda shape: pl.BlockSpec(shape, lambda i: (0,) * len(shape))
    return pl.pallas_call(
        _sgu_kernel,
        grid=(nb,),
        in_specs=[
            tok(SGU_W, OFF_SGU_U // SGU_W),
            tok(SGU_W, OFF_SGU_V // SGU_W),
            const((1, SGU_W)),
            const((1, SGU_W)),
            const((SGU_GROUPS, CHUNK, CHUNK)),
            const((CHUNK, SGU_W)),
            const((CHUNK, CHUNK)),
        ],
        out_specs=pl.BlockSpec((step, SGU_W), lambda i: (i, 0)),
        out_shape=jax.ShapeDtypeStruct((batch * seq, SGU_W), BF16),
        compiler_params=pltpu.CompilerParams(
            dimension_semantics=("parallel",), vmem_limit_bytes=VMEM_LIMIT),
        name="sgu",
    )(proj, proj, lng, lnb, w, b_full, lvl)


def _out_proj_kernel(x_ref, a_ref, b_ref, c_ref, wa_ref, wb_ref, wc_ref, *rest, ride_steps):
    if ride_steps:
        wt_ref, o_ref, ow_ref = rest

        @pl.when(pl.program_id(0) * pl.num_programs(1) + pl.program_id(1) < ride_steps)
        def _():
            _relayout_kernel(wt_ref, ow_ref)
    else:
        o_ref, = rest
    o_ref[...] = (x_ref[...] + _dot(a_ref[...], wa_ref[...]) + _dot(b_ref[...], wb_ref[...])
                  + _dot(c_ref[...], wc_ref[...]))


def _out_proj(x2, mix_a, mix_b, mix_c, w_out, tm, tn, relayout=None):
    t, d = x2.shape
    ni = t // tm
    ride_in, ride_out, ride_shape, ride_args, ride_steps = [], [], [], [], 0
    if relayout is not None and relayout[0].shape[2] // LANE > (d // tn) * ni:
        x1, = _out_proj(x2, mix_a, mix_b, mix_c, w_out, tm, tn)
        return x1, _relayout_w_in(*relayout)
    if relayout is not None:
        w_t, layer = relayout
        ride_steps = w_t.shape[2] // LANE
        at = lambda j, i: jnp.minimum(j * ni + i, ride_steps - 1)
        ride_in = [pl.BlockSpec((None, w_t.shape[1], LANE), lambda j, i: (layer, 0, at(j, i)))]
        ride_out = [pl.BlockSpec((PROJ_W, LANE), lambda j, i: (0, at(j, i)))]
        ride_shape = [jax.ShapeDtypeStruct((PROJ_W, w_t.shape[2]), BF16)]
        ride_args = [w_t]
    once = dict(pipeline_mode=pl.Buffered(1)) if tn == d else {}
    return pl.pallas_call(
        functools.partial(_out_proj_kernel, ride_steps=ride_steps),
        grid=(d // tn, t // tm),
        in_specs=[
            pl.BlockSpec((tm, tn), lambda j, i: (i, j)),
            pl.BlockSpec((tm, V_W), lambda j, i: (i, 0)),
            pl.BlockSpec((tm, V_W), lambda j, i: (i, 0)),
            pl.BlockSpec((tm, SGU_W), lambda j, i: (i, 0)),
            pl.BlockSpec((V_W, tn), lambda j, i: (0, j), **once),
            pl.BlockSpec((V_W, tn), lambda j, i: (1, j), **once),
            pl.BlockSpec((SGU_W, tn), lambda j, i: (2 * V_W // SGU_W, j), **once),
        ] + ride_in,
        out_specs=[pl.BlockSpec((tm, tn), lambda j, i: (i, j))] + ride_out,
        out_shape=[jax.ShapeDtypeStruct((t, d), F32)] + ride_shape,
        compiler_params=pltpu.CompilerParams(
            dimension_semantics=("arbitrary", "arbitrary"), vmem_limit_bytes=VMEM_LIMIT),
        name="out_proj",
    )(x2, mix_a, mix_b, mix_c, w_out, w_out, w_out, *ride_args)


def _ffn_kernel(x_ref, g_ref, wgu_ref, wd_ref, gf_ref, *rest, final_norm, plans):
    n = len(plans)
    o_ref, h_ref = rest[n], rest[2 * n + 1]
    j = pl.program_id(1)
    _run_casts(pl.program_id(0) * pl.num_programs(1) + j, plans, rest[0:n], rest[n + 1:2 * n + 1])

    @pl.when(j == 0)
    def _():
        x = x_ref[...]
        ms = jnp.mean(x * x, axis=-1, keepdims=True)
        h_ref[...] = (x * lax.rsqrt(ms + EPS) * g_ref[...]).astype(BF16)
        o_ref[...] = x

    h = h_ref[...]
    gate_up = _dot(h, wgu_ref[...])
    tf = gate_up.shape[1] // 2
    gate, up = gate_up[:, 0:tf], gate_up[:, tf:2 * tf]
    act = (gate * jax.nn.sigmoid(gate) * up).astype(BF16)
    o_ref[...] += _dot(act, wd_ref[...])

    if final_norm:
        @pl.when(j == pl.num_programs(1) - 1)
        def _():
            y = o_ref[...]
            ms = jnp.mean(y * y, axis=-1, keepdims=True)
            o_ref[...] = y * lax.rsqrt(ms + EPS) * gf_ref[...]


def _ffn(x2, g, w_gu, w_down, g_final, tm, tf, final_norm, casts=()):
    t, d = x2.shape
    nf = D_FF // tf
    plans = [_cast_plan(cw, cl, (t // tm) * nf, *ct) for cw, cl, *ct in casts]
    c_in, c_out, c_shapes, c_args = _cast_specs(plans, lambda i, j: i * nf + j)
    return pl.pallas_call(
        functools.partial(_ffn_kernel, final_norm=final_norm, plans=plans),
        grid=(t // tm, nf),
        in_specs=[
            pl.BlockSpec((tm, d), lambda i, j: (i, 0)),
            pl.BlockSpec((1, d), lambda i, j: (0, 0)),
            pl.BlockSpec((d, 2 * tf), lambda i, j: (0, j)),
            pl.BlockSpec((tf, d), lambda i, j: (j, 0)),
            pl.BlockSpec((1, d), lambda i, j: (0, 0)),
        ] + c_in,
        out_specs=[pl.BlockSpec((tm, d), lambda i, j: (i, 0))] + c_out,
        out_shape=[jax.ShapeDtypeStruct((t, d), F32)] + c_shapes,
        scratch_shapes=[pltpu.VMEM((tm, d), BF16)],
        compiler_params=pltpu.CompilerParams(
            dimension_semantics=("arbitrary", "arbitrary"), vmem_limit_bytes=VMEM_LIMIT),
        name="ffn",
    )(x2, g, w_gu, w_down, g_final, *c_args)


class _Tiles(NamedTuple):
    tm: int
    tn_in: int
    tm_out: int
    tf: int


def _tiles(tokens):
    tm = 1024
    while tokens % tm:
        tm //= 2
    return _Tiles(tm=tm, tn_in=1536, tm_out=max(tm // 2, 8), tf=512)


def kernel(x, norm_mix, w_in, gla_a2, gla_ab, gla_norm, ml_conv, ml_ib, ml_fb, ml_norm,
           sgu_ln_g, sgu_ln_b, sgu_w, sgu_b, w_out, norm_ffn, w_gu, w_down, norm_final):
    batch, seq, d = x.shape
    depth = w_in.shape[0]
    tokens = batch * seq
    tiles = _tiles(tokens)

    lvl = jnp.asarray(_pair_level_matrix())
    ltri = jnp.asarray(np.tril(np.ones((CHUNK, CHUNK), np.float32)), BF16)

    w_in_t = jnp.swapaxes(w_in, 1, 2)
    w_in_b = _relayout_w_in(w_in_t, 0)

    xc = x.reshape(tokens, d)
    for l in range(depth):
        if l == 0:
            proj, gates, w_out_b, w_gu_b, w_down_b = _in_proj(
                xc, norm_mix[l][None, :], w_in_b, tiles.tm, tiles.tn_in, casts=[(w_out, 0), (w_gu, 0, tiles.tf), (w_down, 0)])
        else:
            proj, gates = _in_proj(xc, norm_mix[l][None, :], w_in_b, tiles.tm, tiles.tn_in)

        a2p = jnp.pad(_pad_heads(gla_a2[l]), ((0, LANE - GATE_RANK), (0, 0))).astype(BF16)
        abp = _pad_heads(gla_ab[l])[None, :]
        mix_a = _gla(proj, gates, a2p, abp, gla_norm[l][None, :], ltri, lvl, batch, seq)

        cw = jnp.concatenate([_pad_heads(ml_conv[l][:, :HEADS * DK]),
                              _pad_heads(ml_conv[l][:, HEADS * DK:])], axis=1)
        ifb = jnp.pad(jnp.concatenate([ml_ib[l], ml_fb[l]]), (0, LANE - 2 * HEADS))[None, :]
        mix_b = _mlstm(proj, gates, cw, ifb, ml_norm[l][None, :], ltri, lvl, batch, seq)

        b_full = jnp.repeat(sgu_b[l].T, SGU_CH, axis=1)
        mix_c = _sgu(proj, sgu_ln_g[l][None, :], sgu_ln_b[l][None, :], sgu_w[l], b_full, lvl, batch, seq)

        if l + 1 < depth:
            x1, w_in_b = _out_proj(xc, mix_a, mix_b, mix_c, w_out_b, tiles.tm_out, d, relayout=(w_in_t, l + 1))
        else:
            x1, = _out_proj(xc, mix_a, mix_b, mix_c, w_out_b, tiles.tm, d)
        nxt = [(w_out, l + 1), (w_gu, l + 1, tiles.tf), (w_down, l + 1)] if l + 1 < depth else []
        xc, *cast = _ffn(x1, norm_ffn[l][None, :], w_gu_b, w_down_b, norm_final[None, :], tiles.tm,
                         tiles.tf, final_norm=(l == depth - 1), casts=nxt)
        if cast:
            w_out_b, w_gu_b, w_down_b = cast
    return xc.reshape(batch, seq, d)
```

```python
import functools
from typing import NamedTuple

import numpy as np
import jax
import jax.numpy as jnp
from jax import lax
from jax.experimental import pallas as pl
from jax.experimental.pallas import tpu as pltpu

F32 = jnp.float32
BF16 = jnp.bfloat16

HEADS = 4
DK = 96
DKP = 128
DV = 192
QK_W = HEADS * DKP
V_W = HEADS * DV
PAIR_K = 2 * DKP
PAIR_V = 2 * DV
SGU_W = 512
SGU_GROUPS = 4
SGU_CH = 128
GATE_RANK = 16
GATE_TAU = 16.0
CONV_WIDTH = 4
D_FF = 5632
EPS = 1e-6
CHUNK = 128
N_LEVELS = 7
LANE = 128

OFF_GLA_Q, OFF_GLA_K, OFF_ML_Q, OFF_ML_K = 0, 512, 1024, 1536
OFF_SGU_U, OFF_SGU_V = 2048, 2560
OFF_GLA_V, OFF_GLA_G, OFF_ML_V, OFF_ML_O = 3072, 3840, 4608, 5376
MAIN_W = 6144
OFF_GLA_A1, OFF_ML_IF = 6144, 6272
GATE_W = 2 * LANE
PROJ_W = MAIN_W + GATE_W
SGU_STEP = 4

VMEM_LIMIT = 58 * 1024 * 1024


def _column_moves():
    gk = HEADS * DK
    sizes = (gk, gk, V_W, V_W, GATE_RANK, gk, gk, V_W, V_W, HEADS, HEADS, SGU_W, SGU_W)
    starts = np.concatenate([[0], np.cumsum(sizes)]).tolist()
    (gq, gkk, gv, gg, ga1, mq, mk, mv, mo, mi, mf, su, sv) = starts[:-1]
    moves = []
    for dst, src in ((OFF_GLA_Q, gq), (OFF_GLA_K, gkk), (OFF_ML_Q, mq), (OFF_ML_K, mk)):
        moves += [(dst + h * DKP, src + h * DK, DK) for h in range(HEADS)]
    moves += [(OFF_SGU_U, su, SGU_W), (OFF_SGU_V, sv, SGU_W), (OFF_GLA_V, gv, V_W), (OFF_GLA_G, gg, V_W),
              (OFF_ML_V, mv, V_W), (OFF_ML_O, mo, V_W), (OFF_GLA_A1, ga1, GATE_RANK),
              (OFF_ML_IF, mi, 2 * HEADS)]
    assert mf == mi + HEADS and starts[-1] == sum(sizes)
    return moves


def _relayout_kernel(w_ref, o_ref):
    o_ref[...] = jnp.zeros_like(o_ref)
    for dst, src, n in _column_moves():
        if n % 16:
            pad = jnp.zeros((16 - n % 16, w_ref.shape[1]), F32)
            o_ref[dst:dst + n + pad.shape[0], :] = jnp.concatenate([w_ref[src:src + n, :], pad]).astype(BF16)
        else:
            o_ref[dst:dst + n, :] = w_ref[src:src + n, :].astype(BF16)


def _relayout_w_in(w_t, layer, cols=256):
    _, p, d = w_t.shape
    return pl.pallas_call(
        _relayout_kernel,
        grid=(d // cols,),
        in_specs=[pl.BlockSpec((None, p, cols), lambda i: (layer, 0, i))],
        out_specs=pl.BlockSpec((PROJ_W, cols), lambda i: (0, i)),
        out_shape=jax.ShapeDtypeStruct((PROJ_W, d), BF16),
        compiler_params=pltpu.CompilerParams(
            dimension_semantics=("parallel",), vmem_limit_bytes=VMEM_LIMIT),
        name="relayout",
    )(w_t)


def _pad_heads(a):
    lead = a.shape[:-1]
    a = a.reshape(lead + (HEADS, DK))
    a = jnp.pad(a, [(0, 0)] * len(lead) + [(0, 0), (0, DKP - DK)])
    return a.reshape(lead + (QK_W,))


def _pair_level_matrix():
    c = CHUNK
    t = np.arange(c)[:, None]
    s = np.arange(c)[None, :]
    x = np.bitwise_xor(t, s)
    lvl = np.floor(np.log2(np.maximum(x, 1))).astype(np.int32)
    lvl = np.where(s == t, -1, lvl)
    lvl = np.where(s > t, -2, lvl)
    return lvl.astype(np.int32)


def _dot(a, b):
    return jnp.dot(a, b, preferred_element_type=F32)


def _dot_nt(a, b):
    return lax.dot_general(a, b, (((1,), (1,)), ((), ())), preferred_element_type=F32)


def _split3(x):
    hi = x.astype(BF16)
    r = x - hi.astype(F32)
    mid = r.astype(BF16)
    lo = (r - mid.astype(F32)).astype(BF16)
    return hi, mid, lo


LOG2E = 1.4426950408889634
LN2 = 0.6931471805599453


def _log2_sigmoid(x):
    xl = x * LOG2E
    return jnp.minimum(xl, 0.0) - jnp.log2(1.0 + jnp.exp2(-jnp.abs(xl)))


def _log_sigmoid(x):
    return _log2_sigmoid(x) * LN2


def _pair_cols(c0, c1, rows):
    first = lax.broadcasted_iota(jnp.int32, (rows, LANE), 1) < DV - LANE
    b0 = jnp.broadcast_to(c0, (rows, LANE))
    b1 = jnp.broadcast_to(c1, (rows, LANE))
    return jnp.concatenate([b0, jnp.where(first, b0, b1), b1], axis=1)


def _head_rms_scale(o):
    rows = o.shape[0]
    o2 = o * o
    first = lax.broadcasted_iota(jnp.int32, (rows, LANE), 1) < DV - LANE
    mid = o2[:, LANE:2 * LANE]
    ss0 = jnp.sum(o2[:, 0:LANE] + jnp.where(first, mid, 0.0), axis=-1, keepdims=True)
    ss1 = jnp.sum(o2[:, 2 * LANE:3 * LANE] + jnp.where(first, 0.0, mid), axis=-1, keepdims=True)
    return _pair_cols(lax.rsqrt(ss0 / DV + EPS), lax.rsqrt(ss1 / DV + EPS), rows)


def _cast_plan(w, layer, steps, tile=0):
    rows, cols = w.shape[1], w.shape[2]
    nblk = max(n for n in range(1, steps + 1) if rows % n == 0 and (rows // n) % 16 == 0)
    return w, layer, rows, cols, nblk, tile


def _cast_specs(plans, step_of):
    in_specs, out_specs, out_shapes, args = [], [], [], []
    for w, layer, rows, cols, nblk, _ in plans:
        blk = lambda *g, nblk=nblk: jnp.minimum(step_of(*g), nblk - 1)
        in_specs.append(pl.BlockSpec((None, rows // nblk, cols), lambda *g, blk=blk, layer=layer: (layer, blk(*g), 0)))
        out_specs.append(pl.BlockSpec((rows // nblk, cols), lambda *g, blk=blk: (blk(*g), 0)))
        out_shapes.append(jax.ShapeDtypeStruct((rows, cols), BF16))
        args.append(w)
    return in_specs, out_specs, out_shapes, args


def _run_casts(step, plans, src_refs, dst_refs):
    for (_, _, _, cols, nblk, tile), src, dst in zip(plans, src_refs, dst_refs):
        @pl.when(step < nblk)
        def _(src=src, dst=dst, cols=cols, tile=tile):
            if not tile:
                dst[...] = src[...].astype(BF16)
            else:
                half = cols // 2
                for j in range(half // tile):
                    dst[:, 2 * j * tile:(2 * j + 1) * tile] = src[:, j * tile:(j + 1) * tile].astype(BF16)
                    dst[:, (2 * j + 1) * tile:(2 * j + 2) * tile] = (
                        src[:, half + j * tile:half + (j + 1) * tile].astype(BF16))


def _in_proj_kernel(x_ref, g_ref, w_ref, wg_ref, *rest, plans):
    n = len(plans)
    o_ref, og_ref, h_ref = rest[n], rest[n + 1], rest[2 * n + 2]
    _run_casts(pl.program_id(0) * pl.num_programs(1) + pl.program_id(1), plans,
               rest[0:n], rest[n + 2:2 * n + 2])

    @pl.when(pl.program_id(1) == 0)
    def _():
        x = x_ref[...]
        ms = jnp.mean(x * x, axis=-1, keepdims=True)
        h_ref[...] = (x * lax.rsqrt(ms + EPS) * g_ref[...]).astype(BF16)
        og_ref[...] = _dot_nt(h_ref[...], wg_ref[...])

    o_ref[...] = _dot_nt(h_ref[...], w_ref[...]).astype(o_ref.dtype)


def _in_proj(x2, g, w, tm, tn, casts=()):
    t, d = x2.shape
    ni, nj = t // tm, MAIN_W // tn
    plans = [_cast_plan(cw, cl, ni * nj, *ct) for cw, cl, *ct in casts]
    c_in, c_out, c_shapes, c_args = _cast_specs(plans, lambda i, j: i * nj + j)
    return pl.pallas_call(
        functools.partial(_in_proj_kernel, plans=plans),
        grid=(ni, nj),
        in_specs=[
            pl.BlockSpec((tm, d), lambda i, j: (i, 0)),
            pl.BlockSpec((1, d), lambda i, j: (0, 0)),
            pl.BlockSpec((tn, d), lambda i, j: (j, 0)),
            pl.BlockSpec((GATE_W, d), lambda i, j: (MAIN_W // GATE_W, 0)),
        ] + c_in,
        out_specs=[pl.BlockSpec((tm, tn), lambda i, j: (i, j)),
                   pl.BlockSpec((tm, GATE_W), lambda i, j: (i, 0))] + c_out,
        out_shape=[jax.ShapeDtypeStruct((t, MAIN_W), BF16),
                   jax.ShapeDtypeStruct((t, GATE_W), F32)] + c_shapes,
        scratch_shapes=[pltpu.VMEM((tm, d), BF16)],
        compiler_params=pltpu.CompilerParams(
            dimension_semantics=("arbitrary", "arbitrary"), vmem_limit_bytes=VMEM_LIMIT),
        name="in_proj",
    )(x2, g, w, w, *c_args)


GLA_NSEQ = 2
GLA_SPLIT = 2


def _gla_kernel(qk_ref, vg_ref, a1_ref, a2_ref, ab_ref, nrm_ref, ltri_ref, lvl_ref,
                o_ref, s_ref, cum_ref, la_ref):
    @pl.when(pl.program_id(1) == 0)
    def _():
        s_ref[...] = jnp.zeros_like(s_ref)
        la_ref[0:8, :] = jnp.zeros((8, la_ref.shape[1]), F32)
        la_ref[8 + CHUNK:16 + CHUNK, :] = jnp.zeros((8, la_ref.shape[1]), F32)

    def chunk(c, carry):
        rows = pl.ds(pl.multiple_of(c * CHUNK, CHUNK), CHUNK)
        _gla_chunk(rows, qk_ref, vg_ref, a1_ref, a2_ref, ab_ref, nrm_ref, ltri_ref, lvl_ref,
                   o_ref, s_ref, cum_ref, la_ref)
        return carry

    lax.fori_loop(0, qk_ref.shape[1] // CHUNK, chunk, 0)


def _gla_chunk(rows, qk_ref, vg_ref, a1_ref, a2_ref, ab_ref, nrm_ref, ltri_ref, lvl_ref,
               o_ref, s_ref, cum_ref, la_ref):
    nseq = qk_ref.shape[0]
    width = nseq * QK_W
    n_heads = nseq * HEADS
    side = lambda parts: jnp.concatenate(parts, axis=1)
    q = side([qk_ref[i, rows, 0:QK_W] for i in range(nseq)]).astype(F32) * (DK ** -0.5)
    k = side([qk_ref[i, rows, QK_W:2 * QK_W] for i in range(nseq)]).astype(F32)
    z = side([_dot(a1_ref[i, rows, :].astype(BF16), a2_ref[...]) + ab_ref[...] for i in range(nseq)])
    log_a = _log2_sigmoid(z) * (1.0 / GATE_TAU)
    la_ref[8:8 + CHUNK, :] = log_a
    ltri = ltri_ref[...]
    hi, mid, lo = _split3(log_a)
    cum = _dot(ltri, hi) + _dot(ltri, mid) + _dot(ltri, lo)
    cum_ref[...] = cum

    lvl = lvl_ref[...]
    groups = CHUNK // 8
    rows8 = lambda g: slice(8 * g, 8 * g + 8)
    heads = [slice(h * DKP, (h + 1) * DKP) for h in range(n_heads)]
    row = lax.broadcasted_iota(jnp.int32, (CHUNK, width), 0)

    def scores(xq, yk):
        return [_dot_nt(xq[:, hs], yk[:, hs]) for hs in heads]

    qb = q.astype(BF16)
    kb = k.astype(BF16)
    attn = [[jnp.where(lvl[rows8(g), :] == -1, p[rows8(g), :], 0.0) for g in range(groups)]
            for p in scores(qb, kb)]

    def merge(level, parts, q_groups):
        for h in range(n_heads):
            for i, g in enumerate(q_groups):
                attn[h][g] = jnp.where(lvl[rows8(g), :] == level, parts[h][rows8(i), :], attn[h][g])

    e = jnp.exp2(jnp.where((row & 1) == 1, log_a, 0.0))
    merge(0, scores((q * e).astype(BF16), kb), range(groups))
    nxt = la_ref[pl.ds(9, CHUNK), :]
    prv = la_ref[pl.ds(7, CHUNK), :]
    r4 = row & 3
    e = jnp.exp2(jnp.where(r4 == 0, nxt, jnp.where(r4 == 1, 0.0, jnp.where(r4 == 2, log_a, log_a + prv))))
    merge(1, scores((q * e).astype(BF16), (k * e).astype(BF16)), range(groups))
    sub8 = lax.broadcasted_iota(jnp.int32, (8, width), 0)
    pieces = []
    for g in range(groups):
        d = cum[rows8(g), :] - cum_ref[8 * g + 3:8 * g + 4, :]
        pieces.append(jnp.where(sub8 < 4, -d, d))
    e = jnp.exp2(jnp.concatenate(pieces, axis=0))
    merge(2, scores((q * e).astype(BF16), (k * e).astype(BF16)), range(groups))
    for level in range(3, N_LEVELS):
        m = 1 << level
        xq, yk, q_groups = [], [], []
        for base in range(0, CHUNK, 2 * m):
            k_rows = slice(base, base + m)
            q_rows = slice(base + m, base + 2 * m)
            edge = cum_ref[base + m - 1:base + m, :]
            yk += [k[k_rows, :] * jnp.exp2(edge - cum[k_rows, :]), k[q_rows, :]]
            xq.append(q[q_rows, :] * jnp.exp2(cum[q_rows, :] - edge))
            q_groups += range((base + m) // 8, (base + 2 * m) // 8)
        merge(level, scores(jnp.concatenate(xq, axis=0).astype(BF16),
                            jnp.concatenate(yk, axis=0).astype(BF16)), q_groups)

    last = cum_ref[CHUNK - 1:CHUNK, :]
    q_dec = (q * jnp.exp2(cum)).astype(BF16)
    k_dec = k * jnp.exp2(last - cum)
    decay_all = jnp.exp2(last)

    lane = lax.broadcasted_iota(jnp.int32, (CHUNK, PAIR_V), 1)
    lo_mask = lane < DV

    for pp in range(n_heads // 2):
        seq_i, p = divmod(pp, HEADS // 2)
        ks = slice(pp * PAIR_K, (pp + 1) * PAIR_K)
        vs = slice(p * PAIR_V, (p + 1) * PAIR_V)
        vb = vg_ref[seq_i, rows, vs]
        zero = jnp.zeros_like(vb)
        v_lo = jnp.where(lo_mask, vb, zero)
        v_hi = jnp.where(lo_mask, zero, vb)
        v_blk = jnp.concatenate([v_lo, v_hi], axis=0)
        state = s_ref[pp]
        lhs = jnp.concatenate([jnp.concatenate(attn[2 * pp], axis=0).astype(BF16),
                               jnp.concatenate(attn[2 * pp + 1], axis=0).astype(BF16),
                               q_dec[:, ks]], axis=1)
        rhs = jnp.concatenate([v_blk, state.astype(BF16)], axis=0)
        o = _dot(lhs, rhs)

        kd_t = k_dec[:, ks].T.astype(BF16)
        upd = jnp.concatenate([_dot(kd_t[0:DKP, :], v_lo), _dot(kd_t[DKP:PAIR_K, :], v_hi)], axis=0)
        dcols = []
        for h in (2 * pp, 2 * pp + 1):
            d_row = jnp.broadcast_to(decay_all[:, h * DKP:(h + 1) * DKP], (DKP, DKP))
            d_col = d_row.T
            dcols.append(jnp.concatenate([d_col] * (PAIR_V // LANE), axis=1))
        s_ref[pp] = state * jnp.concatenate(dcols, axis=0) + upd

        gate = vg_ref[seq_i, rows, V_W + p * PAIR_V:V_W + (p + 1) * PAIR_V].astype(F32)
        out = o * _head_rms_scale(o) * nrm_ref[:, vs] * (gate * jax.nn.sigmoid(gate))
        o_ref[seq_i, rows, vs] = out.astype(o_ref.dtype)


def _gla(proj, gates, a2p, abp, nrm, ltri, lvl, batch, seq):
    nseq = GLA_NSEQ if batch % GLA_NSEQ == 0 else 1
    split = GLA_SPLIT if seq % (GLA_SPLIT * CHUNK) == 0 else 1
    part = seq // split
    grouped = lambda a: a.reshape(batch // nseq, nseq, split, part, a.shape[-1])
    tok = lambda w, cb: pl.BlockSpec((None, nseq, None, part, w), lambda b, c: (b, 0, c, 0, cb))
    const = lambda shape: pl.BlockSpec(shape, lambda b, c: (0,) * len(shape))
    assert OFF_GLA_K == OFF_GLA_Q + QK_W and OFF_GLA_G == OFF_GLA_V + V_W
    out = pl.pallas_call(
        _gla_kernel,
        grid=(batch // nseq, split),
        in_specs=[
            tok(2 * QK_W, OFF_GLA_Q // (2 * QK_W)),
            tok(2 * V_W, OFF_GLA_V // (2 * V_W)),
            tok(LANE, (OFF_GLA_A1 - MAIN_W) // LANE),
            const((LANE, QK_W)),
            const((1, QK_W)),
            const((1, V_W)),
            const((CHUNK, CHUNK)),
            const((CHUNK, CHUNK)),
        ],
        out_specs=tok(V_W, 0),
        out_shape=jax.ShapeDtypeStruct((batch // nseq, nseq, split, part, V_W), BF16),
        scratch_shapes=[
            pltpu.VMEM((nseq * HEADS // 2, PAIR_K, PAIR_V), F32),
            pltpu.VMEM((CHUNK, nseq * QK_W), F32),
            pltpu.VMEM((CHUNK + 16, nseq * QK_W), F32),
        ],
        compiler_params=pltpu.CompilerParams(
            dimension_semantics=("parallel", "arbitrary"), vmem_limit_bytes=VMEM_LIMIT),
        name="gla",
    )(grouped(proj), grouped(proj), grouped(gates), a2p, abp, nrm, ltri, lvl)
    return out.reshape(batch * seq, V_W)


ST_W = PAIR_V + LANE
TAIL = 8


def _mlstm_kernel(qk_ref, vo_ref, if_ref, cw_ref, ifb_ref, nrm_ref, ltri_ref, lvl_ref,
                  o_ref, c_ref, m_ref, xe_ref):
    c_ref[...] = jnp.zeros_like(c_ref)
    m_ref[...] = jnp.zeros_like(m_ref)
    xe_ref[0:TAIL, :] = jnp.zeros((TAIL, 2 * QK_W), F32)

    def chunk(c, carry):
        rows = pl.ds(pl.multiple_of(c * CHUNK, CHUNK), CHUNK)
        _mlstm_chunk(rows, qk_ref, vo_ref, if_ref, cw_ref, ifb_ref, nrm_ref, ltri_ref, lvl_ref,
                     o_ref, c_ref, m_ref, xe_ref)
        return carry

    lax.fori_loop(0, qk_ref.shape[0] // CHUNK, chunk, 0, unroll=2)


def _mlstm_chunk(rows, qk_ref, vo_ref, if_ref, cw_ref, ifb_ref, nrm_ref, ltri_ref, lvl_ref,
                 o_ref, c_ref, m_ref, xe_ref):
    xe_ref[TAIL:TAIL + CHUNK, :] = qk_ref[rows, :].astype(F32)
    y = jnp.zeros((CHUNK, 2 * QK_W), F32)
    for j in range(CONV_WIDTH):
        y = y + cw_ref[j:j + 1, :] * xe_ref[pl.ds(TAIL - (CONV_WIDTH - 1) + j, CHUNK), :]
    xe_ref[0:TAIL, :] = xe_ref[CHUNK:CHUNK + TAIL, :]
    y = y * jax.nn.sigmoid(y)
    qm = y[:, 0:QK_W]
    km = y[:, QK_W:2 * QK_W] * (DK ** -0.5)
    qb = qm.astype(BF16)
    kb = km.astype(BF16)
    km_t = km.T

    slab = if_ref[rows, :] + ifb_ref[...]
    lane_g = lax.broadcasted_iota(jnp.int32, (CHUNK, LANE), 1)
    gates = jnp.where(lane_g < HEADS, slab, _log_sigmoid(slab))
    ltri = ltri_ref[...]
    g_hi, g_mid, g_lo = _split3(gates)
    cum_col = _dot(ltri, g_hi) + _dot(ltri, g_mid) + _dot(ltri, g_lo)
    gates_t = gates.T[0:2 * HEADS, :]
    t_hi, t_mid, t_lo = _split3(gates_t)
    cum_row = _dot_nt(t_hi, ltri) + _dot_nt(t_mid, ltri) + _dot_nt(t_lo, ltri)

    lvl = lvl_ref[...]
    causal = lvl >= -1
    lane = lax.broadcasted_iota(jnp.int32, (CHUNK, PAIR_V), 1)
    lo_mask = lane < DV
    lane_e = lax.broadcasted_iota(jnp.int32, (CHUNK, LANE), 1)
    one_col = [jnp.where(lane_e == i, 1.0, 0.0).astype(BF16) for i in range(2)]
    lane_r = lax.broadcasted_iota(jnp.int32, (PAIR_K, LANE), 1)
    row_r = lax.broadcasted_iota(jnp.int32, (PAIR_K, LANE), 0)
    ones_blk = jnp.where(lane_r == jnp.where(row_r < CHUNK, 0, 1), 1.0, 0.0).astype(BF16)

    for p in range(HEADS // 2):
        ks = slice(p * PAIR_K, (p + 1) * PAIR_K)
        vs = slice(p * PAIR_V, (p + 1) * PAIR_V)
        vb = vo_ref[rows, vs]
        zero = jnp.zeros_like(vb)
        sc_parts, qs_parts, floor_parts, kw_parts, dprev_parts = [], [], [], [], []
        for h in (2 * p, 2 * p + 1):
            hs = slice(h * DKP, (h + 1) * DKP)
            cc = jnp.broadcast_to(cum_col[:, HEADS + h:HEADS + h + 1], (CHUNK, CHUNK))
            cum_r = cum_row[HEADS + h:HEADS + h + 1, :]
            ib_r = gates_t[h:h + 1, :]
            m_prev = m_ref[h:h + 1, :]
            dmat = jnp.where(causal, cc - cum_r + ib_r, -jnp.inf)
            inter = cc + m_prev
            m_t = jnp.maximum(inter, jnp.max(dmat, axis=-1, keepdims=True))
            w = jnp.exp(dmat - m_t)
            sc_inter = jnp.exp(inter - m_t)
            sc_parts.append((_dot_nt(qb[:, hs], kb[:, hs]) * w).astype(BF16))
            qs_parts.append((qm[:, hs] * sc_inter).astype(BF16))
            floor_parts.append(jnp.exp(-m_t))
            total = cum_r[:, CHUNK - 1:CHUNK]
            g_row = total - cum_r + ib_r
            m_new = jnp.maximum(total + m_prev, jnp.max(g_row, axis=-1, keepdims=True))
            wj = jnp.exp(g_row - m_new)
            dprev_parts.append(jnp.exp(total + m_prev - m_new))
            kw_parts.append((km_t[hs, :] * wj).astype(BF16))
            m_ref[h:h + 1, :] = m_new

        state = c_ref[p]
        st_m = state.astype(BF16)
        v_lo = jnp.where(lo_mask, vb, zero)
        v_hi = jnp.where(lo_mask, zero, vb)
        v_blk = jnp.concatenate([v_lo, v_hi], axis=0)
        rhs = jnp.concatenate([jnp.concatenate([v_blk, ones_blk], axis=1), st_m], axis=0)
        lhs = jnp.concatenate(sc_parts + qs_parts, axis=1)
        res = _dot(lhs, rhs)
        num = res[:, 0:PAIR_V]
        den = _pair_cols(res[:, PAIR_V:PAIR_V + 1], res[:, PAIR_V + 1:PAIR_V + 2], CHUNK)
        floor = _pair_cols(floor_parts[0][:, 0:1], floor_parts[1][:, 0:1], CHUNK)
        hid = num / jnp.maximum(jnp.abs(den), floor)

        upd = jnp.concatenate(
            [_dot(kw_parts[0], jnp.concatenate([v_lo, one_col[0]], axis=1)),
             _dot(kw_parts[1], jnp.concatenate([v_hi, one_col[1]], axis=1))], axis=0)
        d_rows = jnp.concatenate(
            [jnp.broadcast_to(jnp.concatenate([d] * (ST_W // LANE), axis=1), (DKP, ST_W))
             for d in dprev_parts], axis=0)
        c_ref[p] = d_rows * state + upd

        og = vo_ref[rows, V_W + p * PAIR_V:V_W + (p + 1) * PAIR_V].astype(F32)
        out = hid * _head_rms_scale(hid) * nrm_ref[:, vs] * jax.nn.sigmoid(og)
        o_ref[rows, vs] = out.astype(o_ref.dtype)


def _mlstm(proj, gates, cw, ifb, nrm, ltri, lvl, batch, seq):
    tok = lambda w, cb: pl.BlockSpec((seq, w), lambda b: (b, cb))
    const = lambda shape: pl.BlockSpec(shape, lambda b: (0,) * len(shape))
    assert OFF_ML_K == OFF_ML_Q + QK_W and OFF_ML_O == OFF_ML_V + V_W
    return pl.pallas_call(
        _mlstm_kernel,
        grid=(batch,),
        in_specs=[
            tok(2 * QK_W, OFF_ML_Q // (2 * QK_W)),
            tok(2 * V_W, OFF_ML_V // (2 * V_W)),
            tok(LANE, (OFF_ML_IF - MAIN_W) // LANE),
            const((CONV_WIDTH, 2 * QK_W)),
            const((1, LANE)),
            const((1, V_W)),
            const((CHUNK, CHUNK)),
            const((CHUNK, CHUNK)),
        ],
        out_specs=pl.BlockSpec((seq, V_W), lambda b: (b, 0)),
        out_shape=jax.ShapeDtypeStruct((batch * seq, V_W), BF16),
        scratch_shapes=[
            pltpu.VMEM((HEADS // 2, PAIR_K, ST_W), F32),
            pltpu.VMEM((2 * HEADS, LANE), F32),
            pltpu.VMEM((TAIL + CHUNK, 2 * QK_W), F32),
        ],
        compiler_params=pltpu.CompilerParams(
            dimension_semantics=("parallel",), vmem_limit_bytes=VMEM_LIMIT),
        name="mlstm",
    )(proj, proj, gates, cw, ifb, nrm, ltri, lvl)


def _sgu_kernel(u_ref, v_ref, lng_ref, lnb_ref, w_ref, b_ref, lvl_ref, o_ref):
    causal = lvl_ref[...] >= -1
    w_causal = [jnp.where(causal, w_ref[g], 0.0).astype(BF16) for g in range(SGU_GROUPS)]
    for n in range(u_ref.shape[0] // CHUNK):
        rows = slice(n * CHUNK, (n + 1) * CHUNK)
        u = jax.nn.gelu(u_ref[rows, :].astype(F32))
        v = jax.nn.gelu(v_ref[rows, :].astype(F32))
        mu = jnp.mean(v, axis=-1, keepdims=True)
        var = jnp.mean(jnp.square(v - mu), axis=-1, keepdims=True)
        vn = ((v - mu) * lax.rsqrt(var + EPS) * lng_ref[...] + lnb_ref[...]).astype(BF16)
        for g in range(SGU_GROUPS):
            gs = slice(g * SGU_CH, (g + 1) * SGU_CH)
            mixed = _dot(w_causal[g], vn[:, gs]) + b_ref[:, gs]
            o_ref[rows, gs] = (u[:, gs] * mixed).astype(o_ref.dtype)


def _sgu(proj, lng, lnb, w, b_full, lvl, batch, seq):
    step = CHUNK * SGU_STEP
    while (batch * seq) % step:
        step //= 2
    nb = batch * seq // step
    tok = lambda w_, cb: pl.BlockSpec((step, w_), lambda i: (i, cb))
    const = lambda shape: pl.BlockSpec(shape, lambda i: (0,) * len(shape))
    return pl.pallas_call(
        _sgu_kernel,
        grid=(nb,),
        in_specs=[
            tok(SGU_W, OFF_SGU_U // SGU_W),
            tok(SGU_W, OFF_SGU_V // SGU_W),
            const((1, SGU_W)),
            const((1, SGU_W)),
            const((SGU_GROUPS, CHUNK, CHUNK)),
            const((CHUNK, SGU_W)),
            const((CHUNK, CHUNK)),
        ],
        out_specs=pl.BlockSpec((step, SGU_W), lambda i: (i, 0)),
        out_shape=jax.ShapeDtypeStruct((batch * seq, SGU_W), BF16),
        compiler_params=pltpu.CompilerParams(
            dimension_semantics=("parallel",), vmem_limit_bytes=VMEM_LIMIT),
        name="sgu",
    )(proj, proj, lng, lnb, w, b_full, lvl)


def _out_proj_kernel(x_ref, a_ref, b_ref, c_ref, w_ref, *rest, ride_steps):
    if ride_steps:
        wt_ref, o_ref, ow_ref = rest

        @pl.when(pl.program_id(0) * pl.num_programs(1) + pl.program_id(1) < ride_steps)
        def _():
            _relayout_kernel(wt_ref, ow_ref)
    else:
        o_ref, = rest
    mix = jnp.concatenate([a_ref[...], b_ref[...], c_ref[...]], axis=1)
    o_ref[...] = x_ref[...] + _dot(mix, w_ref[...])


def _out_proj(x2, mix_a, mix_b, mix_c, w_out, tm, tn, relayout=None):
    t, d = x2.shape
    ni = t // tm
    ride_in, ride_out, ride_shape, ride_args, ride_steps = [], [], [], [], 0
    if relayout is not None and relayout[0].shape[2] // LANE > (d // tn) * ni:
        x1, = _out_proj(x2, mix_a, mix_b, mix_c, w_out, tm, tn)
        return x1, _relayout_w_in(*relayout)
    if relayout is not None:
        w_t, layer = relayout
        ride_steps = w_t.shape[2] // LANE
        at = lambda j, i: jnp.minimum(j * ni + i, ride_steps - 1)
        ride_in = [pl.BlockSpec((None, w_t.shape[1], LANE), lambda j, i: (layer, 0, at(j, i)))]
        ride_out = [pl.BlockSpec((PROJ_W, LANE), lambda j, i: (0, at(j, i)))]
        ride_shape = [jax.ShapeDtypeStruct((PROJ_W, w_t.shape[2]), BF16)]
        ride_args = [w_t]
    once = dict(pipeline_mode=pl.Buffered(1)) if tn == d else {}
    return pl.pallas_call(
        functools.partial(_out_proj_kernel, ride_steps=ride_steps),
        grid=(d // tn, t // tm),
        in_specs=[
            pl.BlockSpec((tm, tn), lambda j, i: (i, j)),
            pl.BlockSpec((tm, V_W), lambda j, i: (i, 0)),
            pl.BlockSpec((tm, V_W), lambda j, i: (i, 0)),
            pl.BlockSpec((tm, SGU_W), lambda j, i: (i, 0)),
            pl.BlockSpec((d, tn), lambda j, i: (0, j), **once),
        ] + ride_in,
        out_specs=[pl.BlockSpec((tm, tn), lambda j, i: (i, j))] + ride_out,
        out_shape=[jax.ShapeDtypeStruct((t, d), F32)] + ride_shape,
        compiler_params=pltpu.CompilerParams(
            dimension_semantics=("arbitrary", "arbitrary"), vmem_limit_bytes=VMEM_LIMIT),
        name="out_proj",
    )(x2, mix_a, mix_b, mix_c, w_out, *ride_args)


def _ffn_kernel(x_ref, g_ref, wgu_ref, wd_ref, gf_ref, *rest, final_norm, plans):
    n = len(plans)
    o_ref, h_ref = rest[n], rest[2 * n + 1]
    j = pl.program_id(1)
    _run_casts(pl.program_id(0) * pl.num_programs(1) + j, plans, rest[0:n], rest[n + 1:2 * n + 1])

    @pl.when(j == 0)
    def _():
        x = x_ref[...]
        ms = jnp.mean(x * x, axis=-1, keepdims=True)
        h_ref[...] = (x * lax.rsqrt(ms + EPS) * g_ref[...]).astype(BF16)
        o_ref[...] = x

    h = h_ref[...]
    gate_up = _dot(h, wgu_ref[...])
    tf = gate_up.shape[1] // 2
    gate, up = gate_up[:, 0:tf], gate_up[:, tf:2 * tf]
    act = (gate * jax.nn.sigmoid(gate) * up).astype(BF16)
    o_ref[...] += _dot(act, wd_ref[...])

    if final_norm:
        @pl.when(j == pl.num_programs(1) - 1)
        def _():
            y = o_ref[...]
            ms = jnp.mean(y * y, axis=-1, keepdims=True)
            o_ref[...] = y * lax.rsqrt(ms + EPS) * gf_ref[...]


def _ffn(x2, g, w_gu, w_down, g_final, tm, tf, final_norm, casts=()):
    t, d = x2.shape
    nf = D_FF // tf
    plans = [_cast_plan(cw, cl, (t // tm) * nf, *ct) for cw, cl, *ct in casts]
    c_in, c_out, c_shapes, c_args = _cast_specs(plans, lambda i, j: i * nf + j)
    return pl.pallas_call(
        functools.partial(_ffn_kernel, final_norm=final_norm, plans=plans),
        grid=(t // tm, nf),
        in_specs=[
            pl.BlockSpec((tm, d), lambda i, j: (i, 0)),
            pl.BlockSpec((1, d), lambda i, j: (0, 0)),
            pl.BlockSpec((d, 2 * tf), lambda i, j: (0, j)),
            pl.BlockSpec((tf, d), lambda i, j: (j, 0)),
            pl.BlockSpec((1, d), lambda i, j: (0, 0)),
        ] + c_in,
        out_specs=[pl.BlockSpec((tm, d), lambda i, j: (i, 0))] + c_out,
        out_shape=[jax.ShapeDtypeStruct((t, d), F32)] + c_shapes,
        scratch_shapes=[pltpu.VMEM((tm, d), BF16)],
        compiler_params=pltpu.CompilerParams(
            dimension_semantics=("arbitrary", "arbitrary"), vmem_limit_bytes=VMEM_LIMIT),
        name="ffn",
    )(x2, g, w_gu, w_down, g_final, *c_args)


class _Tiles(NamedTuple):
    tm: int
    tn_in: int
    tm_out: int
    tf: int


def _tiles(tokens):
    tm = 1024
    while tokens % tm:
        tm //= 2
    return _Tiles(tm=tm, tn_in=1536, tm_out=max(tm // 2, 8), tf=512)


def kernel(x, norm_mix, w_in, gla_a2, gla_ab, gla_norm, ml_conv, ml_ib, ml_fb, ml_norm,
           sgu_ln_g, sgu_ln_b, sgu_w, sgu_b, w_out, norm_ffn, w_gu, w_down, norm_final):
    batch, seq, d = x.shape
    depth = w_in.shape[0]
    tokens = batch * seq
    tiles = _tiles(tokens)

    lvl = jnp.asarray(_pair_level_matrix())
    ltri = jnp.asarray(np.tril(np.ones((CHUNK, CHUNK), np.float32)), BF16)

    w_in_t = jnp.swapaxes(w_in, 1, 2)
    w_in_b = _relayout_w_in(w_in_t, 0)

    xc = x.reshape(tokens, d)
    for l in range(depth):
        if l == 0:
            proj, gates, w_out_b, w_gu_b, w_down_b = _in_proj(
                xc, norm_mix[l][None, :], w_in_b, tiles.tm, tiles.tn_in, casts=[(w_out, 0), (w_gu, 0, tiles.tf), (w_down, 0)])
        else:
            proj, gates = _in_proj(xc, norm_mix[l][None, :], w_in_b, tiles.tm, tiles.tn_in)

        a2p = jnp.pad(_pad_heads(gla_a2[l]), ((0, LANE - GATE_RANK), (0, 0))).astype(BF16)
        abp = _pad_heads(gla_ab[l])[None, :]
        mix_a = _gla(proj, gates, a2p, abp, gla_norm[l][None, :], ltri, lvl, batch, seq)

        cw = jnp.concatenate([_pad_heads(ml_conv[l][:, :HEADS * DK]),
                              _pad_heads(ml_conv[l][:, HEADS * DK:])], axis=1)
        ifb = jnp.pad(jnp.concatenate([ml_ib[l], ml_fb[l]]), (0, LANE - 2 * HEADS))[None, :]
        mix_b = _mlstm(proj, gates, cw, ifb, ml_norm[l][None, :], ltri, lvl, batch, seq)

        b_full = jnp.repeat(sgu_b[l].T, SGU_CH, axis=1)
        mix_c = _sgu(proj, sgu_ln_g[l][None, :], sgu_ln_b[l][None, :], sgu_w[l], b_full, lvl, batch, seq)

        if l + 1 < depth:
            x1, w_in_b = _out_proj(xc, mix_a, mix_b, mix_c, w_out_b, tiles.tm_out, d, relayout=(w_in_t, l + 1))
        else:
            x1, = _out_proj(xc, mix_a, mix_b, mix_c, w_out_b, tiles.tm, d)
        nxt = [(w_out, l + 1), (w_gu, l + 1, tiles.tf), (w_down, l + 1)] if l + 1 < depth else []
        xc, *cast = _ffn(x1, norm_ffn[l][None, :], w_gu_b, w_down_b, norm_final[None, :], tiles.tm,
                         tiles.tf, final_norm=(l == depth - 1), casts=nxt)
        if cast:
            w_out_b, w_gu_b, w_down_b = cast
    return xc.reshape(batch, seq, d)
```
